```python
import math
import jax, jax.numpy as jnp
from jax import lax
import numpy as np

D_MODEL = 1024
BATCH = 8
SEQ = 2048
DEPTH = 2

HEAD_DIM = 64
MOBA_HEADS = (D_MODEL // 2) // HEAD_DIM
MOBA_WIDTH = MOBA_HEADS * HEAD_DIM
DIFF_HEADS = (D_MODEL // 2) // (2 * HEAD_DIM)
DIFF_V_DIM = 2 * HEAD_DIM
DIFF_WIDTH = DIFF_HEADS * DIFF_V_DIM
MIX_WIDTH = MOBA_WIDTH + DIFF_WIDTH
IN_WIDTH = 3 * MOBA_WIDTH + 3 * DIFF_WIDTH
MOBA_BLOCK = 256
MOBA_TOPK = 3
MOBA_Q_CHUNK = 64
DENSE_Q_BLOCK = 128
ROPE_THETA = 500000.0
ROPE_DIM = HEAD_DIM // 4
D_FF = -(-(8 * D_MODEL) // (3 * 256)) * 256
N_MOD = 6
EPS = 1e-6

kernel_name = 'hybrid_moba_diffattn_adaln_block'


def rms_norm(x, g):
    xf = x.astype(jnp.float32)
    y = xf * lax.rsqrt(jnp.mean(xf * xf, axis=-1, keepdims=True) + EPS)
    return (y * g.astype(jnp.float32)).astype(x.dtype)


def rope_tables(positions):
    inv = ROPE_THETA ** (-jnp.arange(0, ROPE_DIM, 2, dtype=jnp.float32) / ROPE_DIM)
    ang = positions.astype(jnp.float32)[..., None] * inv
    return jnp.cos(ang)[:, None], jnp.sin(ang)[:, None]


def apply_partial_rope(x, cos, sin):
    half = ROPE_DIM // 2
    xf = x.astype(jnp.float32)
    x1 = xf[..., :half]
    x2 = xf[..., half:ROPE_DIM]
    out = jnp.concatenate([x1 * cos - x2 * sin, x2 * cos + x1 * sin, xf[..., ROPE_DIM:]], axis=-1)
    return out.astype(x.dtype)


def moba_attention(q, k, v):
    B, H, S, dh = q.shape
    nb = -(-S // MOBA_BLOCK)
    pad = nb * MOBA_BLOCK - S
    kp = jnp.pad(k, ((0, 0), (0, 0), (0, pad), (0, 0)))
    vp = jnp.pad(v, ((0, 0), (0, 0), (0, pad), (0, 0)))
    kb = kp.reshape(B, H, nb, MOBA_BLOCK, dh)
    vb = vp.reshape(B, H, nb, MOBA_BLOCK, dh)
    k_mean = jnp.mean(kb.astype(jnp.float32), axis=3)
    topk = min(MOBA_TOPK, nb)
    n_chunks = S // MOBA_Q_CHUNK
    scale = dh ** -0.5
    b_ids = jnp.repeat(jnp.arange(B, dtype=jnp.int32), n_chunks)
    c_ids = jnp.tile(jnp.arange(n_chunks, dtype=jnp.int32), B)
    h_ids = jnp.arange(H)[:, None, None]
    blk_ids = jnp.arange(nb)

    def one_chunk(args):
        b, ci = args
        q0 = ci * MOBA_Q_CHUNK
        qblk = q0 // MOBA_BLOCK
        qpos = q0 + jnp.arange(MOBA_Q_CHUNK)
        qc = lax.dynamic_slice_in_dim(q[b], q0, MOBA_Q_CHUNK, axis=1)
        gate = jnp.einsum('hqd,hnd->hqn', qc.astype(jnp.float32), k_mean[b])
        gate = jnp.where(blk_ids < qblk, gate, -jnp.inf)
        _, idx = lax.top_k(gate, topk)
        sel_valid = idx < qblk
        k_sel = kb[b][h_ids, idx]
        v_sel = vb[b][h_ids, idx]
        s_sel = jnp.einsum('hqd,hqjkd->hqjk', qc, k_sel).astype(jnp.float32) * scale
        s_sel = jnp.where(sel_valid[..., None], s_sel, -jnp.inf)
        k_own = lax.dynamic_slice_in_dim(kp[b], qblk * MOBA_BLOCK, MOBA_BLOCK, axis=1)
        v_own = lax.dynamic_slice_in_dim(vp[b], qblk * MOBA_BLOCK, MOBA_BLOCK, axis=1)
        s_own = jnp.einsum('hqd,hkd->hqk', qc, k_own).astype(jnp.float32) * scale
        kpos = qblk * MOBA_BLOCK + jnp.arange(MOBA_BLOCK)
        s_own = jnp.where(kpos[None, :] <= qpos[:, None], s_own, -jnp.inf)
        s = jnp.concatenate([s_sel.reshape(H, MOBA_Q_CHUNK, topk * MOBA_BLOCK), s_own], axis=-1)
        p = jax.nn.softmax(s, axis=-1).astype(v.dtype)
        p_sel = p[..., :topk * MOBA_BLOCK].reshape(H, MOBA_Q_CHUNK, topk, MOBA_BLOCK)
        p_own = p[..., topk * MOBA_BLOCK:]
        return (jnp.einsum('hqjk,hqjkd->hqd', p_sel, v_sel)
                + jnp.einsum('hqk,hkd->hqd', p_own, v_own))

    out = lax.map(one_chunk, (b_ids, c_ids))
    out = out.reshape(B, n_chunks, H, MOBA_Q_CHUNK, dh).transpose(0, 2, 1, 3, 4)
    return out.reshape(B, H, S, dh)


def diff_attention(q, k, v, lam):
    B, H, _, S, dh = q.shape
    n_qb = S // DENSE_Q_BLOCK
    scale = dh ** -0.5
    kpos = jnp.arange(S)

    def one_block(ci):
        q0 = ci * DENSE_Q_BLOCK
        qc = lax.dynamic_slice_in_dim(q, q0, DENSE_Q_BLOCK, axis=3)
        s = jnp.einsum('bhcqd,bhckd->bhcqk', qc, k).astype(jnp.float32) * scale
        qpos = q0 + jnp.arange(DENSE_Q_BLOCK)
        s = jnp.where(kpos[None, :] <= qpos[:, None], s, -jnp.inf)
        a = jax.nn.softmax(s, axis=-1)
        w = (a[:, :, 0] - lam * a[:, :, 1]).astype(v.dtype)
        return jnp.einsum('bhqk,bhkd->bhqd', w, v)

    out = lax.map(one_block, jnp.arange(n_qb, dtype=jnp.int32))
    return out.transpose(1, 2, 0, 3, 4).reshape(B, H, S, v.shape[-1])


def setup_inputs(seed: int = 0) -> dict:
    key = jax.random.key(seed)
    ks = jax.random.split(key, 20)
    f32 = jnp.float32
    L = DEPTH

    def nrm(k, shape, scale):
        return jax.random.normal(k, shape, f32) * scale

    def gain(k, shape):
        return 1.0 + 0.02 * jax.random.normal(k, shape, f32)

    return {
        'x': nrm(ks[0], (BATCH, SEQ, D_MODEL), 1.0),
        'c': nrm(ks[1], (BATCH, D_MODEL), 1.0),
        'positions': jnp.broadcast_to(jnp.arange(SEQ, dtype=jnp.int32)[None, :], (BATCH, SEQ)),
        'w_mod': nrm(ks[2], (L, D_MODEL, N_MOD * D_MODEL), 0.5 * D_MODEL ** -0.5),
        'b_mod': nrm(ks[3], (L, N_MOD * D_MODEL), 0.01),
        'norm_mix': gain(ks[4], (L, D_MODEL)),
        'w_in': nrm(ks[5], (L, D_MODEL, IN_WIDTH), D_MODEL ** -0.5),
        'moba_q_norm': gain(ks[6], (L, HEAD_DIM)),
        'moba_k_norm': gain(ks[7], (L, HEAD_DIM)),
        'moba_out_norm': gain(ks[8], (L, HEAD_DIM)),
        'diff_q_norm': gain(ks[9], (L, HEAD_DIM)),
        'diff_k_norm': gain(ks[10], (L, HEAD_DIM)),
        'diff_lambda': nrm(ks[11], (L, 4, HEAD_DIM), 0.1),
        'diff_subln': gain(ks[12], (L, DIFF_V_DIM)),
        'w_out': nrm(ks[13], (L, MIX_WIDTH, D_MODEL), MIX_WIDTH ** -0.5),
        'norm_ffn': gain(ks[14], (L, D_MODEL)),
        'w_gate': nrm(ks[15], (L, D_MODEL, D_FF), D_MODEL ** -0.5),
        'w_up': nrm(ks[16], (L, D_MODEL, D_FF), D_MODEL ** -0.5),
        'w_down': nrm(ks[17], (L, D_FF, D_MODEL), D_FF ** -0.5),
    }


def reference(x, c, positions, w_mod, b_mod, norm_mix, w_in, moba_q_norm, moba_k_norm,
              moba_out_norm, diff_q_norm, diff_k_norm, diff_lambda, diff_subln, w_out,
              norm_ffn, w_gate, w_up, w_down):
    B, S, _ = x.shape
    cos, sin = rope_tables(positions)
    cond = jax.nn.silu(c)
    splits = [MOBA_WIDTH, 2 * MOBA_WIDTH, 3 * MOBA_WIDTH,
              3 * MOBA_WIDTH + DIFF_WIDTH, 3 * MOBA_WIDTH + 2 * DIFF_WIDTH]

    def heads(t, n, d):
        return t.reshape(B, S, n, d).transpose(0, 2, 1, 3)

    for l in range(DEPTH):
        mod = (cond @ w_mod[l] + b_mod[l])[:, None, :]
        sh_a, sc_a, g_a, sh_f, sc_f, g_f = jnp.split(mod, N_MOD, axis=-1)

        h = rms_norm(x, norm_mix[l]) * (1 + sc_a) + sh_a
        proj = h @ w_in[l]
        mq, mk, mv, dq, dk, dv = jnp.split(proj, splits, axis=-1)

        mq = apply_partial_rope(rms_norm(heads(mq, MOBA_HEADS, HEAD_DIM), moba_q_norm[l]), cos, sin)
        mk = apply_partial_rope(rms_norm(heads(mk, MOBA_HEADS, HEAD_DIM), moba_k_norm[l]), cos, sin)
        mv = heads(mv, MOBA_HEADS, HEAD_DIM)
        o_m = rms_norm(moba_attention(mq, mk, mv), moba_out_norm[l])
        o_m = o_m.transpose(0, 2, 1, 3).reshape(B, S, MOBA_WIDTH)

        dq = apply_partial_rope(rms_norm(heads(dq, 2 * DIFF_HEADS, HEAD_DIM), diff_q_norm[l]), cos, sin)
        dk = apply_partial_rope(rms_norm(heads(dk, 2 * DIFF_HEADS, HEAD_DIM), diff_k_norm[l]), cos, sin)
        dq = dq.reshape(B, DIFF_HEADS, 2, S, HEAD_DIM)
        dk = dk.reshape(B, DIFF_HEADS, 2, S, HEAD_DIM)
        dv = heads(dv, DIFF_HEADS, DIFF_V_DIM)
        lam_init = 0.8 - 0.6 * math.exp(-0.3 * l)
        lp = diff_lambda[l].astype(jnp.float32)
        lam = jnp.exp(jnp.sum(lp[0] * lp[1])) - jnp.exp(jnp.sum(lp[2] * lp[3])) + lam_init
        o_d = rms_norm(diff_attention(dq, dk, dv, lam), diff_subln[l]) * (1.0 - lam_init)
        o_d = o_d.transpose(0, 2, 1, 3).reshape(B, S, DIFF_WIDTH)

        x = x + g_a * (jnp.concatenate([o_m, o_d], axis=-1) @ w_out[l])

        h = rms_norm(x, norm_ffn[l]) * (1 + sc_f) + sh_f
        x = x + g_f * ((jax.nn.silu(h @ w_gate[l]) * (h @ w_up[l])) @ w_down[l])
    return x
```

```python
import functools
import math

import numpy as np
import jax
import jax.numpy as jnp
from jax import lax
from jax.experimental import pallas as pl
from jax.experimental.pallas import tpu as pltpu

F32 = jnp.float32
BF16 = jnp.bfloat16
HIGHEST = lax.Precision.HIGHEST

LANES = 128
MXU_DIM = 256
VMEM_LIMIT = 56 * 1024 * 1024

HEAD_DIM = 64
ROPE_DIM = HEAD_DIM // 4
ROPE_HALF = ROPE_DIM // 2
ROPE_THETA = 500000.0
MOBA_BLOCK = 256
MOBA_TOPK = 3
N_MOD = 6
EPS = 1e-6

ROW_TILE = 512
ATT_TILE = 256
MOD_COL_TILE = 1536


def _dot(a, b):
    return jnp.dot(a, b, preferred_element_type=F32)


def _dot_nt(a, b):
    return lax.dot_general(a, b, (((1,), (1,)), ((), ())), preferred_element_type=F32)


def _mod_kernel(c_ref, w_ref, b_ref, o_ref):
    c = c_ref[...]
    cond = c / (1.0 + jnp.exp(-c))
    o_ref[...] = jnp.dot(cond, w_ref[...], preferred_element_type=F32, precision=HIGHEST) + b_ref[...]


def _mod_call(c, w_mod, b_mod):
    depth, d_model, n_out = w_mod.shape
    batch = c.shape[0]
    return pl.pallas_call(
        _mod_kernel,
        grid=(depth, n_out // MOD_COL_TILE),
        in_specs=[
            pl.BlockSpec((batch, d_model), lambda l, j: (0, 0)),
            pl.BlockSpec((None, d_model, MOD_COL_TILE), lambda l, j: (l, 0, j)),
            pl.BlockSpec((None, 1, MOD_COL_TILE), lambda l, j: (l, 0, j)),
        ],
        out_specs=pl.BlockSpec((None, batch, MOD_COL_TILE), lambda l, j: (l, 0, j)),
        out_shape=jax.ShapeDtypeStruct((depth, batch, n_out), F32),
        compiler_params=pltpu.CompilerParams(
            dimension_semantics=("arbitrary", "arbitrary"), vmem_limit_bytes=VMEM_LIMIT),
        name="adaln_mod",
    )(c, w_mod, b_mod.reshape(depth, 1, n_out))


def _trig_kernel(lam_inits, pos_ref, inv_ref, dl_ref, cos_ref, sin_ref, lam_ref):
    ang = pos_ref[...].astype(F32) * inv_ref[...]
    cos_ref[...] = jnp.cos(ang)
    sin_ref[...] = jnp.sin(ang)
    for l, lam_init in enumerate(lam_inits):
        lp = dl_ref[l]
        a = jnp.sum(lp[0:1] * lp[1:2], axis=-1, keepdims=True)
        b = jnp.sum(lp[2:3] * lp[3:4], axis=-1, keepdims=True)
        lam_ref[l] = jnp.broadcast_to(jnp.exp(a) - jnp.exp(b) + lam_init, (1, LANES))


def _trig_call(positions, diff_lambda, lam_inits):
    batch, seq = positions.shape
    depth = diff_lambda.shape[0]
    inv = ROPE_THETA ** (-jnp.arange(0, ROPE_DIM, 2, dtype=F32) / ROPE_DIM)
    n_rows = batch * seq * ROPE_HALF // LANES
    pos_rep = jnp.broadcast_to(positions[..., None], (batch, seq, ROPE_HALF)).reshape(n_rows, LANES)
    inv_row = jnp.tile(inv, LANES // ROPE_HALF)[None, :]
    cos, sin, lam = pl.pallas_call(
        functools.partial(_trig_kernel, lam_inits),
        out_shape=(jax.ShapeDtypeStruct((n_rows, LANES), F32),
                   jax.ShapeDtypeStruct((n_rows, LANES), F32),
                   jax.ShapeDtypeStruct((depth, 1, LANES), F32)),
        name="rope_trig_lambda",
    )(pos_rep, inv_row, diff_lambda)
    cos = cos.reshape(batch, seq, ROPE_HALF)
    sin = sin.reshape(batch, seq, ROPE_HALF)
    ones = jnp.ones((batch, seq, HEAD_DIM - ROPE_DIM), F32)
    zeros = jnp.zeros((batch, seq, HEAD_DIM - ROPE_DIM), F32)
    zh = jnp.zeros_like(sin)
    per_head = [jnp.concatenate([cos, cos, ones], -1),
                jnp.concatenate([-sin, zh, zeros], -1),
                jnp.concatenate([zh, sin, zeros], -1)]
    tab = jnp.concatenate([jnp.tile(t, (1, 1, LANES // HEAD_DIM)) for t in per_head], -1)
    return tab, lam


def _adaln(x, norm_gain, scale, shift):
    ms = jnp.mean(x * x, axis=-1, keepdims=True)
    return (x * lax.rsqrt(ms + EPS)) * (norm_gain * (1.0 + scale)) + shift


def _inproj_kernel(x_ref, nm_ref, sc_ref, sh_ref, w_ref, gsum_ref, gains_ref, tab_ref, o_ref):
    h = _adaln(x_ref[...], nm_ref[...], sc_ref[...], sh_ref[...]).astype(BF16)
    tab = tab_ref[...]
    rope_c, rope_n, rope_p = tab[:, :LANES], tab[:, LANES:2 * LANES], tab[:, 2 * LANES:]
    group_sum = gsum_ref[...]
    n_chunks = w_ref.shape[1] // MXU_DIM
    gain_row = {0: 0, 1: 1, 3: 2, 4: 3}
    for c in range(n_chunks):
        cols = slice(c * MXU_DIM, (c + 1) * MXU_DIM)
        y = _dot(h, w_ref[:, cols])
        kind = c // 2
        if kind in gain_row:
            ssq = _dot((y * y).astype(BF16), group_sum)
            r = gain_row[kind]
            y = (y * lax.rsqrt(ssq * (1.0 / HEAD_DIM) + EPS)) * gains_ref[r:r + 1, :]
            for half in range(MXU_DIM // LANES):
                yh = y[:, half * LANES:(half + 1) * LANES]
                yh = (yh * rope_c + pltpu.roll(yh, LANES - ROPE_HALF, 1) * rope_n
                      + pltpu.roll(yh, ROPE_HALF, 1) * rope_p)
                lo = c * MXU_DIM + half * LANES
                o_ref[:, lo:lo + LANES] = yh.astype(BF16)
        else:
            o_ref[:, cols] = y.astype(BF16)


def _mod_spec(d_model, layer, k):
    return pl.BlockSpec((None, None, None, 1, d_model), lambda b, i: (layer, b, k, 0, 0))


def _inproj_call(x, mod5, layer, norm_gain, w_in, group_sum, gains, tab):
    batch, seq, d_model = x.shape
    n_out = w_in.shape[1]
    return pl.pallas_call(
        _inproj_kernel,
        grid=(batch, seq // ROW_TILE),
        in_specs=[
            pl.BlockSpec((None, ROW_TILE, d_model), lambda b, i: (b, i, 0)),
            pl.BlockSpec((1, d_model), lambda b, i: (0, 0)),
            _mod_spec(d_model, layer, 1),
            _mod_spec(d_model, layer, 0),
            pl.BlockSpec((d_model, n_out), lambda b, i: (0, 0)),
            pl.BlockSpec((MXU_DIM, MXU_DIM), lambda b, i: (0, 0)),
            pl.BlockSpec(gains.shape, lambda b, i: (0, 0)),
            pl.BlockSpec((None, ROW_TILE, 3 * LANES), lambda b, i: (b, i, 0)),
        ],
        out_specs=pl.BlockSpec((None, ROW_TILE, n_out), lambda b, i: (b, i, 0)),
        out_shape=jax.ShapeDtypeStruct((batch, seq, n_out), BF16),
        compiler_params=pltpu.CompilerParams(
            dimension_semantics=("parallel", "parallel"), vmem_limit_bytes=VMEM_LIMIT),
        name="in_projection",
    )(x, norm_gain, mod5, mod5, w_in, group_sum, gains, tab)


def _first_tile(qh, k, v):
    s = _dot_nt(qh, k)
    row = lax.broadcasted_iota(jnp.int32, s.shape, 0)
    col = lax.broadcasted_iota(jnp.int32, s.shape, 1)
    s = jnp.where(col <= row, s, -jnp.inf)
    m = jnp.max(s, axis=-1, keepdims=True)
    p = jnp.exp(s - m)
    l = jnp.sum(p, axis=-1, keepdims=True)
    return m, l, _dot(p.astype(BF16), v)


def _next_tile(carry, s, v):
    m, l, acc = carry
    m_new = jnp.maximum(m, jnp.max(s, axis=-1, keepdims=True))
    alpha = jnp.exp(m - m_new)
    p = jnp.exp(s - m_new)
    l = alpha * l + jnp.sum(p, axis=-1, keepdims=True)
    return m_new, l, alpha * acc + _dot(p.astype(BF16), v)


def _tile_rows(j):
    return pl.ds(pl.multiple_of(j * ATT_TILE, ATT_TILE), ATT_TILE)


def _moba_kernel(q_ref, k_ref, v_ref, gn_ref, o_ref, kmean_ref):
    i = pl.program_id(2)
    n_blocks = k_ref.shape[0] // MOBA_BLOCK
    lane = lax.broadcasted_iota(jnp.int32, (1, LANES), 1)
    first_head = lane < HEAD_DIM

    @pl.when(i == 0)
    def _():
        for j in range(n_blocks):
            kb = k_ref[j * MOBA_BLOCK:(j + 1) * MOBA_BLOCK, :].astype(F32)
            kmean_ref[j:j + 1, :] = jnp.mean(kb, axis=0, keepdims=True)

    q = q_ref[...]
    qf = q.astype(F32)
    kmean = kmean_ref[...]
    blk = lax.broadcasted_iota(jnp.int32, (ATT_TILE, n_blocks), 1)
    outs = []
    for head in range(LANES // HEAD_DIM):
        in_head = first_head if head == 0 else jnp.logical_not(first_head)
        qh = jnp.where(in_head, q, jnp.zeros_like(q))
        gate = lax.dot_general(jnp.where(in_head, qf, 0.0), kmean, (((1,), (1,)), ((), ())),
                               preferred_element_type=F32, precision=HIGHEST)
        cnt = jnp.zeros((ATT_TILE, n_blocks), jnp.int32)
        for m in range(n_blocks):
            gm = gate[:, m:m + 1]
            beats = jnp.logical_or(gm > gate, jnp.logical_and(gm == gate, m < blk))
            cnt = cnt + jnp.where(jnp.logical_and(beats, m < i), 1, 0)
        selected = jnp.logical_and(cnt < MOBA_TOPK, blk < i)
        bias = jnp.where(selected, 0.0, -jnp.inf)

        carry = _first_tile(qh, k_ref[_tile_rows(i), :], v_ref[_tile_rows(i), :])

        def past_block(j, carry, qh=qh, bias=bias):
            s = _dot_nt(qh, k_ref[_tile_rows(j), :])
            s = s + jnp.max(jnp.where(blk == j, bias, -jnp.inf), axis=-1, keepdims=True)
            return _next_tile(carry, s, v_ref[_tile_rows(j), :])

        _, l, acc = lax.fori_loop(0, i, past_block, carry)
        outs.append(acc / l)
    o = jnp.where(first_head, outs[0], outs[1])
    o2 = o * o
    ss_first = jnp.sum(jnp.where(first_head, o2, 0.0), axis=-1, keepdims=True)
    ss_all = jnp.sum(o2, axis=-1, keepdims=True)
    ss = jnp.where(first_head, ss_first, ss_all - ss_first)
    o_ref[...] = ((o * lax.rsqrt(ss * (1.0 / HEAD_DIM) + EPS)) * gn_ref[...]).astype(BF16)


def _moba_call(proj, out_gain, width):
    batch, seq, _ = proj.shape
    n_pairs = width // LANES
    return pl.pallas_call(
        _moba_kernel,
        grid=(batch, n_pairs, seq // ATT_TILE),
        in_specs=[
            pl.BlockSpec((None, ATT_TILE, LANES), lambda b, p, i: (b, i, p)),
            pl.BlockSpec((None, seq, LANES), lambda b, p, i: (b, 0, n_pairs + p)),
            pl.BlockSpec((None, seq, LANES), lambda b, p, i: (b, 0, 2 * n_pairs + p)),
            pl.BlockSpec((1, LANES), lambda b, p, i: (0, 0)),
        ],
        out_specs=pl.BlockSpec((None, ATT_TILE, LANES), lambda b, p, i: (b, i, p)),
        out_shape=jax.ShapeDtypeStruct((batch, seq, width), BF16),
        scratch_shapes=[pltpu.VMEM((seq // MOBA_BLOCK, LANES), F32)],
        compiler_params=pltpu.CompilerParams(
            dimension_semantics=("parallel", "parallel", "arbitrary"), vmem_limit_bytes=VMEM_LIMIT),
        name="moba_attention",
    )(proj, proj, proj, out_gain)


def _diff_kernel(out_scale, lam_ref, q_ref, k_ref, v_ref, gn_ref, o_ref):
    i = pl.program_id(2)
    lane = lax.broadcasted_iota(jnp.int32, (1, LANES), 1)
    first_comp = lane < HEAD_DIM
    q = q_ref[...]
    outs = []
    for comp in range(2):
        in_comp = first_comp if comp == 0 else jnp.logical_not(first_comp)
        qh = jnp.where(in_comp, q, jnp.zeros_like(q))
        carry = _first_tile(qh, k_ref[_tile_rows(i), :], v_ref[_tile_rows(i), :])

        def past_tile(j, carry, qh=qh):
            s = _dot_nt(qh, k_ref[_tile_rows(j), :])
            return _next_tile(carry, s, v_ref[_tile_rows(j), :])

        _, l, acc = lax.fori_loop(0, i, past_tile, carry)
        outs.append(acc / l)
    o = outs[0] - lam_ref[...] * outs[1]
    ms = jnp.mean(o * o, axis=-1, keepdims=True)
    o_ref[...] = (((o * lax.rsqrt(ms + EPS)) * gn_ref[...]) * out_scale).astype(BF16)


def _diff_call(proj, lam, layer, out_gain, width, col0, out_scale):
    batch, seq, _ = proj.shape
    n_heads = width // LANES
    c0 = col0 // LANES
    return pl.pallas_call(
        functools.partial(_diff_kernel, out_scale),
        grid=(batch, n_heads, seq // ATT_TILE),
        in_specs=[
            pl.BlockSpec((None, 1, LANES), lambda b, h, i: (layer, 0, 0)),
            pl.BlockSpec((None, ATT_TILE, LANES), lambda b, h, i: (b, i, c0 + h)),
            pl.BlockSpec((None, seq, LANES), lambda b, h, i: (b, 0, c0 + n_heads + h)),
            pl.BlockSpec((None, seq, LANES), lambda b, h, i: (b, 0, c0 + 2 * n_heads + h)),
            pl.BlockSpec((1, LANES), lambda b, h, i: (0, 0)),
        ],
        out_specs=pl.BlockSpec((None, ATT_TILE, LANES), lambda b, h, i: (b, i, h)),
        out_shape=jax.ShapeDtypeStruct((batch, seq, width), BF16),
        compiler_params=pltpu.CompilerParams(
            dimension_semantics=("parallel", "parallel", "arbitrary"), vmem_limit_bytes=VMEM_LIMIT),
        name="diff_attention",
    )(lam, proj, proj, proj, out_gain)


def _outproj_kernel(om_ref, od_ref, w_ref, x_ref, ga_ref, nf_ref, sc_ref, sh_ref, xo_ref, h_ref):
    width = om_ref.shape[1]
    y = _dot(om_ref[...], w_ref[:width, :]) + _dot(od_ref[...], w_ref[width:, :])
    x = x_ref[...] + ga_ref[...] * y
    xo_ref[...] = x
    h_ref[...] = _adaln(x, nf_ref[...], sc_ref[...], sh_ref[...]).astype(BF16)


def _outproj_call(o_m, o_d, w_out, x, mod5, layer, norm_gain):
    batch, seq, d_model = x.shape
    width = o_m.shape[2]
    row_spec = lambda cols: pl.BlockSpec((None, ROW_TILE, cols), lambda b, i: (b, i, 0))
    return pl.pallas_call(
        _outproj_kernel,
        grid=(batch, seq // ROW_TILE),
        in_specs=[
            row_spec(width), row_spec(width),
            pl.BlockSpec(w_out.shape, lambda b, i: (0, 0)),
            row_spec(d_model),
            _mod_spec(d_model, layer, 2),
            pl.BlockSpec((1, d_model), lambda b, i: (0, 0)),
            _mod_spec(d_model, layer, 4),
            _mod_spec(d_model, layer, 3),
        ],
        out_specs=(row_spec(d_model), row_spec(d_model)),
        out_shape=(jax.ShapeDtypeStruct((batch, seq, d_model), F32),
                   jax.ShapeDtypeStruct((batch, seq, d_model), BF16)),
        compiler_params=pltpu.CompilerParams(
            dimension_semantics=("parallel", "parallel"), vmem_limit_bytes=VMEM_LIMIT),
        name="out_projection",
    )(o_m, o_d, w_out, x, mod5, norm_gain, mod5, mod5)


def _ffn_kernel(h_ref, x_ref, gf_ref, wg_ref, wu_ref, wd_ref, o_ref, act_ref):
    h = h_ref[...]
    for c in range(wg_ref.shape[1] // MXU_DIM):
        cols = slice(c * MXU_DIM, (c + 1) * MXU_DIM)
        g = _dot(h, wg_ref[:, cols])
        u = _dot(h, wu_ref[:, cols])
        act_ref[:, cols] = ((g / (1.0 + jnp.exp(-g))) * u).astype(BF16)
    o_ref[...] = x_ref[...] + gf_ref[...] * _dot(act_ref[...], wd_ref[...])


def _ffn_call(h, x, mod5, layer, w_gate, w_up, w_down):
    batch, seq, d_model = x.shape
    d_ff = w_gate.shape[1]
    row_spec = pl.BlockSpec((None, ROW_TILE, d_model), lambda b, i: (b, i, 0))
    resident = lambda shape: pl.BlockSpec(shape, lambda b, i: (0, 0), pipeline_mode=pl.Buffered(1))
    return pl.pallas_call(
        _ffn_kernel,
        grid=(batch, seq // ROW_TILE),
        in_specs=[
            row_spec, row_spec,
            _mod_spec(d_model, layer, 5),
            resident(w_gate.shape), resident(w_up.shape), resident(w_down.shape),
        ],
        out_specs=row_spec,
        out_shape=jax.ShapeDtypeStruct((batch, seq, d_model), F32),
        scratch_shapes=[pltpu.VMEM((ROW_TILE, d_ff), BF16)],
        compiler_params=pltpu.CompilerParams(
            dimension_semantics=("parallel", "parallel"), vmem_limit_bytes=VMEM_LIMIT),
        name="swiglu_ffn",
    )(h, x, mod5, w_gate, w_up, w_down)


def kernel(x, c, positions, w_mod, b_mod, norm_mix, w_in, moba_q_norm, moba_k_norm, moba_out_norm,
           diff_q_norm, diff_k_norm, diff_lambda, diff_subln, w_out, norm_ffn, w_gate, w_up, w_down):
    batch, seq, d_model = x.shape
    depth = w_mod.shape[0]
    moba_width = d_model // 2
    diff_width = d_model // 2
    assert seq % ROW_TILE == 0 and seq % MOBA_BLOCK == 0 and ATT_TILE == MOBA_BLOCK
    assert w_in.shape[2] == 3 * moba_width + 3 * diff_width and moba_width == 2 * MXU_DIM

    lam_inits = tuple(0.8 - 0.6 * math.exp(-0.3 * l) for l in range(depth))
    mod5 = _mod_call(c, w_mod, b_mod).reshape(depth, batch, N_MOD, 1, d_model)
    tab, lam = _trig_call(positions, diff_lambda, lam_inits)

    head_of_lane = np.arange(MXU_DIM) // HEAD_DIM
    group_sum = jnp.asarray(head_of_lane[:, None] == head_of_lane[None, :], BF16)
    reps = MXU_DIM // HEAD_DIM
    qk_scale = HEAD_DIM ** -0.5

    w_in_b, w_out_b = w_in.astype(BF16), w_out.astype(BF16)
    w_gate_b, w_up_b, w_down_b = w_gate.astype(BF16), w_up.astype(BF16), w_down.astype(BF16)

    for l in range(depth):
        gains = jnp.stack([jnp.tile(moba_q_norm[l], reps) * qk_scale, jnp.tile(moba_k_norm[l], reps),
                           jnp.tile(diff_q_norm[l], reps) * qk_scale, jnp.tile(diff_k_norm[l], reps)])
        proj = _inproj_call(x, mod5, l, norm_mix[l][None, :], w_in_b[l], group_sum, gains, tab)
        o_m = _moba_call(proj, jnp.tile(moba_out_norm[l], LANES // HEAD_DIM)[None, :], moba_width)
        o_d = _diff_call(proj, lam, l, diff_subln[l][None, :], diff_width, 3 * moba_width,
                         1.0 - lam_inits[l])
        x, h = _outproj_call(o_m, o_d, w_out_b[l], x, mod5, l, norm_ffn[l][None, :])
        x = _ffn_call(h, x, mod5, l, w_gate_b[l], w_up_b[l], w_down_b[l])
    return x
```

```python
import functools
import math

import numpy as np
import jax
import jax.numpy as jnp
from jax import lax
from jax.experimental import pallas as pl
from jax.experimental.pallas import tpu as pltpu

F32 = jnp.float32
BF16 = jnp.bfloat16
HIGHEST = lax.Precision.HIGHEST

LANES = 128
MXU_DIM = 256
VMEM_LIMIT = 56 * 1024 * 1024

HEAD_DIM = 64
ROPE_DIM = HEAD_DIM // 4
ROPE_HALF = ROPE_DIM // 2
ROPE_THETA = 500000.0
MOBA_BLOCK = 256
MOBA_TOPK = 3
N_MOD = 6
EPS = 1e-6

ROW_TILE = 512
ATT_TILE = 256
MOD_COL_TILE = 1536


def _dot(a, b):
    return jnp.dot(a, b, preferred_element_type=F32)


def _dot_nt(a, b):
    return lax.dot_general(a, b, (((1,), (1,)), ((), ())), preferred_element_type=F32)


def _mod_kernel(c_ref, w_ref, b_ref, o_ref):
    c = c_ref[...]
    cond = c / (1.0 + jnp.exp(-c))
    o_ref[...] = jnp.dot(cond, w_ref[...], preferred_element_type=F32, precision=HIGHEST) + b_ref[...]


def _mod_call(c, w_mod, b_mod):
    depth, d_model, n_out = w_mod.shape
    batch = c.shape[0]
    return pl.pallas_call(
        _mod_kernel,
        grid=(depth, n_out // MOD_COL_TILE),
        in_specs=[
            pl.BlockSpec((batch, d_model), lambda l, j: (0, 0)),
            pl.BlockSpec((None, d_model, MOD_COL_TILE), lambda l, j: (l, 0, j)),
            pl.BlockSpec((None, 1, MOD_COL_TILE), lambda l, j: (l, 0, j)),
        ],
        out_specs=pl.BlockSpec((None, batch, MOD_COL_TILE), lambda l, j: (l, 0, j)),
        out_shape=jax.ShapeDtypeStruct((depth, batch, n_out), F32),
        compiler_params=pltpu.CompilerParams(
            dimension_semantics=("arbitrary", "arbitrary"), vmem_limit_bytes=VMEM_LIMIT),
        name="adaln_mod",
    )(c, w_mod, b_mod.reshape(depth, 1, n_out))


def _trig_kernel(lam_inits, pos_ref, inv_ref, dl_ref, cos_ref, sin_ref, lam_ref):
    ang = pos_ref[...].astype(F32) * inv_ref[...]
    cos_ref[...] = jnp.cos(ang)
    sin_ref[...] = jnp.sin(ang)
    for l, lam_init in enumerate(lam_inits):
        lp = dl_ref[l]
        a = jnp.sum(lp[0:1] * lp[1:2], axis=-1, keepdims=True)
        b = jnp.sum(lp[2:3] * lp[3:4], axis=-1, keepdims=True)
        lam_ref[l] = jnp.broadcast_to(jnp.exp(a) - jnp.exp(b) + lam_init, (1, LANES))


def _trig_call(positions, diff_lambda, lam_inits):
    batch, seq = positions.shape
    depth = diff_lambda.shape[0]
    inv = ROPE_THETA ** (-jnp.arange(0, ROPE_DIM, 2, dtype=F32) / ROPE_DIM)
    n_rows = batch * seq * ROPE_HALF // LANES
    pos_rep = jnp.broadcast_to(positions[..., None], (batch, seq, ROPE_HALF)).reshape(n_rows, LANES)
    inv_row = jnp.tile(inv, LANES // ROPE_HALF)[None, :]
    cos, sin, lam = pl.pallas_call(
        functools.partial(_trig_kernel, lam_inits),
        out_shape=(jax.ShapeDtypeStruct((n_rows, LANES), F32),
                   jax.ShapeDtypeStruct((n_rows, LANES), F32),
                   jax.ShapeDtypeStruct((depth, 1, LANES), F32)),
        name="rope_trig_lambda",
    )(pos_rep, inv_row, diff_lambda)
    cos = cos.reshape(batch, seq, ROPE_HALF)
    sin = sin.reshape(batch, seq, ROPE_HALF)
    ones = jnp.ones((batch, seq, HEAD_DIM - ROPE_DIM), F32)
    zeros = jnp.zeros((batch, seq, HEAD_DIM - ROPE_DIM), F32)
    zh = jnp.zeros_like(sin)
    per_head = [jnp.concatenate([cos, cos, ones], -1),
                jnp.concatenate([-sin, zh, zeros], -1),
                jnp.concatenate([zh, sin, zeros], -1)]
    tab = jnp.concatenate([jnp.tile(t, (1, 1, LANES // HEAD_DIM)) for t in per_head], -1)
    return tab, lam


def _adaln(x, norm_gain, scale, shift):
    ms = jnp.mean(x * x, axis=-1, keepdims=True)
    return (x * lax.rsqrt(ms + EPS)) * (norm_gain * (1.0 + scale)) + shift


def _inproj_kernel(x_ref, nm_ref, sc_ref, sh_ref, w_ref, gsum_ref, gains_ref, tab_ref, o_ref):
    h = _adaln(x_ref[...], nm_ref[...], sc_ref[...], sh_ref[...]).astype(BF16)
    tab = tab_ref[...]
    rope_c, rope_n, rope_p = tab[:, :LANES], tab[:, LANES:2 * LANES], tab[:, 2 * LANES:]
    group_sum = gsum_ref[...]
    n_chunks = w_ref.shape[1] // MXU_DIM
    gain_row = {0: 0, 1: 1, 3: 2, 4: 3}
    for c in range(n_chunks):
        cols = slice(c * MXU_DIM, (c + 1) * MXU_DIM)
        y = _dot(h, w_ref[:, cols])
        kind = c // 2
        if kind in gain_row:
            ssq = _dot((y * y).astype(BF16), group_sum)
            r = gain_row[kind]
            y = (y * lax.rsqrt(ssq * (1.0 / HEAD_DIM) + EPS)) * gains_ref[r:r + 1, :]
            for half in range(MXU_DIM // LANES):
                yh = y[:, half * LANES:(half + 1) * LANES]
                yh = (yh * rope_c + pltpu.roll(yh, LANES - ROPE_HALF, 1) * rope_n
                      + pltpu.roll(yh, ROPE_HALF, 1) * rope_p)
                lo = c * MXU_DIM + half * LANES
                o_ref[:, lo:lo + LANES] = yh.astype(BF16)
        else:
            o_ref[:, cols] = y.astype(BF16)


def _mod_spec(d_model, layer, k):
    return pl.BlockSpec((None, None, None, 1, d_model), lambda b, i: (layer, b, k, 0, 0))


def _inproj_call(x, mod5, layer, norm_gain, w_in, group_sum, gains, tab):
    batch, seq, d_model = x.shape
    n_out = w_in.shape[1]
    return pl.pallas_call(
        _inproj_kernel,
        grid=(batch, seq // ROW_TILE),
        in_specs=[
            pl.BlockSpec((None, ROW_TILE, d_model), lambda b, i: (b, i, 0)),
            pl.BlockSpec((1, d_model), lambda b, i: (0, 0)),
            _mod_spec(d_model, layer, 1),
            _mod_spec(d_model, layer, 0),
            pl.BlockSpec((d_model, n_out), lambda b, i: (0, 0)),
            pl.BlockSpec((MXU_DIM, MXU_DIM), lambda b, i: (0, 0)),
            pl.BlockSpec(gains.shape, lambda b, i: (0, 0)),
            pl.BlockSpec((None, ROW_TILE, 3 * LANES), lambda b, i: (b, i, 0)),
        ],
        out_specs=pl.BlockSpec((None, ROW_TILE, n_out), lambda b, i: (b, i, 0)),
        out_shape=jax.ShapeDtypeStruct((batch, seq, n_out), BF16),
        compiler_params=pltpu.CompilerParams(
            dimension_semantics=("parallel", "parallel"), vmem_limit_bytes=VMEM_LIMIT),
        name="in_projection",
    )(x, norm_gain, mod5, mod5, w_in, group_sum, gains, tab)


MASKED = -1e30


def _causal_softmax_numerator(s, n_past):
    s_own = s[:, n_past:]
    row = lax.broadcasted_iota(jnp.int32, s_own.shape, 0)
    col = lax.broadcasted_iota(jnp.int32, s_own.shape, 1)
    s_own = jnp.where(col <= row, s_own, -jnp.inf)
    m = jnp.max(s_own, axis=-1, keepdims=True)
    if n_past == 0:
        p = jnp.exp(s_own - m)
    else:
        s_past = s[:, :n_past]
        m = jnp.maximum(m, jnp.max(s_past, axis=-1, keepdims=True))
        p = jnp.concatenate([jnp.exp(s_past - m), jnp.exp(s_own - m)], axis=1)
    return p, jnp.sum(p, axis=-1, keepdims=True)


def _lane_halves():
    lane = lax.broadcasted_iota(jnp.int32, (1, LANES), 1)
    first = lane < HEAD_DIM
    return first, jnp.logical_not(first)


def _moba_prepare(k_ref, kaug_ref, kmean_ref):
    seq = k_ref.shape[0]
    n_blocks = seq // MOBA_BLOCK
    kaug_ref[:, :LANES] = k_ref[...]
    row_block = lax.broadcasted_iota(jnp.int32, (seq, LANES), 0) // MOBA_BLOCK
    lane = lax.broadcasted_iota(jnp.int32, (seq, LANES), 1)
    kaug_ref[:, LANES:] = jnp.where(row_block == lane, 1.0, 0.0).astype(BF16)
    means = [jnp.mean(k_ref[j * MOBA_BLOCK:(j + 1) * MOBA_BLOCK, :].astype(F32), axis=0, keepdims=True)
             for j in range(n_blocks)]
    means = jnp.concatenate(means + [jnp.zeros((LANES - n_blocks, LANES), F32)], axis=0)
    hi = means.astype(BF16)
    lo = (means - hi.astype(F32)).astype(BF16)
    kmean_ref[...] = jnp.concatenate([hi, lo], axis=1)


def _moba_tile(c, q_ref, k_ref, v_ref, gn_ref, o_ref, kaug_ref, kmean_ref):
    n_past = c * ATT_TILE
    n_kv = n_past + ATT_TILE
    first_head, second_head = _lane_halves()
    q = q_ref[...]
    v = v_ref[:n_kv, :]
    outs = []
    for in_head in (first_head, second_head):
        qh = jnp.where(in_head, q, jnp.zeros_like(q))
        if c > MOBA_TOPK:
            gate = _dot_nt(jnp.concatenate([qh, qh], axis=1), kmean_ref[...])
            blk = lax.broadcasted_iota(jnp.int32, gate.shape, 1)
            cnt = jnp.zeros(gate.shape, jnp.int32)
            for m in range(c):
                gm = gate[:, m:m + 1]
                cnt = cnt + jnp.where(blk > m, jnp.where(gm >= gate, 1, 0), jnp.where(gm > gate, 1, 0))
            bias = jnp.where(blk >= c, 0.0, jnp.where(cnt < MOBA_TOPK, 0.0, MASKED)).astype(BF16)
            s = _dot_nt(jnp.concatenate([qh, bias], axis=1), kaug_ref[:n_kv, :])
        else:
            s = _dot_nt(qh, k_ref[:n_kv, :])
        p, l = _causal_softmax_numerator(s, n_past)
        outs.append(_dot(p.astype(BF16), v) / l)
    o = jnp.where(first_head, outs[0], outs[1])
    o2 = o * o
    ss_first = jnp.sum(jnp.where(first_head, o2, 0.0), axis=-1, keepdims=True)
    ss_all = jnp.sum(o2, axis=-1, keepdims=True)
    ss = jnp.where(first_head, ss_first, ss_all - ss_first)
    o_ref[...] = ((o * lax.rsqrt(ss * (1.0 / HEAD_DIM) + EPS)) * gn_ref[...]).astype(BF16)


def _moba_kernel(q_ref, k_ref, v_ref, gn_ref, o_ref, kaug_ref, kmean_ref):
    i = pl.program_id(2)
    pl.when(i == 0)(functools.partial(_moba_prepare, k_ref, kaug_ref, kmean_ref))
    for c in range(k_ref.shape[0] // ATT_TILE):
        pl.when(i == c)(functools.partial(_moba_tile, c, q_ref, k_ref, v_ref, gn_ref, o_ref,
                                          kaug_ref, kmean_ref))


def _moba_call(proj, out_gain, width):
    batch, seq, _ = proj.shape
    n_pairs = width // LANES
    return pl.pallas_call(
        _moba_kernel,
        grid=(batch, n_pairs, seq // ATT_TILE),
        in_specs=[
            pl.BlockSpec((None, ATT_TILE, LANES), lambda b, p, i: (b, i, p)),
            pl.BlockSpec((None, seq, LANES), lambda b, p, i: (b, 0, n_pairs + p)),
            pl.BlockSpec((None, seq, LANES), lambda b, p, i: (b, 0, 2 * n_pairs + p)),
            pl.BlockSpec((1, LANES), lambda b, p, i: (0, 0)),
        ],
        out_specs=pl.BlockSpec((None, ATT_TILE, LANES), lambda b, p, i: (b, i, p)),
        out_shape=jax.ShapeDtypeStruct((batch, seq, width), BF16),
        scratch_shapes=[pltpu.VMEM((seq, 2 * LANES), BF16), pltpu.VMEM((LANES, 2 * LANES), BF16)],
        compiler_params=pltpu.CompilerParams(
            dimension_semantics=("parallel", "parallel", "arbitrary"), vmem_limit_bytes=VMEM_LIMIT),
        name="moba_attention",
    )(proj, proj, proj, out_gain)


def _diff_tile(c, out_scale, lam_ref, q_ref, k_ref, v_ref, gn_ref, o_ref):
    n_past = c * ATT_TILE
    n_kv = n_past + ATT_TILE
    q = q_ref[...]
    k = k_ref[:n_kv, :]
    lam = lam_ref[0:1, 0:1]
    w = None
    for comp, in_comp in enumerate(_lane_halves()):
        qh = jnp.where(in_comp, q, jnp.zeros_like(q))
        p, l = _causal_softmax_numerator(_dot_nt(qh, k), n_past)
        w = p * (1.0 / l) if comp == 0 else w - p * (lam / l)
    o = _dot(w.astype(BF16), v_ref[:n_kv, :])
    ms = jnp.mean(o * o, axis=-1, keepdims=True)
    o_ref[...] = (((o * lax.rsqrt(ms + EPS)) * gn_ref[...]) * out_scale).astype(BF16)


def _diff_kernel(out_scale, lam_ref, q_ref, k_ref, v_ref, gn_ref, o_ref):
    i = pl.program_id(2)
    for c in range(k_ref.shape[0] // ATT_TILE):
        pl.when(i == c)(functools.partial(_diff_tile, c, out_scale, lam_ref, q_ref, k_ref, v_ref,
                                          gn_ref, o_ref))


def _diff_call(proj, lam, layer, out_gain, width, col0, out_scale):
    batch, seq, _ = proj.shape
    n_heads = width // LANES
    c0 = col0 // LANES
    return pl.pallas_call(
        functools.partial(_diff_kernel, out_scale),
        grid=(batch, n_heads, seq // ATT_TILE),
        in_specs=[
            pl.BlockSpec((None, 1, LANES), lambda b, h, i: (layer, 0, 0)),
            pl.BlockSpec((None, ATT_TILE, LANES), lambda b, h, i: (b, i, c0 + h)),
            pl.BlockSpec((None, seq, LANES), lambda b, h, i: (b, 0, c0 + n_heads + h)),
            pl.BlockSpec((None, seq, LANES), lambda b, h, i: (b, 0, c0 + 2 * n_heads + h)),
            pl.BlockSpec((1, LANES), lambda b, h, i: (0, 0)),
        ],
        out_specs=pl.BlockSpec((None, ATT_TILE, LANES), lambda b, h, i: (b, i, h)),
        out_shape=jax.ShapeDtypeStruct((batch, seq, width), BF16),
        compiler_params=pltpu.CompilerParams(
            dimension_semantics=("parallel", "parallel", "arbitrary"), vmem_limit_bytes=VMEM_LIMIT),
        name="diff_attention",
    )(lam, proj, proj, proj, out_gain)


def _outproj_kernel(om_ref, od_ref, w_ref, x_ref, ga_ref, nf_ref, sc_ref, sh_ref, xo_ref, h_ref):
    width = om_ref.shape[1]
    y = _dot(om_ref[...], w_ref[:width, :]) + _dot(od_ref[...], w_ref[width:, :])
    x = x_ref[...] + ga_ref[...] * y
    xo_ref[...] = x
    h_ref[...] = _adaln(x, nf_ref[...], sc_ref[...], sh_ref[...]).astype(BF16)


def _outproj_call(o_m, o_d, w_out, x, mod5, layer, norm_gain):
    batch, seq, d_model = x.shape
    width = o_m.shape[2]
    row_spec = lambda cols: pl.BlockSpec((None, ROW_TILE, cols), lambda b, i: (b, i, 0))
    return pl.pallas_call(
        _outproj_kernel,
        grid=(batch, seq // ROW_TILE),
        in_specs=[
            row_spec(width), row_spec(width),
            pl.BlockSpec(w_out.shape, lambda b, i: (0, 0)),
            row_spec(d_model),
            _mod_spec(d_model, layer, 2),
            pl.BlockSpec((1, d_model), lambda b, i: (0, 0)),
            _mod_spec(d_model, layer, 4),
            _mod_spec(d_model, layer, 3),
        ],
        out_specs=(row_spec(d_model), row_spec(d_model)),
        out_shape=(jax.ShapeDtypeStruct((batch, seq, d_model), F32),
                   jax.ShapeDtypeStruct((batch, seq, d_model), BF16)),
        compiler_params=pltpu.CompilerParams(
            dimension_semantics=("parallel", "parallel"), vmem_limit_bytes=VMEM_LIMIT),
        name="out_projection",
    )(o_m, o_d, w_out, x, mod5, norm_gain, mod5, mod5)


def _ffn_kernel(h_ref, x_ref, gf_ref, wg_ref, wu_ref, wd_ref, o_ref, act_ref):
    h = h_ref[...]
    for c in range(wg_ref.shape[1] // MXU_DIM):
        cols = slice(c * MXU_DIM, (c + 1) * MXU_DIM)
        g = _dot(h, wg_ref[:, cols])
        u = _dot(h, wu_ref[:, cols])
        act_ref[:, cols] = ((g / (1.0 + jnp.exp(-g))) * u).astype(BF16)
    o_ref[...] = x_ref[...] + gf_ref[...] * _dot(act_ref[...], wd_ref[...])


def _ffn_call(h, x, mod5, layer, w_gate, w_up, w_down):
    batch, seq, d_model = x.shape
    d_ff = w_gate.shape[1]
    row_spec = pl.BlockSpec((None, ROW_TILE, d_model), lambda b, i: (b, i, 0))
    resident = lambda shape: pl.BlockSpec(shape, lambda b, i: (0, 0), pipeline_mode=pl.Buffered(1))
    return pl.pallas_call(
        _ffn_kernel,
        grid=(batch, seq // ROW_TILE),
        in_specs=[
            row_spec, row_spec,
            _mod_spec(d_model, layer, 5),
            resident(w_gate.shape), resident(w_up.shape), resident(w_down.shape),
        ],
        out_specs=row_spec,
        out_shape=jax.ShapeDtypeStruct((batch, seq, d_model), F32),
        scratch_shapes=[pltpu.VMEM((ROW_TILE, d_ff), BF16)],
        compiler_params=pltpu.CompilerParams(
            dimension_semantics=("parallel", "parallel"), vmem_limit_bytes=VMEM_LIMIT),
        name="swiglu_ffn",
    )(h, x, mod5, w_gate, w_up, w_down)


def kernel(x, c, positions, w_mod, b_mod, norm_mix, w_in, moba_q_norm, moba_k_norm, moba_out_norm,
           diff_q_norm, diff_k_norm, diff_lambda, diff_subln, w_out, norm_ffn, w_gate, w_up, w_down):
    batch, seq, d_model = x.shape
    depth = w_mod.shape[0]
    moba_width = d_model // 2
    diff_width = d_model // 2
    assert seq % ROW_TILE == 0 and seq % MOBA_BLOCK == 0 and ATT_TILE == MOBA_BLOCK
    assert w_in.shape[2] == 3 * moba_width + 3 * diff_width and moba_width == 2 * MXU_DIM

    lam_inits = tuple(0.8 - 0.6 * math.exp(-0.3 * l) for l in range(depth))
    mod5 = _mod_call(c, w_mod, b_mod).reshape(depth, batch, N_MOD, 1, d_model)
    tab, lam = _trig_call(positions, diff_lambda, lam_inits)

    head_of_lane = np.arange(MXU_DIM) // HEAD_DIM
    group_sum = jnp.asarray(head_of_lane[:, None] == head_of_lane[None, :], BF16)
    reps = MXU_DIM // HEAD_DIM
    qk_scale = HEAD_DIM ** -0.5

    w_in_b, w_out_b = w_in.astype(BF16), w_out.astype(BF16)
    w_gate_b, w_up_b, w_down_b = w_gate.astype(BF16), w_up.astype(BF16), w_down.astype(BF16)

    for l in range(depth):
        gains = jnp.stack([jnp.tile(moba_q_norm[l], reps) * qk_scale, jnp.tile(moba_k_norm[l], reps),
                           jnp.tile(diff_q_norm[l], reps) * qk_scale, jnp.tile(diff_k_norm[l], reps)])
        proj = _inproj_call(x, mod5, l, norm_mix[l][None, :], w_in_b[l], group_sum, gains, tab)
        o_m = _moba_call(proj, jnp.tile(moba_out_norm[l], LANES // HEAD_DIM)[None, :], moba_width)
        o_d = _diff_call(proj, lam, l, diff_subln[l][None, :], diff_width, 3 * moba_width,
                         1.0 - lam_inits[l])
        x, h = _outproj_call(o_m, o_d, w_out_b[l], x, mod5, l, norm_ffn[l][None, :])
        x = _ffn_call(h, x, mod5, l, w_gate_b[l], w_up_b[l], w_down_b[l])
    return x
```

```python
import functools
import math

import numpy as np
import jax
import jax.numpy as jnp
from jax import lax
from jax.experimental import pallas as pl
from jax.experimental.pallas import tpu as pltpu

F32 = jnp.float32
BF16 = jnp.bfloat16
HIGHEST = lax.Precision.HIGHEST

LANES = 128
MXU_DIM = 256
VMEM_LIMIT = 56 * 1024 * 1024

HEAD_DIM = 64
ROPE_DIM = HEAD_DIM // 4
ROPE_HALF = ROPE_DIM // 2
ROPE_THETA = 500000.0
MOBA_BLOCK = 256
MOBA_TOPK = 3
N_MOD = 6
EPS = 1e-6

ROW_TILE = 512
ATT_TILE = 256
MOD_COL_TILE = 1536


def _dot(a, b):
    return jnp.dot(a, b, preferred_element_type=F32)


def _dot_nt(a, b):
    return lax.dot_general(a, b, (((1,), (1,)), ((), ())), preferred_element_type=F32)


def _mod_kernel(c_ref, w_ref, b_ref, o_ref):
    c = c_ref[...]
    cond = c / (1.0 + jnp.exp(-c))
    o_ref[...] = jnp.dot(cond, w_ref[...], preferred_element_type=F32, precision=HIGHEST) + b_ref[...]


def _mod_call(c, w_mod, b_mod):
    depth, d_model, n_out = w_mod.shape
    batch = c.shape[0]
    return pl.pallas_call(
        _mod_kernel,
        grid=(depth, n_out // MOD_COL_TILE),
        in_specs=[
            pl.BlockSpec((batch, d_model), lambda l, j: (0, 0)),
            pl.BlockSpec((None, d_model, MOD_COL_TILE), lambda l, j: (l, 0, j)),
            pl.BlockSpec((None, 1, MOD_COL_TILE), lambda l, j: (l, 0, j)),
        ],
        out_specs=pl.BlockSpec((None, batch, MOD_COL_TILE), lambda l, j: (l, 0, j)),
        out_shape=jax.ShapeDtypeStruct((depth, batch, n_out), F32),
        compiler_params=pltpu.CompilerParams(
            dimension_semantics=("arbitrary", "arbitrary"), vmem_limit_bytes=VMEM_LIMIT),
        name="adaln_mod",
    )(c, w_mod, b_mod.reshape(depth, 1, n_out))


def _trig_kernel(lam_inits, pos_ref, inv_ref, dl_ref, cos_ref, sin_ref, lam_ref):
    ang = pos_ref[...].astype(F32) * inv_ref[...]
    cos_ref[...] = jnp.cos(ang)
    sin_ref[...] = jnp.sin(ang)
    for l, lam_init in enumerate(lam_inits):
        lp = dl_ref[l]
        a = jnp.sum(lp[0:1] * lp[1:2], axis=-1, keepdims=True)
        b = jnp.sum(lp[2:3] * lp[3:4], axis=-1, keepdims=True)
        lam_ref[l] = jnp.broadcast_to(jnp.exp(a) - jnp.exp(b) + lam_init, (1, LANES))


def _trig_call(positions, diff_lambda, lam_inits):
    batch, seq = positions.shape
    depth = diff_lambda.shape[0]
    inv = ROPE_THETA ** (-jnp.arange(0, ROPE_DIM, 2, dtype=F32) / ROPE_DIM)
    n_rows = batch * seq * ROPE_HALF // LANES
    pos_rep = jnp.broadcast_to(positions[..., None], (batch, seq, ROPE_HALF)).reshape(n_rows, LANES)
    inv_row = jnp.tile(inv, LANES // ROPE_HALF)[None, :]
    cos, sin, lam = pl.pallas_call(
        functools.partial(_trig_kernel, lam_inits),
        out_shape=(jax.ShapeDtypeStruct((n_rows, LANES), F32),
                   jax.ShapeDtypeStruct((n_rows, LANES), F32),
                   jax.ShapeDtypeStruct((depth, 1, LANES), F32)),
        name="rope_trig_lambda",
    )(pos_rep, inv_row, diff_lambda)
    cos = cos.reshape(batch, seq, ROPE_HALF)
    sin = sin.reshape(batch, seq, ROPE_HALF)
    ones = jnp.ones((batch, seq, HEAD_DIM - ROPE_DIM), F32)
    zeros = jnp.zeros((batch, seq, HEAD_DIM - ROPE_DIM), F32)
    zh = jnp.zeros_like(sin)
    per_head = [jnp.concatenate([cos, cos, ones], -1),
                jnp.concatenate([-sin, zh, zeros], -1),
                jnp.concatenate([zh, sin, zeros], -1)]
    tab = jnp.concatenate([jnp.tile(t, (1, 1, LANES // HEAD_DIM)) for t in per_head], -1)
    return tab, lam


def _adaln(x, norm_gain, scale, shift):
    ms = jnp.mean(x * x, axis=-1, keepdims=True)
    return (x * lax.rsqrt(ms + EPS)) * (norm_gain * (1.0 + scale)) + shift


def _inproj_kernel(x_ref, nm_ref, sc_ref, sh_ref, w_ref, gsum_ref, gains_ref, tab_ref, o_ref):
    h = _adaln(x_ref[...], nm_ref[...], sc_ref[...], sh_ref[...]).astype(BF16)
    tab = tab_ref[...]
    rope_c, rope_n, rope_p = tab[:, :LANES], tab[:, LANES:2 * LANES], tab[:, 2 * LANES:]
    group_sum = gsum_ref[...]
    n_chunks = w_ref.shape[1] // MXU_DIM
    gain_row = {0: 0, 1: 1, 3: 2, 4: 3}
    for c in range(n_chunks):
        cols = slice(c * MXU_DIM, (c + 1) * MXU_DIM)
        y = _dot(h, w_ref[:, cols])
        kind = c // 2
        if kind in gain_row:
            ssq = _dot((y * y).astype(BF16), group_sum)
            r = gain_row[kind]
            y = (y * lax.rsqrt(ssq * (1.0 / HEAD_DIM) + EPS)) * gains_ref[r:r + 1, :]
            for half in range(MXU_DIM // LANES):
                yh = y[:, half * LANES:(half + 1) * LANES]
                yh = (yh * rope_c + pltpu.roll(yh, LANES - ROPE_HALF, 1) * rope_n
                      + pltpu.roll(yh, ROPE_HALF, 1) * rope_p)
                lo = c * MXU_DIM + half * LANES
                o_ref[:, lo:lo + LANES] = yh.astype(BF16)
        else:
            o_ref[:, cols] = y.astype(BF16)


def _mod_spec(d_model, layer, k):
    return pl.BlockSpec((None, None, None, 1, d_model), lambda b, i: (layer, b, k, 0, 0))


def _inproj_call(x, mod5, layer, norm_gain, w_in, group_sum, gains, tab):
    batch, seq, d_model = x.shape
    n_out = w_in.shape[1]
    return pl.pallas_call(
        _inproj_kernel,
        grid=(batch, seq // ROW_TILE),
        in_specs=[
            pl.BlockSpec((None, ROW_TILE, d_model), lambda b, i: (b, i, 0)),
            pl.BlockSpec((1, d_model), lambda b, i: (0, 0)),
            _mod_spec(d_model, layer, 1),
            _mod_spec(d_model, layer, 0),
            pl.BlockSpec((d_model, n_out), lambda b, i: (0, 0)),
            pl.BlockSpec((MXU_DIM, MXU_DIM), lambda b, i: (0, 0)),
            pl.BlockSpec(gains.shape, lambda b, i: (0, 0)),
            pl.BlockSpec((None, ROW_TILE, 3 * LANES), lambda b, i: (b, i, 0)),
        ],
        out_specs=pl.BlockSpec((None, ROW_TILE, n_out), lambda b, i: (b, i, 0)),
        out_shape=jax.ShapeDtypeStruct((batch, seq, n_out), BF16),
        compiler_params=pltpu.CompilerParams(
            dimension_semantics=("parallel", "parallel"), vmem_limit_bytes=VMEM_LIMIT),
        name="in_projection",
    )(x, norm_gain, mod5, mod5, w_in, group_sum, gains, tab)


MASKED = -1e30


def _store_scores(s_ref, c, s_past, s_own):
    n_past = c * ATT_TILE
    row = lax.broadcasted_iota(jnp.int32, s_own.shape, 0)
    col = lax.broadcasted_iota(jnp.int32, s_own.shape, 1)
    s_ref[:, n_past:n_past + ATT_TILE] = jnp.where(col <= row, s_own, -jnp.inf)
    if s_past is not None:
        s_ref[:, :n_past] = s_past


def _softmax_numerator(s_ref, p_ref, c):
    n_kv = (c + 1) * ATT_TILE
    m = jnp.max(s_ref[:, :n_kv], axis=-1, keepdims=True)
    p = jnp.exp2(s_ref[:, :n_kv] - m)
    p_ref[:, :n_kv] = p.astype(BF16)
    return jnp.sum(p, axis=-1, keepdims=True)


def _staged(n_tiles, scores_fn, finish_fn, prepare_fn=None):
    def stage(c):
        if c + 1 < n_tiles:
            scores_fn(c + 1)
        if c >= 0:
            finish_fn(c)

    if prepare_fn is not None:
        prepare_fn()
    for c in range(-1, n_tiles):
        stage(c)


def _lane_halves():
    lane = lax.broadcasted_iota(jnp.int32, (1, LANES), 1)
    first = lane < HEAD_DIM
    return first, jnp.logical_not(first)


def _tile_rows(c):
    return slice(c * ATT_TILE, (c + 1) * ATT_TILE)


KMEAN_ROWS = 16
GATE_ROWS = 8


def _moba_prepare(k_ref, kaug_ref, kmean_ref):
    seq = k_ref.shape[0]
    n_blocks = seq // MOBA_BLOCK
    kaug_ref[:, :LANES] = k_ref[...]
    row_block = lax.broadcasted_iota(jnp.int32, (seq, LANES), 0) // MOBA_BLOCK
    lane = lax.broadcasted_iota(jnp.int32, (seq, LANES), 1)
    kaug_ref[:, LANES:] = jnp.where(row_block == lane, 1.0, 0.0).astype(BF16)
    means = [jnp.mean(k_ref[_tile_rows(j), :].astype(F32), axis=0, keepdims=True) for j in range(n_blocks)]
    means = jnp.concatenate(means + [jnp.zeros((KMEAN_ROWS - n_blocks, LANES), F32)], axis=0)
    hi = means.astype(BF16)
    lo = (means - hi.astype(F32)).astype(BF16)
    kmean_ref[...] = jnp.concatenate([hi, lo], axis=1)


def _moba_block_bias(c, qh, kmean_ref):
    gate = _dot_nt(kmean_ref[...], jnp.concatenate([qh, qh], axis=1))[:GATE_ROWS, :]
    blk = lax.broadcasted_iota(jnp.int32, gate.shape, 0)
    cnt = jnp.zeros(gate.shape, jnp.int32)
    for m in range(c):
        gm = gate[m:m + 1, :]
        cnt = cnt + jnp.where(blk > m, jnp.where(gm >= gate, 1, 0), jnp.where(gm > gate, 1, 0))
    bias_t = jnp.where(cnt < MOBA_TOPK, 0.0, MASKED)
    bias_t = jnp.concatenate([bias_t, jnp.zeros((LANES - GATE_ROWS, gate.shape[1]), F32)], axis=0)
    return bias_t.T.astype(BF16)


def _moba_scores(c, q_ref, k_ref, kaug_ref, kmean_ref, s_ref):
    rows, past = _tile_rows(c), slice(0, c * ATT_TILE)
    q = q_ref[rows, :]
    for head, in_head in enumerate(_lane_halves()):
        qh = jnp.where(in_head, q, jnp.zeros_like(q))
        if c == 0:
            s_past = None
        elif c > MOBA_TOPK:
            bias = _moba_block_bias(c, qh, kmean_ref)
            s_past = _dot_nt(jnp.concatenate([qh, bias], axis=1), kaug_ref[past, :])
        else:
            s_past = _dot_nt(qh, k_ref[past, :])
        _store_scores(s_ref.at[2 * (c % 2) + head], c, s_past, _dot_nt(qh, k_ref[rows, :]))


def _moba_finish(c, s_ref, p_ref, v_ref, gn_ref, o_ref):
    first_head, _ = _lane_halves()
    n_kv = (c + 1) * ATT_TILE
    outs = []
    for head in range(2):
        l = _softmax_numerator(s_ref.at[2 * (c % 2) + head], p_ref.at[head], c)
        outs.append(_dot(p_ref[head, :, :n_kv], v_ref[:n_kv, :]) / l)
    o = jnp.where(first_head, outs[0], outs[1])
    o2 = o * o
    ss_first = jnp.sum(jnp.where(first_head, o2, 0.0), axis=-1, keepdims=True)
    ss_all = jnp.sum(o2, axis=-1, keepdims=True)
    ss = jnp.where(first_head, ss_first, ss_all - ss_first)
    o_ref[_tile_rows(c), :] = ((o * lax.rsqrt(ss * (1.0 / HEAD_DIM) + EPS)) * gn_ref[...]).astype(BF16)


def _moba_kernel(q_ref, k_ref, v_ref, gn_ref, o_ref, kaug_ref, kmean_ref, s_ref, p_ref):
    _staged(k_ref.shape[0] // ATT_TILE,
            functools.partial(_moba_scores, q_ref=q_ref, k_ref=k_ref, kaug_ref=kaug_ref,
                              kmean_ref=kmean_ref, s_ref=s_ref),
            functools.partial(_moba_finish, s_ref=s_ref, p_ref=p_ref, v_ref=v_ref, gn_ref=gn_ref,
                              o_ref=o_ref),
            functools.partial(_moba_prepare, k_ref, kaug_ref, kmean_ref))


def _moba_call(proj, out_gain, width):
    batch, seq, _ = proj.shape
    n_pairs = width // LANES
    col_spec = lambda c0: pl.BlockSpec((None, seq, LANES), lambda b, p: (b, 0, c0 + p))
    return pl.pallas_call(
        _moba_kernel,
        grid=(batch, n_pairs),
        in_specs=[col_spec(0), col_spec(n_pairs), col_spec(2 * n_pairs),
                  pl.BlockSpec((1, LANES), lambda b, p: (0, 0))],
        out_specs=col_spec(0),
        out_shape=jax.ShapeDtypeStruct((batch, seq, width), BF16),
        scratch_shapes=[pltpu.VMEM((seq, 2 * LANES), BF16), pltpu.VMEM((KMEAN_ROWS, 2 * LANES), BF16),
                        pltpu.VMEM((4, ATT_TILE, seq), F32),
                        pltpu.VMEM((2, ATT_TILE, seq), BF16)],
        compiler_params=pltpu.CompilerParams(
            dimension_semantics=("parallel", "parallel"), vmem_limit_bytes=VMEM_LIMIT),
        name="moba_attention",
    )(proj, proj, proj, out_gain)


def _diff_scores(c, q_ref, k_ref, s_ref):
    rows, past = _tile_rows(c), slice(0, c * ATT_TILE)
    q = q_ref[rows, :]
    for comp, in_comp in enumerate(_lane_halves()):
        qh = jnp.where(in_comp, q, jnp.zeros_like(q))
        s_past = None if c == 0 else _dot_nt(qh, k_ref[past, :])
        _store_scores(s_ref.at[2 * (c % 2) + comp], c, s_past, _dot_nt(qh, k_ref[rows, :]))


def _diff_finish(c, out_scale, lam_ref, s_ref, p_ref, v_ref, gn_ref, o_ref):
    n_kv = (c + 1) * ATT_TILE
    l1, l2 = [_softmax_numerator(s_ref.at[2 * (c % 2) + comp], p_ref.at[comp], c) for comp in range(2)]
    coef = (lam_ref[0:1, 0:1] * l1 / l2).astype(BF16)
    p_ref[0, :, :n_kv] = p_ref[0, :, :n_kv] - p_ref[1, :, :n_kv] * coef
    o = _dot(p_ref[0, :, :n_kv], v_ref[:n_kv, :]) / l1
    ms = jnp.mean(o * o, axis=-1, keepdims=True)
    o_ref[_tile_rows(c), :] = (((o * lax.rsqrt(ms + EPS)) * gn_ref[...]) * out_scale).astype(BF16)


def _diff_kernel(out_scale, lam_ref, q_ref, k_ref, v_ref, gn_ref, o_ref, s_ref, p_ref):
    _staged(k_ref.shape[0] // ATT_TILE,
            functools.partial(_diff_scores, q_ref=q_ref, k_ref=k_ref, s_ref=s_ref),
            functools.partial(_diff_finish, out_scale=out_scale, lam_ref=lam_ref, s_ref=s_ref,
                              p_ref=p_ref, v_ref=v_ref, gn_ref=gn_ref, o_ref=o_ref))


def _diff_call(proj, lam, layer, out_gain, width, col0, out_scale):
    batch, seq, _ = proj.shape
    n_heads = width // LANES
    c0 = col0 // LANES
    col_spec = lambda c: pl.BlockSpec((None, seq, LANES), lambda b, h: (b, 0, c + h))
    return pl.pallas_call(
        functools.partial(_diff_kernel, out_scale),
        grid=(batch, n_heads),
        in_specs=[pl.BlockSpec((None, 1, LANES), lambda b, h: (layer, 0, 0)),
                  col_spec(c0), col_spec(c0 + n_heads), col_spec(c0 + 2 * n_heads),
                  pl.BlockSpec((1, LANES), lambda b, h: (0, 0))],
        out_specs=col_spec(0),
        out_shape=jax.ShapeDtypeStruct((batch, seq, width), BF16),
        scratch_shapes=[pltpu.VMEM((4, ATT_TILE, seq), F32),
                        pltpu.VMEM((2, ATT_TILE, seq), BF16)],
        compiler_params=pltpu.CompilerParams(
            dimension_semantics=("parallel", "parallel"), vmem_limit_bytes=VMEM_LIMIT),
        name="diff_attention",
    )(lam, proj, proj, proj, out_gain)


def _outproj_kernel(om_ref, od_ref, w_ref, x_ref, ga_ref, nf_ref, sc_ref, sh_ref, xo_ref, h_ref):
    width = om_ref.shape[1]
    y = _dot(om_ref[...], w_ref[:width, :]) + _dot(od_ref[...], w_ref[width:, :])
    x = x_ref[...] + ga_ref[...] * y
    xo_ref[...] = x
    h_ref[...] = _adaln(x, nf_ref[...], sc_ref[...], sh_ref[...]).astype(BF16)


def _outproj_call(o_m, o_d, w_out, x, mod5, layer, norm_gain):
    batch, seq, d_model = x.shape
    width = o_m.shape[2]
    row_spec = lambda cols: pl.BlockSpec((None, ROW_TILE, cols), lambda b, i: (b, i, 0))
    return pl.pallas_call(
        _outproj_kernel,
        grid=(batch, seq // ROW_TILE),
        in_specs=[
            row_spec(width), row_spec(width),
            pl.BlockSpec(w_out.shape, lambda b, i: (0, 0)),
            row_spec(d_model),
            _mod_spec(d_model, layer, 2),
            pl.BlockSpec((1, d_model), lambda b, i: (0, 0)),
            _mod_spec(d_model, layer, 4),
            _mod_spec(d_model, layer, 3),
        ],
        out_specs=(row_spec(d_model), row_spec(d_model)),
        out_shape=(jax.ShapeDtypeStruct((batch, seq, d_model), F32),
                   jax.ShapeDtypeStruct((batch, seq, d_model), BF16)),
        compiler_params=pltpu.CompilerParams(
            dimension_semantics=("parallel", "parallel"), vmem_limit_bytes=VMEM_LIMIT),
        name="out_projection",
    )(o_m, o_d, w_out, x, mod5, norm_gain, mod5, mod5)


def _ffn_kernel(h_ref, x_ref, gf_ref, wg_ref, wu_ref, wd_ref, o_ref, act_ref):
    h = h_ref[...]
    for c in range(wg_ref.shape[1] // MXU_DIM):
        cols = slice(c * MXU_DIM, (c + 1) * MXU_DIM)
        g = _dot(h, wg_ref[:, cols])
        u = _dot(h, wu_ref[:, cols])
        act_ref[:, cols] = ((g / (1.0 + jnp.exp(-g))) * u).astype(BF16)
    o_ref[...] = x_ref[...] + gf_ref[...] * _dot(act_ref[...], wd_ref[...])


def _ffn_call(h, x, mod5, layer, w_gate, w_up, w_down):
    batch, seq, d_model = x.shape
    d_ff = w_gate.shape[1]
    row_spec = pl.BlockSpec((None, ROW_TILE, d_model), lambda b, i: (b, i, 0))
    resident = lambda shape: pl.BlockSpec(shape, lambda b, i: (0, 0), pipeline_mode=pl.Buffered(1))
    return pl.pallas_call(
        _ffn_kernel,
        grid=(batch, seq // ROW_TILE),
        in_specs=[
            row_spec, row_spec,
            _mod_spec(d_model, layer, 5),
            resident(w_gate.shape), resident(w_up.shape), resident(w_down.shape),
        ],
        out_specs=row_spec,
        out_shape=jax.ShapeDtypeStruct((batch, seq, d_model), F32),
        scratch_shapes=[pltpu.VMEM((ROW_TILE, d_ff), BF16)],
        compiler_params=pltpu.CompilerParams(
            dimension_semantics=("parallel", "parallel"), vmem_limit_bytes=VMEM_LIMIT),
        name="swiglu_ffn",
    )(h, x, mod5, w_gate, w_up, w_down)


def kernel(x, c, positions, w_mod, b_mod, norm_mix, w_in, moba_q_norm, moba_k_norm, moba_out_norm,
           diff_q_norm, diff_k_norm, diff_lambda, diff_subln, w_out, norm_ffn, w_gate, w_up, w_down):
    batch, seq, d_model = x.shape
    depth = w_mod.shape[0]
    moba_width = d_model // 2
    diff_width = d_model // 2
    assert seq % ROW_TILE == 0 and seq % MOBA_BLOCK == 0 and ATT_TILE == MOBA_BLOCK
    assert seq // MOBA_BLOCK <= GATE_ROWS
    assert w_in.shape[2] == 3 * moba_width + 3 * diff_width and moba_width == 2 * MXU_DIM

    lam_inits = tuple(0.8 - 0.6 * math.exp(-0.3 * l) for l in range(depth))
    mod5 = _mod_call(c, w_mod, b_mod).reshape(depth, batch, N_MOD, 1, d_model)
    tab, lam = _trig_call(positions, diff_lambda, lam_inits)

    head_of_lane = np.arange(MXU_DIM) // HEAD_DIM
    group_sum = jnp.asarray(head_of_lane[:, None] == head_of_lane[None, :], BF16)
    reps = MXU_DIM // HEAD_DIM
    qk_scale = HEAD_DIM ** -0.5 * math.log2(math.e)

    w_in_b, w_out_b = w_in.astype(BF16), w_out.astype(BF16)
    w_gate_b, w_up_b, w_down_b = w_gate.astype(BF16), w_up.astype(BF16), w_down.astype(BF16)

    for l in range(depth):
        gains = jnp.stack([jnp.tile(moba_q_norm[l], reps) * qk_scale, jnp.tile(moba_k_norm[l], reps),
                           jnp.tile(diff_q_norm[l], reps) * qk_scale, jnp.tile(diff_k_norm[l], reps)])
        proj = _inproj_call(x, mod5, l, norm_mix[l][None, :], w_in_b[l], group_sum, gains, tab)
        o_m = _moba_call(proj, jnp.tile(moba_out_norm[l], LANES // HEAD_DIM)[None, :], moba_width)
        o_d = _diff_call(proj, lam, l, diff_subln[l][None, :], diff_width, 3 * moba_width,
                         1.0 - lam_inits[l])
        x, h = _outproj_call(o_m, o_d, w_out_b[l], x, mod5, l, norm_ffn[l][None, :])
        x = _ffn_call(h, x, mod5, l, w_gate_b[l], w_up_b[l], w_down_b[l])
    return x
```

```python
import functools
import math

import numpy as np
import jax
import jax.numpy as jnp
from jax import lax
from jax.experimental import pallas as pl
from jax.experimental.pallas import tpu as pltpu

F32 = jnp.float32
BF16 = jnp.bfloat16
HIGHEST = lax.Precision.HIGHEST

LANES = 128
MXU_DIM = 256
VMEM_LIMIT = 56 * 1024 * 1024

HEAD_DIM = 64
ROPE_DIM = HEAD_DIM // 4
ROPE_HALF = ROPE_DIM // 2
ROPE_THETA = 500000.0
MOBA_BLOCK = 256
MOBA_TOPK = 3
N_MOD = 6
EPS = 1e-6

ROW_TILE = 512
ATT_TILE = 256
MOD_COL_TILE = 1536


def _dot(a, b):
    return jnp.dot(a, b, preferred_element_type=F32)


def _dot_nt(a, b):
    return lax.dot_general(a, b, (((1,), (1,)), ((), ())), preferred_element_type=F32)


def _mod_kernel(c_ref, w_ref, b_ref, o_ref):
    c = c_ref[...]
    cond = c / (1.0 + jnp.exp(-c))
    o_ref[...] = jnp.dot(cond, w_ref[...], preferred_element_type=F32, precision=HIGHEST) + b_ref[...]


def _mod_call(c, w_mod, b_mod):
    depth, d_model, n_out = w_mod.shape
    batch = c.shape[0]
    return pl.pallas_call(
        _mod_kernel,
        grid=(depth, n_out // MOD_COL_TILE),
        in_specs=[
            pl.BlockSpec((batch, d_model), lambda l, j: (0, 0)),
            pl.BlockSpec((None, d_model, MOD_COL_TILE), lambda l, j: (l, 0, j)),
            pl.BlockSpec((None, 1, MOD_COL_TILE), lambda l, j: (l, 0, j)),
        ],
        out_specs=pl.BlockSpec((None, batch, MOD_COL_TILE), lambda l, j: (l, 0, j)),
        out_shape=jax.ShapeDtypeStruct((depth, batch, n_out), F32),
        compiler_params=pltpu.CompilerParams(
            dimension_semantics=("arbitrary", "arbitrary"), vmem_limit_bytes=VMEM_LIMIT),
        name="adaln_mod",
    )(c, w_mod, b_mod.reshape(depth, 1, n_out))


def _trig_kernel(lam_inits, pos_ref, inv_ref, dl_ref, cs_ref, lam_ref):
    ang = pos_ref[...].astype(F32) * inv_ref[...]
    lane = lax.broadcasted_iota(jnp.int32, ang.shape, 1)
    cs_ref[...] = jnp.where((lane & (ROPE_DIM - 1)) < ROPE_HALF, jnp.cos(ang), jnp.sin(ang))
    for l, lam_init in enumerate(lam_inits):
        lp = dl_ref[l]
        a = jnp.sum(lp[0:1] * lp[1:2], axis=-1, keepdims=True)
        b = jnp.sum(lp[2:3] * lp[3:4], axis=-1, keepdims=True)
        lam_ref[l] = jnp.broadcast_to(jnp.exp(a) - jnp.exp(b) + lam_init, (1, LANES))


def _trig_call(positions, diff_lambda, lam_inits):
    batch, seq = positions.shape
    depth = diff_lambda.shape[0]
    inv = ROPE_THETA ** (-jnp.arange(0, ROPE_DIM, 2, dtype=F32) / ROPE_DIM)
    n_rows = batch * seq * ROPE_DIM // LANES
    pos_rep = jnp.broadcast_to(positions[..., None], (batch, seq, ROPE_DIM)).reshape(n_rows, LANES)
    inv_row = jnp.tile(jnp.concatenate([inv, inv]), LANES // ROPE_DIM)[None, :]
    cs, lam = pl.pallas_call(
        functools.partial(_trig_kernel, lam_inits),
        out_shape=(jax.ShapeDtypeStruct((n_rows, LANES), F32),
                   jax.ShapeDtypeStruct((depth, 1, LANES), F32)),
        name="rope_trig_lambda",
    )(pos_rep, inv_row, diff_lambda)
    return cs.reshape(batch, seq, ROPE_DIM), lam


def _rope_expansion():
    expand = np.zeros((ROPE_DIM, 3 * LANES), np.float32)
    base = np.zeros((1, 3 * LANES), np.float32)
    base[0, :LANES] = 1.0
    for h0 in range(0, LANES, HEAD_DIM):
        base[0, h0:h0 + ROPE_DIM] = 0.0
        for j in range(ROPE_HALF):
            expand[j, h0 + j] = 1.0
            expand[j, h0 + ROPE_HALF + j] = 1.0
            expand[ROPE_HALF + j, LANES + h0 + j] = -1.0
            expand[ROPE_HALF + j, 2 * LANES + h0 + ROPE_HALF + j] = 1.0
    return jnp.asarray(expand), jnp.asarray(base)


def _adaln(x, norm_gain, scale, shift):
    ms = jnp.mean(x * x, axis=-1, keepdims=True)
    return (x * lax.rsqrt(ms + EPS)) * (norm_gain * (1.0 + scale)) + shift


def _inproj_kernel(x_ref, nm_ref, sc_ref, sh_ref, w_ref, gsum_ref, gains_ref, cs_ref, exp_ref, base_ref,
                   o_ref):
    h = _adaln(x_ref[...], nm_ref[...], sc_ref[...], sh_ref[...]).astype(BF16)
    tab = jnp.dot(cs_ref[...], exp_ref[...], preferred_element_type=F32, precision=HIGHEST) + base_ref[...]
    rope_c, rope_n, rope_p = tab[:, :LANES], tab[:, LANES:2 * LANES], tab[:, 2 * LANES:]
    group_sum = gsum_ref[...]
    n_chunks = w_ref.shape[1] // MXU_DIM
    gain_row = {0: 0, 1: 1, 3: 2, 4: 3}

    def project(c):
        return _dot(h, w_ref[:, c * MXU_DIM:(c + 1) * MXU_DIM])

    def finish(c, y):
        kind = c // 2
        if kind not in gain_row:
            o_ref[:, c * MXU_DIM:(c + 1) * MXU_DIM] = y.astype(BF16)
            return
        ssq = _dot((y * y).astype(BF16), group_sum)
        inv_rms = lax.rsqrt(ssq * (1.0 / HEAD_DIM) + EPS)
        r = gain_row[kind]
        y = y * gains_ref[r:r + 1, :]
        for half in range(MXU_DIM // LANES):
            lanes = slice(half * LANES, (half + 1) * LANES)
            yh = y[:, lanes]
            yh = (yh * rope_c + pltpu.roll(yh, LANES - ROPE_HALF, 1) * rope_n
                  + pltpu.roll(yh, ROPE_HALF, 1) * rope_p)
            lo = c * MXU_DIM + half * LANES
            o_ref[:, lo:lo + LANES] = (yh * inv_rms[:, lanes]).astype(BF16)

    upcoming = project(0)
    for c in range(n_chunks):
        y = upcoming
        if c + 1 < n_chunks:
            upcoming = project(c + 1)
        finish(c, y)


def _mod_spec(d_model, layer, k):
    return pl.BlockSpec((None, None, None, 1, d_model), lambda b, i: (layer, b, k, 0, 0))


def _layer_spec(array, layer):
    return pl.BlockSpec((None,) + array.shape[1:], lambda b, i: (layer, 0, 0))


def _const_spec(array):
    return pl.BlockSpec(array.shape, lambda b, i: (0,) * array.ndim)


def _inproj_call(x, mod5, layer, norm_gain, w_in, group_sum, gains, cs, rope_expand, rope_base):
    batch, seq, d_model = x.shape
    n_out = w_in.shape[2]
    return pl.pallas_call(
        _inproj_kernel,
        grid=(batch, seq // ROW_TILE),
        in_specs=[
            pl.BlockSpec((None, ROW_TILE, d_model), lambda b, i: (b, i, 0)),
            _layer_spec(norm_gain, layer),
            _mod_spec(d_model, layer, 1),
            _mod_spec(d_model, layer, 0),
            _layer_spec(w_in, layer),
            _const_spec(group_sum),
            _layer_spec(gains, layer),
            pl.BlockSpec((None, ROW_TILE, ROPE_DIM), lambda b, i: (b, i, 0)),
            _const_spec(rope_expand),
            _const_spec(rope_base),
        ],
        out_specs=pl.BlockSpec((None, ROW_TILE, n_out), lambda b, i: (b, i, 0)),
        out_shape=jax.ShapeDtypeStruct((batch, seq, n_out), BF16),
        compiler_params=pltpu.CompilerParams(
            dimension_semantics=("parallel", "parallel"), vmem_limit_bytes=VMEM_LIMIT),
        name="in_projection",
    )(x, norm_gain, mod5, mod5, w_in, group_sum, gains, cs, rope_expand, rope_base)


MASKED = -1e30


def _store_scores(s_ref, c, s_past, s_own):
    n_past = c * ATT_TILE
    row = lax.broadcasted_iota(jnp.int32, s_own.shape, 0)
    col = lax.broadcasted_iota(jnp.int32, s_own.shape, 1)
    s_ref[:, n_past:n_past + ATT_TILE] = jnp.where(col <= row, s_own, -jnp.inf)
    if s_past is not None:
        s_ref[:, :n_past] = s_past


def _softmax_numerator(s_ref, p_ref, c):
    n_kv = (c + 1) * ATT_TILE
    m = jnp.max(s_ref[:, :n_kv], axis=-1, keepdims=True)
    p = jnp.exp2(s_ref[:, :n_kv] - m)
    p_ref[:, :n_kv] = p.astype(BF16)
    return jnp.sum(p, axis=-1, keepdims=True)


def _staged(n_tiles, scores_fn, finish_fn, prepare_fn=None):
    def stage(c):
        if c + 1 < n_tiles:
            scores_fn(c + 1)
        if c >= 0:
            finish_fn(c)

    if prepare_fn is not None:
        prepare_fn()
    for c in range(-1, n_tiles):
        stage(c)


def _lane_halves():
    lane = lax.broadcasted_iota(jnp.int32, (1, LANES), 1)
    first = lane < HEAD_DIM
    return first, jnp.logical_not(first)


def _tile_rows(c):
    return slice(c * ATT_TILE, (c + 1) * ATT_TILE)


KMEAN_ROWS = 16
GATE_ROWS = 8


def _moba_prepare(k_ref, kaug_ref, kmean_ref):
    seq = k_ref.shape[0]
    n_blocks = seq // MOBA_BLOCK
    kaug_ref[:, :LANES] = k_ref[...]
    row_block = lax.broadcasted_iota(jnp.int32, (seq, LANES), 0) // MOBA_BLOCK
    lane = lax.broadcasted_iota(jnp.int32, (seq, LANES), 1)
    kaug_ref[:, LANES:] = jnp.where(row_block == lane, 1.0, 0.0).astype(BF16)
    means = [jnp.mean(k_ref[_tile_rows(j), :].astype(F32), axis=0, keepdims=True) for j in range(n_blocks)]
    means = jnp.concatenate(means + [jnp.zeros((KMEAN_ROWS - n_blocks, LANES), F32)], axis=0)
    hi = means.astype(BF16)
    lo = (means - hi.astype(F32)).astype(BF16)
    kmean_ref[...] = jnp.concatenate([hi, lo], axis=1)


def _moba_block_bias(c, qh, kmean_ref):
    gate = _dot_nt(kmean_ref[...], jnp.concatenate([qh, qh], axis=1))[:GATE_ROWS, :]
    blk = lax.broadcasted_iota(jnp.int32, gate.shape, 0)
    cnt = jnp.zeros(gate.shape, jnp.int32)
    for m in range(c):
        gm = gate[m:m + 1, :]
        cnt = cnt + jnp.where(blk > m, jnp.where(gm >= gate, 1, 0), jnp.where(gm > gate, 1, 0))
    bias_t = jnp.where(cnt < MOBA_TOPK, 0.0, MASKED)
    bias_t = jnp.concatenate([bias_t, jnp.zeros((LANES - GATE_ROWS, gate.shape[1]), F32)], axis=0)
    return bias_t.T.astype(BF16)


def _moba_scores(c, q_ref, k_ref, kaug_ref, kmean_ref, s_ref):
    rows, past = _tile_rows(c), slice(0, c * ATT_TILE)
    q = q_ref[rows, :]
    for head, in_head in enumerate(_lane_halves()):
        qh = jnp.where(in_head, q, jnp.zeros_like(q))
        if c == 0:
            s_past = None
        elif c > MOBA_TOPK:
            bias = _moba_block_bias(c, qh, kmean_ref)
            s_past = _dot_nt(jnp.concatenate([qh, bias], axis=1), kaug_ref[past, :])
        else:
            s_past = _dot_nt(qh, k_ref[past, :])
        _store_scores(s_ref.at[2 * (c % 2) + head], c, s_past, _dot_nt(qh, k_ref[rows, :]))


def _moba_finish(c, s_ref, p_ref, v_ref, gn_ref, o_ref):
    first_head, _ = _lane_halves()
    n_kv = (c + 1) * ATT_TILE
    outs = []
    for head in range(2):
        l = _softmax_numerator(s_ref.at[2 * (c % 2) + head], p_ref.at[head], c)
        outs.append(_dot(p_ref[head, :, :n_kv], v_ref[:n_kv, :]) / l)
    o = jnp.where(first_head, outs[0], outs[1])
    o2 = o * o
    ss_first = jnp.sum(jnp.where(first_head, o2, 0.0), axis=-1, keepdims=True)
    ss_all = jnp.sum(o2, axis=-1, keepdims=True)
    ss = jnp.where(first_head, ss_first, ss_all - ss_first)
    o_ref[_tile_rows(c), :] = ((o * lax.rsqrt(ss * (1.0 / HEAD_DIM) + EPS)) * gn_ref[...]).astype(BF16)


def _moba_kernel(q_ref, k_ref, v_ref, gn_ref, o_ref, kaug_ref, kmean_ref, s_ref, p_ref):
    _staged(k_ref.shape[0] // ATT_TILE,
            functools.partial(_moba_scores, q_ref=q_ref, k_ref=k_ref, kaug_ref=kaug_ref,
                              kmean_ref=kmean_ref, s_ref=s_ref),
            functools.partial(_moba_finish, s_ref=s_ref, p_ref=p_ref, v_ref=v_ref, gn_ref=gn_ref,
                              o_ref=o_ref),
            functools.partial(_moba_prepare, k_ref, kaug_ref, kmean_ref))


def _moba_call(proj, out_gain, layer, width):
    batch, seq, _ = proj.shape
    n_pairs = width // LANES
    col_spec = lambda c0: pl.BlockSpec((None, seq, LANES), lambda b, p: (b, 0, c0 + p))
    return pl.pallas_call(
        _moba_kernel,
        grid=(batch, n_pairs),
        in_specs=[col_spec(0), col_spec(n_pairs), col_spec(2 * n_pairs),
                  pl.BlockSpec((None, 1, LANES), lambda b, p: (layer, 0, 0))],
        out_specs=col_spec(0),
        out_shape=jax.ShapeDtypeStruct((batch, seq, width), BF16),
        scratch_shapes=[pltpu.VMEM((seq, 2 * LANES), BF16), pltpu.VMEM((KMEAN_ROWS, 2 * LANES), BF16),
                        pltpu.VMEM((4, ATT_TILE, seq), F32),
                        pltpu.VMEM((2, ATT_TILE, seq), BF16)],
        compiler_params=pltpu.CompilerParams(
            dimension_semantics=("parallel", "parallel"), vmem_limit_bytes=VMEM_LIMIT),
        name="moba_attention",
    )(proj, proj, proj, out_gain)


def _diff_scores(c, q_ref, k_ref, s_ref):
    rows, past = _tile_rows(c), slice(0, c * ATT_TILE)
    q = q_ref[rows, :]
    for comp, in_comp in enumerate(_lane_halves()):
        qh = jnp.where(in_comp, q, jnp.zeros_like(q))
        s_past = None if c == 0 else _dot_nt(qh, k_ref[past, :])
        _store_scores(s_ref.at[2 * (c % 2) + comp], c, s_past, _dot_nt(qh, k_ref[rows, :]))


def _diff_finish(c, out_scale, lam_ref, s_ref, p_ref, v_ref, gn_ref, o_ref):
    n_kv = (c + 1) * ATT_TILE
    l1, l2 = [_softmax_numerator(s_ref.at[2 * (c % 2) + comp], p_ref.at[comp], c) for comp in range(2)]
    coef = (lam_ref[0:1, 0:1] * l1 / l2).astype(BF16)
    p_ref[0, :, :n_kv] = p_ref[0, :, :n_kv] - p_ref[1, :, :n_kv] * coef
    o = _dot(p_ref[0, :, :n_kv], v_ref[:n_kv, :]) / l1
    ms = jnp.mean(o * o, axis=-1, keepdims=True)
    o_ref[_tile_rows(c), :] = (((o * lax.rsqrt(ms + EPS)) * gn_ref[...]) * out_scale).astype(BF16)


def _diff_kernel(out_scale, lam_ref, q_ref, k_ref, v_ref, gn_ref, o_ref, s_ref, p_ref):
    _staged(k_ref.shape[0] // ATT_TILE,
            functools.partial(_diff_scores, q_ref=q_ref, k_ref=k_ref, s_ref=s_ref),
            functools.partial(_diff_finish, out_scale=out_scale, lam_ref=lam_ref, s_ref=s_ref,
                              p_ref=p_ref, v_ref=v_ref, gn_ref=gn_ref, o_ref=o_ref))


def _diff_call(proj, lam, layer, out_gain, width, col0, out_scale):
    batch, seq, _ = proj.shape
    n_heads = width // LANES
    c0 = col0 // LANES
    col_spec = lambda c: pl.BlockSpec((None, seq, LANES), lambda b, h: (b, 0, c + h))
    return pl.pallas_call(
        functools.partial(_diff_kernel, out_scale),
        grid=(batch, n_heads),
        in_specs=[pl.BlockSpec((None, 1, LANES), lambda b, h: (layer, 0, 0)),
                  col_spec(c0), col_spec(c0 + n_heads), col_spec(c0 + 2 * n_heads),
                  pl.BlockSpec((None, 1, LANES), lambda b, h: (layer, 0, 0))],
        out_specs=col_spec(0),
        out_shape=jax.ShapeDtypeStruct((batch, seq, width), BF16),
        scratch_shapes=[pltpu.VMEM((4, ATT_TILE, seq), F32),
                        pltpu.VMEM((2, ATT_TILE, seq), BF16)],
        compiler_params=pltpu.CompilerParams(
            dimension_semantics=("parallel", "parallel"), vmem_limit_bytes=VMEM_LIMIT),
        name="diff_attention",
    )(lam, proj, proj, proj, out_gain)


def _mlp_kernel(om_ref, od_ref, wo_ref, x_ref, ga_ref, nf_ref, sc_ref, sh_ref, gf_ref, wg_ref, wu_ref,
                wd_ref, o_ref, act_ref):
    width = om_ref.shape[1]
    y = _dot(om_ref[...], wo_ref[:width, :]) + _dot(od_ref[...], wo_ref[width:, :])
    x = x_ref[...] + ga_ref[...] * y
    h = _adaln(x, nf_ref[...], sc_ref[...], sh_ref[...]).astype(BF16)
    for c in range(wg_ref.shape[1] // MXU_DIM):
        cols = slice(c * MXU_DIM, (c + 1) * MXU_DIM)
        g = _dot(h, wg_ref[:, cols])
        u = _dot(h, wu_ref[:, cols])
        act_ref[:, cols] = ((g / (1.0 + jnp.exp(-g))) * u).astype(BF16)
    o_ref[...] = x + gf_ref[...] * _dot(act_ref[...], wd_ref[...])


def _mlp_call(o_m, o_d, x, mod5, layer, norm_gain, w_out, w_gate, w_up, w_down):
    batch, seq, d_model = x.shape
    d_ff = w_gate.shape[2]
    row_spec = lambda cols: pl.BlockSpec((None, ROW_TILE, cols), lambda b, i: (b, i, 0))
    resident = lambda w: pl.BlockSpec((None,) + w.shape[1:], lambda b, i: (layer, 0, 0),
                                      pipeline_mode=pl.Buffered(1))
    return pl.pallas_call(
        _mlp_kernel,
        grid=(batch, seq // ROW_TILE),
        in_specs=[
            row_spec(o_m.shape[2]), row_spec(o_d.shape[2]),
            resident(w_out),
            row_spec(d_model),
            _mod_spec(d_model, layer, 2),
            _layer_spec(norm_gain, layer),
            _mod_spec(d_model, layer, 4),
            _mod_spec(d_model, layer, 3),
            _mod_spec(d_model, layer, 5),
            resident(w_gate), resident(w_up), resident(w_down),
        ],
        out_specs=row_spec(d_model),
        out_shape=jax.ShapeDtypeStruct((batch, seq, d_model), F32),
        scratch_shapes=[pltpu.VMEM((ROW_TILE, d_ff), BF16)],
        compiler_params=pltpu.CompilerParams(
            dimension_semantics=("parallel", "parallel"), vmem_limit_bytes=VMEM_LIMIT),
        name="out_projection_swiglu",
    )(o_m, o_d, w_out, x, mod5, norm_gain, mod5, mod5, mod5, w_gate, w_up, w_down)


def kernel(x, c, positions, w_mod, b_mod, norm_mix, w_in, moba_q_norm, moba_k_norm, moba_out_norm,
           diff_q_norm, diff_k_norm, diff_lambda, diff_subln, w_out, norm_ffn, w_gate, w_up, w_down):
    batch, seq, d_model = x.shape
    depth = w_mod.shape[0]
    moba_width = d_model // 2
    diff_width = d_model // 2
    assert seq % ROW_TILE == 0 and seq % MOBA_BLOCK == 0 and ATT_TILE == MOBA_BLOCK
    assert seq // MOBA_BLOCK <= GATE_ROWS
    assert w_in.shape[2] == 3 * moba_width + 3 * diff_width and moba_width == 2 * MXU_DIM

    lam_inits = tuple(0.8 - 0.6 * math.exp(-0.3 * l) for l in range(depth))
    mod5 = _mod_call(c, w_mod, b_mod).reshape(depth, batch, N_MOD, 1, d_model)
    cs, lam = _trig_call(positions, diff_lambda, lam_inits)
    rope_expand, rope_base = _rope_expansion()

    head_of_lane = np.arange(MXU_DIM) // HEAD_DIM
    group_sum = jnp.asarray(head_of_lane[:, None] == head_of_lane[None, :], BF16)
    reps = MXU_DIM // HEAD_DIM
    qk_scale = HEAD_DIM ** -0.5 * math.log2(math.e)
    gains = jnp.stack([jnp.tile(moba_q_norm, (1, reps)) * qk_scale, jnp.tile(moba_k_norm, (1, reps)),
                       jnp.tile(diff_q_norm, (1, reps)) * qk_scale, jnp.tile(diff_k_norm, (1, reps))],
                      axis=1)
    moba_out_gain = jnp.tile(moba_out_norm, (1, LANES // HEAD_DIM))[:, None, :]
    diff_out_gain = diff_subln[:, None, :]
    norm_mix3, norm_ffn3 = norm_mix[:, None, :], norm_ffn[:, None, :]

    w_in_b, w_out_b = w_in.astype(BF16), w_out.astype(BF16)
    w_gate_b, w_up_b, w_down_b = w_gate.astype(BF16), w_up.astype(BF16), w_down.astype(BF16)

    for l in range(depth):
        proj = _inproj_call(x, mod5, l, norm_mix3, w_in_b, group_sum, gains, cs, rope_expand, rope_base)
        o_m = _moba_call(proj, moba_out_gain, l, moba_width)
        o_d = _diff_call(proj, lam, l, diff_out_gain, diff_width, 3 * moba_width, 1.0 - lam_inits[l])
        x = _mlp_call(o_m, o_d, x, mod5, l, norm_ffn3, w_out_b, w_gate_b, w_up_b, w_down_b)
    return x
```

```python
import functools
import math

import numpy as np
import jax
import jax.numpy as jnp
from jax import lax
from jax.experimental import pallas as pl
from jax.experimental.pallas import tpu as pltpu

F32 = jnp.float32
BF16 = jnp.bfloat16
HIGHEST = lax.Precision.HIGHEST

LANES = 128
MXU_DIM = 256
VMEM_LIMIT = 56 * 1024 * 1024

HEAD_DIM = 64
ROPE_DIM = HEAD_DIM // 4
ROPE_HALF = ROPE_DIM // 2
ROPE_THETA = 500000.0
MOBA_BLOCK = 256
MOBA_TOPK = 3
N_MOD = 6
EPS = 1e-6

ROW_TILE = 512
ATT_TILE = 256
MOD_COL_TILE = 1536


def _dot(a, b):
    return jnp.dot(a, b, preferred_element_type=F32)


def _dot_nt(a, b):
    return lax.dot_general(a, b, (((1,), (1,)), ((), ())), preferred_element_type=F32)


def _mod_kernel(c_ref, w_ref, b_ref, o_ref):
    c = c_ref[...]
    cond = c / (1.0 + jnp.exp(-c))
    o_ref[...] = jnp.dot(cond, w_ref[...], preferred_element_type=F32, precision=HIGHEST) + b_ref[...]


def _mod_call(c, w_mod, b_mod):
    depth, d_model, n_out = w_mod.shape
    batch = c.shape[0]
    return pl.pallas_call(
        _mod_kernel,
        grid=(depth, n_out // MOD_COL_TILE),
        in_specs=[
            pl.BlockSpec((batch, d_model), lambda l, j: (0, 0)),
            pl.BlockSpec((None, d_model, MOD_COL_TILE), lambda l, j: (l, 0, j)),
            pl.BlockSpec((None, 1, MOD_COL_TILE), lambda l, j: (l, 0, j)),
        ],
        out_specs=pl.BlockSpec((None, batch, MOD_COL_TILE), lambda l, j: (l, 0, j)),
        out_shape=jax.ShapeDtypeStruct((depth, batch, n_out), F32),
        compiler_params=pltpu.CompilerParams(
            dimension_semantics=("arbitrary", "arbitrary"), vmem_limit_bytes=VMEM_LIMIT),
        name="adaln_mod",
    )(c, w_mod, b_mod.reshape(depth, 1, n_out))


def _trig_kernel(lam_inits, pos_ref, inv_ref, dl_ref, cs_ref, lam_ref):
    ang = pos_ref[...].astype(F32) * inv_ref[...]
    lane = lax.broadcasted_iota(jnp.int32, ang.shape, 1)
    cs_ref[...] = jnp.where((lane & (ROPE_DIM - 1)) < ROPE_HALF, jnp.cos(ang), jnp.sin(ang))
    for l, lam_init in enumerate(lam_inits):
        lp = dl_ref[l]
        a = jnp.sum(lp[0:1] * lp[1:2], axis=-1, keepdims=True)
        b = jnp.sum(lp[2:3] * lp[3:4], axis=-1, keepdims=True)
        lam_ref[l] = jnp.broadcast_to(jnp.exp(a) - jnp.exp(b) + lam_init, (1, LANES))


def _trig_call(positions, diff_lambda, lam_inits):
    batch, seq = positions.shape
    depth = diff_lambda.shape[0]
    inv = ROPE_THETA ** (-jnp.arange(0, ROPE_DIM, 2, dtype=F32) / ROPE_DIM)
    n_rows = batch * seq * ROPE_DIM // LANES
    pos_rep = jnp.broadcast_to(positions[..., None], (batch, seq, ROPE_DIM)).reshape(n_rows, LANES)
    inv_row = jnp.tile(jnp.concatenate([inv, inv]), LANES // ROPE_DIM)[None, :]
    cs, lam = pl.pallas_call(
        functools.partial(_trig_kernel, lam_inits),
        out_shape=(jax.ShapeDtypeStruct((n_rows, LANES), F32),
                   jax.ShapeDtypeStruct((depth, 1, LANES), F32)),
        name="rope_trig_lambda",
    )(pos_rep, inv_row, diff_lambda)
    return cs.reshape(batch, seq, ROPE_DIM), lam


def _rope_expansion():
    expand = np.zeros((ROPE_DIM, 3 * LANES), np.float32)
    base = np.zeros((1, 3 * LANES), np.float32)
    base[0, :LANES] = 1.0
    for h0 in range(0, LANES, HEAD_DIM):
        base[0, h0:h0 + ROPE_DIM] = 0.0
        for j in range(ROPE_HALF):
            expand[j, h0 + j] = 1.0
            expand[j, h0 + ROPE_HALF + j] = 1.0
            expand[ROPE_HALF + j, LANES + h0 + j] = -1.0
            expand[ROPE_HALF + j, 2 * LANES + h0 + ROPE_HALF + j] = 1.0
    return jnp.asarray(expand), jnp.asarray(base)


def _adaln(x, norm_gain, scale, shift):
    ms = jnp.mean(x * x, axis=-1, keepdims=True)
    return (x * lax.rsqrt(ms + EPS)) * (norm_gain * (1.0 + scale)) + shift


def _inproj_kernel(x_ref, nm_ref, sc_ref, sh_ref, w_ref, gsum_ref, gains_ref, cs_ref, exp_ref, base_ref,
                   o_ref):
    h = _adaln(x_ref[...], nm_ref[...], sc_ref[...], sh_ref[...]).astype(BF16)
    tab = jnp.dot(cs_ref[...], exp_ref[...], preferred_element_type=F32, precision=HIGHEST) + base_ref[...]
    rope_c, rope_n, rope_p = tab[:, :LANES], tab[:, LANES:2 * LANES], tab[:, 2 * LANES:]
    group_sum = gsum_ref[...]
    n_chunks = w_ref.shape[1] // MXU_DIM
    gain_row = {0: 0, 1: 1, 3: 2, 4: 3}

    def project(c):
        return _dot(h, w_ref[:, c * MXU_DIM:(c + 1) * MXU_DIM])

    def finish(c, y):
        kind = c // 2
        if kind not in gain_row:
            o_ref[:, c * MXU_DIM:(c + 1) * MXU_DIM] = y.astype(BF16)
            return
        ssq = _dot((y * y).astype(BF16), group_sum)
        inv_rms = lax.rsqrt(ssq * (1.0 / HEAD_DIM) + EPS)
        r = gain_row[kind]
        y = y * gains_ref[r:r + 1, :]
        for half in range(MXU_DIM // LANES):
            lanes = slice(half * LANES, (half + 1) * LANES)
            yh = y[:, lanes]
            yh = (yh * rope_c + pltpu.roll(yh, LANES - ROPE_HALF, 1) * rope_n
                  + pltpu.roll(yh, ROPE_HALF, 1) * rope_p)
            lo = c * MXU_DIM + half * LANES
            o_ref[:, lo:lo + LANES] = (yh * inv_rms[:, lanes]).astype(BF16)

    upcoming = project(0)
    for c in range(n_chunks):
        y = upcoming
        if c + 1 < n_chunks:
            upcoming = project(c + 1)
        finish(c, y)


def _mod_spec(d_model, layer, k):
    return pl.BlockSpec((None, None, None, 1, d_model), lambda b, i: (layer, b, k, 0, 0))


def _layer_spec(array, layer):
    return pl.BlockSpec((None,) + array.shape[1:], lambda b, i: (layer, 0, 0))


def _const_spec(array):
    return pl.BlockSpec(array.shape, lambda b, i: (0,) * array.ndim)


def _inproj_call(x, mod5, layer, norm_gain, w_in, group_sum, gains, cs, rope_expand, rope_base):
    batch, seq, d_model = x.shape
    n_out = w_in.shape[2]
    return pl.pallas_call(
        _inproj_kernel,
        grid=(batch, seq // ROW_TILE),
        in_specs=[
            pl.BlockSpec((None, ROW_TILE, d_model), lambda b, i: (b, i, 0)),
            _layer_spec(norm_gain, layer),
            _mod_spec(d_model, layer, 1),
            _mod_spec(d_model, layer, 0),
            _layer_spec(w_in, layer),
            _const_spec(group_sum),
            _layer_spec(gains, layer),
            pl.BlockSpec((None, ROW_TILE, ROPE_DIM), lambda b, i: (b, i, 0)),
            _const_spec(rope_expand),
            _const_spec(rope_base),
        ],
        out_specs=pl.BlockSpec((None, ROW_TILE, n_out), lambda b, i: (b, i, 0)),
        out_shape=jax.ShapeDtypeStruct((batch, seq, n_out), BF16),
        compiler_params=pltpu.CompilerParams(
            dimension_semantics=("parallel", "parallel"), vmem_limit_bytes=VMEM_LIMIT),
        name="in_projection",
    )(x, norm_gain, mod5, mod5, w_in, group_sum, gains, cs, rope_expand, rope_base)


MASKED = -1e30
ONES_ROWS = 16
PIPE_SLOTS = 2


def _slot(c, half):
    return 2 * (c % PIPE_SLOTS) + half


def _store_scores(s_ref, c, s_t):
    n_past = c * ATT_TILE
    own = s_t[n_past:, :]
    key = lax.broadcasted_iota(jnp.int32, own.shape, 0)
    query = lax.broadcasted_iota(jnp.int32, own.shape, 1)
    own = jnp.where(key <= query, own, -jnp.inf)
    s_ref[n_past:n_past + ATT_TILE, :] = own
    m = jnp.max(own, axis=0, keepdims=True)
    if c > 0:
        past = s_t[:n_past, :]
        s_ref[:n_past, :] = past
        m = jnp.maximum(m, jnp.max(past, axis=0, keepdims=True))
    return m


def _store_numerators(s_ref, p_ref, m, c):
    n_kv = (c + 1) * ATT_TILE
    p_ref[:n_kv, :] = jnp.exp2(s_ref[:n_kv, :] - m).astype(BF16)


def _weighted_values(p_ref, vt_ref, c, n_rows):
    n_kv = (c + 1) * ATT_TILE
    o_aug = _dot(vt_ref[:, :n_kv], p_ref[:n_kv, :])
    return o_aug[:n_rows] * (1.0 / o_aug[n_rows:n_rows + 1])


def _staged(n_tiles, scores_fn, probs_fn, output_fn, prepare_fn):
    prepare_fn()
    maxes = {}
    for t in range(-1, n_tiles + 1):
        if 0 <= t - 1 < n_tiles:
            output_fn(t - 1)
        if 0 <= t + 1 < n_tiles:
            maxes[t + 1] = scores_fn(t + 1)
        if 0 <= t < n_tiles:
            probs_fn(t, maxes.pop(t))


def _lane_halves():
    lane = lax.broadcasted_iota(jnp.int32, (1, LANES), 1)
    first = lane < HEAD_DIM
    return first, jnp.logical_not(first)


def _tile_rows(c):
    return slice(c * ATT_TILE, (c + 1) * ATT_TILE)


def _transposed_values(v_ref):
    return v_ref[...].astype(F32).T


KMEAN_ROWS = 16
GATE_ROWS = 8


def _moba_prepare(k_ref, v_ref, kaug_ref, kmean_ref, vt_ref):
    seq = k_ref.shape[0]
    n_blocks = seq // MOBA_BLOCK
    kaug_ref[:, :LANES] = k_ref[...]
    row_block = lax.broadcasted_iota(jnp.int32, (seq, LANES), 0) // MOBA_BLOCK
    lane = lax.broadcasted_iota(jnp.int32, (seq, LANES), 1)
    kaug_ref[:, LANES:] = jnp.where(row_block == lane, 1.0, 0.0).astype(BF16)
    means = [jnp.mean(k_ref[_tile_rows(j), :].astype(F32), axis=0, keepdims=True) for j in range(n_blocks)]
    means = jnp.concatenate(means + [jnp.zeros((KMEAN_ROWS - n_blocks, LANES), F32)], axis=0)
    hi = means.astype(BF16)
    lo = (means - hi.astype(F32)).astype(BF16)
    kmean_ref[...] = jnp.concatenate([hi, lo], axis=1)
    v_t = _transposed_values(v_ref)
    for head in range(LANES // HEAD_DIM):
        vt_ref[head, :HEAD_DIM, :] = v_t[head * HEAD_DIM:(head + 1) * HEAD_DIM].astype(BF16)
        vt_ref[head, HEAD_DIM:, :] = jnp.ones((ONES_ROWS, seq), BF16)


def _moba_block_bias(c, qh, kmean_ref):
    gate = _dot_nt(kmean_ref[...], jnp.concatenate([qh, qh], axis=1))[:GATE_ROWS, :]
    blk = lax.broadcasted_iota(jnp.int32, gate.shape, 0)
    cnt = jnp.zeros(gate.shape, jnp.int32)
    for m in range(c):
        gm = gate[m:m + 1, :]
        cnt = cnt + jnp.where(blk > m, jnp.where(gm >= gate, 1, 0), jnp.where(gm > gate, 1, 0))
    bias_t = jnp.where(blk < c, jnp.where(cnt < MOBA_TOPK, 0.0, MASKED), 0.0)
    bias_t = jnp.concatenate([bias_t, jnp.zeros((LANES - GATE_ROWS, gate.shape[1]), F32)], axis=0)
    return bias_t.T.astype(BF16)


def _moba_scores(c, q_ref, k_ref, kaug_ref, kmean_ref, s_ref):
    n_kv = (c + 1) * ATT_TILE
    q = q_ref[_tile_rows(c), :]
    maxes = []
    for head, in_head in enumerate(_lane_halves()):
        qh = jnp.where(in_head, q, jnp.zeros_like(q))
        if c > MOBA_TOPK:
            q_aug = jnp.concatenate([qh, _moba_block_bias(c, qh, kmean_ref)], axis=1)
            s_t = _dot_nt(kaug_ref[:n_kv, :], q_aug)
        else:
            s_t = _dot_nt(k_ref[:n_kv, :], qh)
        maxes.append(_store_scores(s_ref.at[_slot(c, head)], c, s_t))
    return maxes


def _store_both_numerators(c, maxes, s_ref, p_ref):
    for half, m in enumerate(maxes):
        _store_numerators(s_ref.at[_slot(c, half)], p_ref.at[_slot(c, half)], m, c)


def _moba_output(c, p_ref, vt_ref, gn_ref, o_ref):
    outs = []
    for head in range(LANES // HEAD_DIM):
        o_h = _weighted_values(p_ref.at[_slot(c, head)], vt_ref.at[head], c, HEAD_DIM)
        ms = jnp.mean(o_h * o_h, axis=0, keepdims=True)
        outs.append(o_h * lax.rsqrt(ms + EPS))
    o_t = jnp.concatenate(outs, axis=0)
    o_ref[_tile_rows(c), :] = (o_t.T * gn_ref[...]).astype(BF16)


def _moba_kernel(q_ref, k_ref, v_ref, gn_ref, o_ref, kaug_ref, kmean_ref, vt_ref, s_ref, p_ref):
    _staged(k_ref.shape[0] // ATT_TILE,
            functools.partial(_moba_scores, q_ref=q_ref, k_ref=k_ref, kaug_ref=kaug_ref,
                              kmean_ref=kmean_ref, s_ref=s_ref),
            functools.partial(_store_both_numerators, s_ref=s_ref, p_ref=p_ref),
            functools.partial(_moba_output, p_ref=p_ref, vt_ref=vt_ref, gn_ref=gn_ref, o_ref=o_ref),
            functools.partial(_moba_prepare, k_ref, v_ref, kaug_ref, kmean_ref, vt_ref))


def _moba_call(proj, out_gain, layer, width):
    batch, seq, _ = proj.shape
    n_pairs = width // LANES
    col_spec = lambda c0: pl.BlockSpec((None, seq, LANES), lambda b, p: (b, 0, c0 + p))
    return pl.pallas_call(
        _moba_kernel,
        grid=(batch, n_pairs),
        in_specs=[col_spec(0), col_spec(n_pairs), col_spec(2 * n_pairs),
                  pl.BlockSpec((None, 1, LANES), lambda b, p: (layer, 0, 0))],
        out_specs=col_spec(0),
        out_shape=jax.ShapeDtypeStruct((batch, seq, width), BF16),
        scratch_shapes=[pltpu.VMEM((seq, 2 * LANES), BF16), pltpu.VMEM((KMEAN_ROWS, 2 * LANES), BF16),
                        pltpu.VMEM((LANES // HEAD_DIM, HEAD_DIM + ONES_ROWS, seq), BF16),
                        pltpu.VMEM((2 * PIPE_SLOTS, seq, ATT_TILE), F32),
                        pltpu.VMEM((2 * PIPE_SLOTS, seq, ATT_TILE), BF16)],
        compiler_params=pltpu.CompilerParams(
            dimension_semantics=("parallel", "parallel"), vmem_limit_bytes=VMEM_LIMIT),
        name="moba_attention",
    )(proj, proj, proj, out_gain)


def _diff_prepare(v_ref, vt_ref):
    vt_ref[:LANES, :] = _transposed_values(v_ref).astype(BF16)
    vt_ref[LANES:, :] = jnp.ones((ONES_ROWS, v_ref.shape[0]), BF16)


def _diff_scores(c, q_ref, k_ref, s_ref):
    q = q_ref[_tile_rows(c), :]
    maxes = []
    for comp, in_comp in enumerate(_lane_halves()):
        qh = jnp.where(in_comp, q, jnp.zeros_like(q))
        s_t = _dot_nt(k_ref[:(c + 1) * ATT_TILE, :], qh)
        maxes.append(_store_scores(s_ref.at[_slot(c, comp)], c, s_t))
    return maxes


def _diff_output(c, out_scale, lam_ref, p_ref, vt_ref, gn_ref, o_ref):
    a1, a2 = [_weighted_values(p_ref.at[_slot(c, comp)], vt_ref, c, LANES) for comp in range(2)]
    o_t = a1 - lam_ref[0:1, 0:1] * a2
    ms = jnp.mean(o_t * o_t, axis=0, keepdims=True)
    o_t = o_t * lax.rsqrt(ms + EPS)
    o_ref[_tile_rows(c), :] = ((o_t.T * gn_ref[...]) * out_scale).astype(BF16)


def _diff_kernel(out_scale, lam_ref, q_ref, k_ref, v_ref, gn_ref, o_ref, vt_ref, s_ref, p_ref):
    _staged(k_ref.shape[0] // ATT_TILE,
            functools.partial(_diff_scores, q_ref=q_ref, k_ref=k_ref, s_ref=s_ref),
            functools.partial(_store_both_numerators, s_ref=s_ref, p_ref=p_ref),
            functools.partial(_diff_output, out_scale=out_scale, lam_ref=lam_ref, p_ref=p_ref,
                              vt_ref=vt_ref, gn_ref=gn_ref, o_ref=o_ref),
            functools.partial(_diff_prepare, v_ref, vt_ref))


def _diff_call(proj, lam, layer, out_gain, width, col0, out_scale):
    batch, seq, _ = proj.shape
    n_heads = width // LANES
    c0 = col0 // LANES
    col_spec = lambda c: pl.BlockSpec((None, seq, LANES), lambda b, h: (b, 0, c + h))
    return pl.pallas_call(
        functools.partial(_diff_kernel, out_scale),
        grid=(batch, n_heads),
        in_specs=[pl.BlockSpec((None, 1, LANES), lambda b, h: (layer, 0, 0)),
                  col_spec(c0), col_spec(c0 + n_heads), col_spec(c0 + 2 * n_heads),
                  pl.BlockSpec((None, 1, LANES), lambda b, h: (layer, 0, 0))],
        out_specs=col_spec(0),
        out_shape=jax.ShapeDtypeStruct((batch, seq, width), BF16),
        scratch_shapes=[pltpu.VMEM((LANES + ONES_ROWS, seq), BF16),
                        pltpu.VMEM((2 * PIPE_SLOTS, seq, ATT_TILE), F32),
                        pltpu.VMEM((2 * PIPE_SLOTS, seq, ATT_TILE), BF16)],
        compiler_params=pltpu.CompilerParams(
            dimension_semantics=("parallel", "parallel"), vmem_limit_bytes=VMEM_LIMIT),
        name="diff_attention",
    )(lam, proj, proj, proj, out_gain)


def _mlp_kernel(om_ref, od_ref, wo_ref, x_ref, ga_ref, nf_ref, sc_ref, sh_ref, gf_ref, wg_ref, wu_ref,
                wd_ref, o_ref, act_ref):
    width = om_ref.shape[1]
    y = _dot(om_ref[...], wo_ref[:width, :]) + _dot(od_ref[...], wo_ref[width:, :])
    x = x_ref[...] + ga_ref[...] * y
    h = _adaln(x, nf_ref[...], sc_ref[...], sh_ref[...]).astype(BF16)
    for c in range(wg_ref.shape[1] // MXU_DIM):
        cols = slice(c * MXU_DIM, (c + 1) * MXU_DIM)
        g = _dot(h, wg_ref[:, cols])
        u = _dot(h, wu_ref[:, cols])
        act_ref[:, cols] = ((g / (1.0 + jnp.exp(-g))) * u).astype(BF16)
    o_ref[...] = x + gf_ref[...] * _dot(act_ref[...], wd_ref[...])


def _mlp_call(o_m, o_d, x, mod5, layer, norm_gain, w_out, w_gate, w_up, w_down):
    batch, seq, d_model = x.shape
    d_ff = w_gate.shape[2]
    row_spec = lambda cols: pl.BlockSpec((None, ROW_TILE, cols), lambda b, i: (b, i, 0))
    resident = lambda w: pl.BlockSpec((None,) + w.shape[1:], lambda b, i: (layer, 0, 0),
                                      pipeline_mode=pl.Buffered(1))
    return pl.pallas_call(
        _mlp_kernel,
        grid=(batch, seq // ROW_TILE),
        in_specs=[
            row_spec(o_m.shape[2]), row_spec(o_d.shape[2]),
            resident(w_out),
            row_spec(d_model),
            _mod_spec(d_model, layer, 2),
            _layer_spec(norm_gain, layer),
            _mod_spec(d_model, layer, 4),
            _mod_spec(d_model, layer, 3),
            _mod_spec(d_model, layer, 5),
            resident(w_gate), resident(w_up), resident(w_down),
        ],
        out_specs=row_spec(d_model),
        out_shape=jax.ShapeDtypeStruct((batch, seq, d_model), F32),
        scratch_shapes=[pltpu.VMEM((ROW_TILE, d_ff), BF16)],
        compiler_params=pltpu.CompilerParams(
            dimension_semantics=("parallel", "parallel"), vmem_limit_bytes=VMEM_LIMIT),
        name="out_projection_swiglu",
    )(o_m, o_d, w_out, x, mod5, norm_gain, mod5, mod5, mod5, w_gate, w_up, w_down)


def kernel(x, c, positions, w_mod, b_mod, norm_mix, w_in, moba_q_norm, moba_k_norm, moba_out_norm,
           diff_q_norm, diff_k_norm, diff_lambda, diff_subln, w_out, norm_ffn, w_gate, w_up, w_down):
    batch, seq, d_model = x.shape
    depth = w_mod.shape[0]
    moba_width = d_model // 2
    diff_width = d_model // 2
    assert seq % ROW_TILE == 0 and seq % MOBA_BLOCK == 0 and ATT_TILE == MOBA_BLOCK
    assert seq // MOBA_BLOCK <= GATE_ROWS
    assert w_in.shape[2] == 3 * moba_width + 3 * diff_width and moba_width == 2 * MXU_DIM

    lam_inits = tuple(0.8 - 0.6 * math.exp(-0.3 * l) for l in range(depth))
    mod5 = _mod_call(c, w_mod, b_mod).reshape(depth, batch, N_MOD, 1, d_model)
    cs, lam = _trig_call(positions, diff_lambda, lam_inits)
    rope_expand, rope_base = _rope_expansion()

    head_of_lane = np.arange(MXU_DIM) // HEAD_DIM
    group_sum = jnp.asarray(head_of_lane[:, None] == head_of_lane[None, :], BF16)
    reps = MXU_DIM // HEAD_DIM
    qk_scale = HEAD_DIM ** -0.5 * math.log2(math.e)
    gains = jnp.stack([jnp.tile(moba_q_norm, (1, reps)) * qk_scale, jnp.tile(moba_k_norm, (1, reps)),
                       jnp.tile(diff_q_norm, (1, reps)) * qk_scale, jnp.tile(diff_k_norm, (1, reps))],
                      axis=1)
    moba_out_gain = jnp.tile(moba_out_norm, (1, LANES // HEAD_DIM))[:, None, :]
    diff_out_gain = diff_subln[:, None, :]
    norm_mix3, norm_ffn3 = norm_mix[:, None, :], norm_ffn[:, None, :]

    w_in_b, w_out_b = w_in.astype(BF16), w_out.astype(BF16)
    w_gate_b, w_up_b, w_down_b = w_gate.astype(BF16), w_up.astype(BF16), w_down.astype(BF16)

    for l in range(depth):
        proj = _inproj_call(x, mod5, l, norm_mix3, w_in_b, group_sum, gains, cs, rope_expand, rope_base)
        o_m = _moba_call(proj, moba_out_gain, l, moba_width)
        o_d = _diff_call(proj, lam, l, diff_out_gain, diff_width, 3 * moba_width, 1.0 - lam_inits[l])
        x = _mlp_call(o_m, o_d, x, mod5, l, norm_ffn3, w_out_b, w_gate_b, w_up_b, w_down_b)
    return x
```

```python
import functools
import math

import numpy as np
import jax
import jax.numpy as jnp
from jax import lax
from jax.experimental import pallas as pl
from jax.experimental.pallas import tpu as pltpu

F32 = jnp.float32
BF16 = jnp.bfloat16
HIGHEST = lax.Precision.HIGHEST

LANES = 128
MXU_DIM = 256
VMEM_LIMIT = 56 * 1024 * 1024

HEAD_DIM = 64
ROPE_DIM = HEAD_DIM // 4
ROPE_HALF = ROPE_DIM // 2
ROPE_THETA = 500000.0
MOBA_BLOCK = 256
MOBA_TOPK = 3
N_MOD = 6
EPS = 1e-6

ROW_TILE = 512
ATT_TILE = 256
MOD_COL_TILE = 1536


def _dot(a, b):
    return jnp.dot(a, b, preferred_element_type=F32)


def _dot_nt(a, b):
    return lax.dot_general(a, b, (((1,), (1,)), ((), ())), preferred_element_type=F32)


def _mod_kernel(c_ref, w_ref, b_ref, o_ref):
    c = c_ref[...]
    cond = c / (1.0 + jnp.exp(-c))
    o_ref[...] = jnp.dot(cond, w_ref[...], preferred_element_type=F32, precision=HIGHEST) + b_ref[...]


def _mod_call(c, w_mod, b_mod):
    depth, d_model, n_out = w_mod.shape
    batch = c.shape[0]
    return pl.pallas_call(
        _mod_kernel,
        grid=(depth, n_out // MOD_COL_TILE),
        in_specs=[
            pl.BlockSpec((batch, d_model), lambda l, j: (0, 0)),
            pl.BlockSpec((None, d_model, MOD_COL_TILE), lambda l, j: (l, 0, j)),
            pl.BlockSpec((None, 1, MOD_COL_TILE), lambda l, j: (l, 0, j)),
        ],
        out_specs=pl.BlockSpec((None, batch, MOD_COL_TILE), lambda l, j: (l, 0, j)),
        out_shape=jax.ShapeDtypeStruct((depth, batch, n_out), F32),
        compiler_params=pltpu.CompilerParams(
            dimension_semantics=("arbitrary", "arbitrary"), vmem_limit_bytes=VMEM_LIMIT),
        name="adaln_mod",
    )(c, w_mod, b_mod.reshape(depth, 1, n_out))


N_PIECES = 3


def _trig_kernel(lam_inits, pos_ref, inv_ref, dl_ref, cs_ref, lam_ref):
    ang = pos_ref[...].astype(F32) * inv_ref[...]
    lane = lax.broadcasted_iota(jnp.int32, ang.shape, 1)
    cs = jnp.where((lane & (ROPE_DIM - 1)) < ROPE_HALF, jnp.cos(ang), jnp.sin(ang))
    for i in range(N_PIECES):
        piece = cs.astype(BF16)
        cs_ref[i] = piece
        cs = cs - piece.astype(F32)
    for l, lam_init in enumerate(lam_inits):
        lp = dl_ref[l]
        a = jnp.sum(lp[0:1] * lp[1:2], axis=-1, keepdims=True)
        b = jnp.sum(lp[2:3] * lp[3:4], axis=-1, keepdims=True)
        lam_ref[l] = jnp.broadcast_to(jnp.exp(a) - jnp.exp(b) + lam_init, (1, LANES))


def _trig_call(positions, diff_lambda, lam_inits):
    batch, seq = positions.shape
    depth = diff_lambda.shape[0]
    inv = ROPE_THETA ** (-jnp.arange(0, ROPE_DIM, 2, dtype=F32) / ROPE_DIM)
    n_rows = batch * seq * ROPE_DIM // LANES
    pos_rep = jnp.broadcast_to(positions[..., None], (batch, seq, ROPE_DIM)).reshape(n_rows, LANES)
    inv_row = jnp.tile(jnp.concatenate([inv, inv]), LANES // ROPE_DIM)[None, :]
    cs, lam = pl.pallas_call(
        functools.partial(_trig_kernel, lam_inits),
        out_shape=(jax.ShapeDtypeStruct((N_PIECES, n_rows, LANES), BF16),
                   jax.ShapeDtypeStruct((depth, 1, LANES), F32)),
        name="rope_trig_lambda",
    )(pos_rep, inv_row, diff_lambda)
    cs = cs.reshape(N_PIECES, batch, seq, ROPE_DIM)
    return jnp.concatenate([cs[i] for i in range(N_PIECES)], axis=-1), lam


def _rope_expansion():
    expand = np.zeros((ROPE_DIM, 3 * LANES), np.float32)
    base = np.zeros((1, 3 * LANES), np.float32)
    base[0, :LANES] = 1.0
    for h0 in range(0, LANES, HEAD_DIM):
        base[0, h0:h0 + ROPE_DIM] = 0.0
        for j in range(ROPE_HALF):
            expand[j, h0 + j] = 1.0
            expand[j, h0 + ROPE_HALF + j] = 1.0
            expand[ROPE_HALF + j, LANES + h0 + j] = -1.0
            expand[ROPE_HALF + j, 2 * LANES + h0 + ROPE_HALF + j] = 1.0
    return jnp.asarray(np.tile(expand, (N_PIECES, 1)), BF16), jnp.asarray(base)


def _adaln(x, norm_gain, scale, shift):
    ms = jnp.mean(x * x, axis=-1, keepdims=True)
    return (x * lax.rsqrt(ms + EPS)) * (norm_gain * (1.0 + scale)) + shift


def _inproj_kernel(x_ref, nm_ref, sc_ref, sh_ref, w_ref, gsum_ref, gains_ref, cs_ref, exp_ref, base_ref,
                   o_ref):
    h = _adaln(x_ref[...], nm_ref[...], sc_ref[...], sh_ref[...]).astype(BF16)
    tab = base_ref[...] + _dot(cs_ref[...], exp_ref[...])
    rope_c, rope_n, rope_p = tab[:, :LANES], tab[:, LANES:2 * LANES], tab[:, 2 * LANES:]
    group_mean = gsum_ref[...]
    n_chunks = w_ref.shape[1] // MXU_DIM
    gain_row = {0: 0, 1: 1, 3: 2, 4: 3}

    def project(c):
        return _dot(h, w_ref[:, c * MXU_DIM:(c + 1) * MXU_DIM])

    def finish(c, y):
        kind = c // 2
        if kind not in gain_row:
            o_ref[:, c * MXU_DIM:(c + 1) * MXU_DIM] = y.astype(BF16)
            return
        inv_rms = lax.rsqrt(_dot((y * y).astype(BF16), group_mean) + EPS)
        r = gain_row[kind]
        y = y * gains_ref[r:r + 1, :]
        for half in range(MXU_DIM // LANES):
            lanes = slice(half * LANES, (half + 1) * LANES)
            yh = y[:, lanes]
            yh = (yh * rope_c + pltpu.roll(yh, LANES - ROPE_HALF, 1) * rope_n
                  + pltpu.roll(yh, ROPE_HALF, 1) * rope_p)
            lo = c * MXU_DIM + half * LANES
            o_ref[:, lo:lo + LANES] = (yh * inv_rms[:, lanes]).astype(BF16)

    upcoming = project(0)
    for c in range(n_chunks):
        y = upcoming
        if c + 1 < n_chunks:
            upcoming = project(c + 1)
        finish(c, y)


def _mod_spec(d_model, layer, k):
    return pl.BlockSpec((None, None, None, 1, d_model), lambda b, i: (layer, b, k, 0, 0))


def _layer_spec(array, layer):
    return pl.BlockSpec((None,) + array.shape[1:], lambda b, i: (layer, 0, 0))


def _const_spec(array):
    return pl.BlockSpec(array.shape, lambda b, i: (0,) * array.ndim)


def _inproj_call(x, mod5, layer, norm_gain, w_in, group_mean, gains, cs, rope_expand, rope_base):
    batch, seq, d_model = x.shape
    n_out = w_in.shape[2]
    return pl.pallas_call(
        _inproj_kernel,
        grid=(batch, seq // ROW_TILE),
        in_specs=[
            pl.BlockSpec((None, ROW_TILE, d_model), lambda b, i: (b, i, 0)),
            _layer_spec(norm_gain, layer),
            _mod_spec(d_model, layer, 1),
            _mod_spec(d_model, layer, 0),
            _layer_spec(w_in, layer),
            _const_spec(group_mean),
            _layer_spec(gains, layer),
            pl.BlockSpec((None, ROW_TILE, N_PIECES * ROPE_DIM), lambda b, i: (b, i, 0)),
            _const_spec(rope_expand),
            _const_spec(rope_base),
        ],
        out_specs=pl.BlockSpec((None, ROW_TILE, n_out), lambda b, i: (b, i, 0)),
        out_shape=jax.ShapeDtypeStruct((batch, seq, n_out), BF16),
        compiler_params=pltpu.CompilerParams(
            dimension_semantics=("parallel", "parallel"), vmem_limit_bytes=VMEM_LIMIT),
        name="in_projection",
    )(x, norm_gain, mod5, mod5, w_in, group_mean, gains, cs, rope_expand, rope_base)


MASKED = -1e30
ONES_ROWS = 16
PIPE_SLOTS = 2


def _slot(c, half):
    return 2 * (c % PIPE_SLOTS) + half


def _slot_scratch(seq):
    return ([pltpu.VMEM((seq, ATT_TILE), F32) for _ in range(2 * PIPE_SLOTS)]
            + [pltpu.VMEM((seq, ATT_TILE), BF16) for _ in range(2 * PIPE_SLOTS)])


def _store_scores(s_ref, c, s_t):
    n_past = c * ATT_TILE
    own = s_t[n_past:, :]
    key = lax.broadcasted_iota(jnp.int32, own.shape, 0)
    query = lax.broadcasted_iota(jnp.int32, own.shape, 1)
    own = jnp.where(key <= query, own, -jnp.inf)
    s_ref[n_past:n_past + ATT_TILE, :] = own
    m = jnp.max(own, axis=0, keepdims=True)
    if c > 0:
        past = s_t[:n_past, :]
        s_ref[:n_past, :] = past
        m = jnp.maximum(m, jnp.max(past, axis=0, keepdims=True))
    return m


def _store_numerators(s_ref, p_ref, m, c):
    n_kv = (c + 1) * ATT_TILE
    p_ref[:n_kv, :] = jnp.exp2(s_ref[:n_kv, :] - m).astype(BF16)


def _weighted_values(p_ref, vt_ref, c, n_rows):
    n_kv = (c + 1) * ATT_TILE
    o_aug = _dot(vt_ref[:, :n_kv], p_ref[:n_kv, :])
    return o_aug[:n_rows] * (1.0 / o_aug[n_rows:n_rows + 1])


def _staged(n_tiles, scores_fn, output_fn, prepare_fn, s_refs, p_refs):
    prepare_fn()
    maxes = {}
    for t in range(-1, n_tiles + 1):
        if 0 <= t - 1 < n_tiles:
            output_fn(t - 1)
        for half in range(2):
            if 0 <= t + 1 < n_tiles:
                maxes[t + 1, half] = scores_fn(t + 1, half)
            if 0 <= t < n_tiles:
                _store_numerators(s_refs[_slot(t, half)], p_refs[_slot(t, half)], maxes.pop((t, half)), t)


def _lane_halves():
    lane = lax.broadcasted_iota(jnp.int32, (1, LANES), 1)
    first = lane < HEAD_DIM
    return first, jnp.logical_not(first)


def _tile_rows(c):
    return slice(c * ATT_TILE, (c + 1) * ATT_TILE)


def _transposed_values(v_ref):
    return v_ref[...].astype(F32).T


KMEAN_ROWS = 16
GATE_ROWS = 8


def _moba_prepare(k_ref, v_ref, kaug_ref, kmean_ref, vt_ref):
    seq = k_ref.shape[0]
    n_blocks = seq // MOBA_BLOCK
    kaug_ref[:, :LANES] = k_ref[...]
    row_block = lax.broadcasted_iota(jnp.int32, (seq, LANES), 0) // MOBA_BLOCK
    lane = lax.broadcasted_iota(jnp.int32, (seq, LANES), 1)
    kaug_ref[:, LANES:] = jnp.where(row_block == lane, 1.0, 0.0).astype(BF16)
    means = [jnp.mean(k_ref[_tile_rows(j), :].astype(F32), axis=0, keepdims=True) for j in range(n_blocks)]
    means = jnp.concatenate(means + [jnp.zeros((KMEAN_ROWS - n_blocks, LANES), F32)], axis=0)
    hi = means.astype(BF16)
    lo = (means - hi.astype(F32)).astype(BF16)
    kmean_ref[...] = jnp.concatenate([hi, lo], axis=1)
    v_t = _transposed_values(v_ref)
    for head in range(LANES // HEAD_DIM):
        vt_ref[head, :HEAD_DIM, :] = v_t[head * HEAD_DIM:(head + 1) * HEAD_DIM].astype(BF16)
        vt_ref[head, HEAD_DIM:, :] = jnp.ones((ONES_ROWS, seq), BF16)


def _moba_block_bias(c, qh, kmean_ref):
    gate = _dot_nt(kmean_ref[...], jnp.concatenate([qh, qh], axis=1))[:GATE_ROWS, :]
    blk = lax.broadcasted_iota(jnp.int32, gate.shape, 0)
    cnt = jnp.zeros(gate.shape, jnp.int32)
    for m in range(c):
        gm = gate[m:m + 1, :]
        cnt = cnt + jnp.where(blk > m, jnp.where(gm >= gate, 1, 0), jnp.where(gm > gate, 1, 0))
    bias_t = jnp.where(blk < c, jnp.where(cnt < MOBA_TOPK, 0.0, MASKED), 0.0)
    bias_t = jnp.concatenate([bias_t, jnp.zeros((LANES - GATE_ROWS, gate.shape[1]), F32)], axis=0)
    return bias_t.T.astype(BF16)


def _moba_scores(c, head, q_ref, k_ref, kaug_ref, kmean_ref, s_refs):
    n_kv = (c + 1) * ATT_TILE
    q = q_ref[_tile_rows(c), :]
    qh = jnp.where(_lane_halves()[head], q, jnp.zeros_like(q))
    if c > MOBA_TOPK:
        q_aug = jnp.concatenate([qh, _moba_block_bias(c, qh, kmean_ref)], axis=1)
        s_t = _dot_nt(kaug_ref[:n_kv, :], q_aug)
    else:
        s_t = _dot_nt(k_ref[:n_kv, :], qh)
    return _store_scores(s_refs[_slot(c, head)], c, s_t)


def _moba_output(c, p_refs, vt_ref, gn_ref, o_ref):
    outs = []
    for head in range(LANES // HEAD_DIM):
        o_h = _weighted_values(p_refs[_slot(c, head)], vt_ref.at[head], c, HEAD_DIM)
        ms = jnp.mean(o_h * o_h, axis=0, keepdims=True)
        outs.append(o_h * lax.rsqrt(ms + EPS))
    o_t = jnp.concatenate(outs, axis=0)
    o_ref[_tile_rows(c), :] = (o_t.T * gn_ref[...]).astype(BF16)


def _moba_kernel(q_ref, k_ref, v_ref, gn_ref, o_ref, kaug_ref, kmean_ref, vt_ref, *slot_refs):
    s_refs, p_refs = slot_refs[:2 * PIPE_SLOTS], slot_refs[2 * PIPE_SLOTS:]
    _staged(k_ref.shape[0] // ATT_TILE,
            functools.partial(_moba_scores, q_ref=q_ref, k_ref=k_ref, kaug_ref=kaug_ref,
                              kmean_ref=kmean_ref, s_refs=s_refs),
            functools.partial(_moba_output, p_refs=p_refs, vt_ref=vt_ref, gn_ref=gn_ref, o_ref=o_ref),
            functools.partial(_moba_prepare, k_ref, v_ref, kaug_ref, kmean_ref, vt_ref), s_refs, p_refs)


def _moba_call(proj, out_gain, layer, width):
    batch, seq, _ = proj.shape
    n_pairs = width // LANES
    col_spec = lambda c0: pl.BlockSpec((None, seq, LANES), lambda b, p: (b, 0, c0 + p))
    return pl.pallas_call(
        _moba_kernel,
        grid=(batch, n_pairs),
        in_specs=[col_spec(0), col_spec(n_pairs), col_spec(2 * n_pairs),
                  pl.BlockSpec((None, 1, LANES), lambda b, p: (layer, 0, 0))],
        out_specs=col_spec(0),
        out_shape=jax.ShapeDtypeStruct((batch, seq, width), BF16),
        scratch_shapes=[pltpu.VMEM((seq, 2 * LANES), BF16), pltpu.VMEM((KMEAN_ROWS, 2 * LANES), BF16),
                        pltpu.VMEM((LANES // HEAD_DIM, HEAD_DIM + ONES_ROWS, seq), BF16),
                        *_slot_scratch(seq)],
        compiler_params=pltpu.CompilerParams(
            dimension_semantics=("parallel", "parallel"), vmem_limit_bytes=VMEM_LIMIT),
        name="moba_attention",
    )(proj, proj, proj, out_gain)


def _diff_prepare(v_ref, vt_ref):
    vt_ref[:LANES, :] = _transposed_values(v_ref).astype(BF16)
    vt_ref[LANES:, :] = jnp.ones((ONES_ROWS, v_ref.shape[0]), BF16)


def _diff_scores(c, comp, q_ref, k_ref, s_refs):
    q = q_ref[_tile_rows(c), :]
    qh = jnp.where(_lane_halves()[comp], q, jnp.zeros_like(q))
    return _store_scores(s_refs[_slot(c, comp)], c, _dot_nt(k_ref[:(c + 1) * ATT_TILE, :], qh))


def _diff_output(c, out_scale, lam_ref, p_refs, vt_ref, gn_ref, o_ref):
    a1, a2 = [_weighted_values(p_refs[_slot(c, comp)], vt_ref, c, LANES) for comp in range(2)]
    o_t = a1 - lam_ref[0:1, 0:1] * a2
    ms = jnp.mean(o_t * o_t, axis=0, keepdims=True)
    o_t = o_t * lax.rsqrt(ms + EPS)
    o_ref[_tile_rows(c), :] = ((o_t.T * gn_ref[...]) * out_scale).astype(BF16)


def _diff_kernel(out_scale, lam_ref, q_ref, k_ref, v_ref, gn_ref, o_ref, vt_ref, *slot_refs):
    s_refs, p_refs = slot_refs[:2 * PIPE_SLOTS], slot_refs[2 * PIPE_SLOTS:]
    _staged(k_ref.shape[0] // ATT_TILE,
            functools.partial(_diff_scores, q_ref=q_ref, k_ref=k_ref, s_refs=s_refs),
            functools.partial(_diff_output, out_scale=out_scale, lam_ref=lam_ref, p_refs=p_refs,
                              vt_ref=vt_ref, gn_ref=gn_ref, o_ref=o_ref),
            functools.partial(_diff_prepare, v_ref, vt_ref), s_refs, p_refs)


def _diff_call(proj, lam, layer, out_gain, width, col0, out_scale):
    batch, seq, _ = proj.shape
    n_heads = width // LANES
    c0 = col0 // LANES
    col_spec = lambda c: pl.BlockSpec((None, seq, LANES), lambda b, h: (b, 0, c + h))
    return pl.pallas_call(
        functools.partial(_diff_kernel, out_scale),
        grid=(batch, n_heads),
        in_specs=[pl.BlockSpec((None, 1, LANES), lambda b, h: (layer, 0, 0)),
                  col_spec(c0), col_spec(c0 + n_heads), col_spec(c0 + 2 * n_heads),
                  pl.BlockSpec((None, 1, LANES), lambda b, h: (layer, 0, 0))],
        out_specs=col_spec(0),
        out_shape=jax.ShapeDtypeStruct((batch, seq, width), BF16),
        scratch_shapes=[pltpu.VMEM((LANES + ONES_ROWS, seq), BF16),
                        *_slot_scratch(seq)],
        compiler_params=pltpu.CompilerParams(
            dimension_semantics=("parallel", "parallel"), vmem_limit_bytes=VMEM_LIMIT),
        name="diff_attention",
    )(lam, proj, proj, proj, out_gain)


def _mlp_kernel(om_ref, od_ref, wo_ref, x_ref, ga_ref, nf_ref, sc_ref, sh_ref, gf_ref, wg_ref, wu_ref,
                wd_ref, o_ref, act_ref):
    width = om_ref.shape[1]
    y = _dot(om_ref[...], wo_ref[:width, :]) + _dot(od_ref[...], wo_ref[width:, :])
    x = x_ref[...] + ga_ref[...] * y
    h = _adaln(x, nf_ref[...], sc_ref[...], sh_ref[...]).astype(BF16)
    for c in range(wg_ref.shape[1] // MXU_DIM):
        cols = slice(c * MXU_DIM, (c + 1) * MXU_DIM)
        g = _dot(h, wg_ref[:, cols])
        u = _dot(h, wu_ref[:, cols])
        act_ref[:, cols] = ((g / (1.0 + jnp.exp(-g))) * u).astype(BF16)
    o_ref[...] = x + gf_ref[...] * _dot(act_ref[...], wd_ref[...])


def _mlp_call(o_m, o_d, x, mod5, layer, norm_gain, w_out, w_gate, w_up, w_down):
    batch, seq, d_model = x.shape
    d_ff = w_gate.shape[2]
    row_spec = lambda cols: pl.BlockSpec((None, ROW_TILE, cols), lambda b, i: (b, i, 0))
    resident = lambda w: pl.BlockSpec((None,) + w.shape[1:], lambda b, i: (layer, 0, 0),
                                      pipeline_mode=pl.Buffered(1))
    return pl.pallas_call(
        _mlp_kernel,
        grid=(batch, seq // ROW_TILE),
        in_specs=[
            row_spec(o_m.shape[2]), row_spec(o_d.shape[2]),
            resident(w_out),
            row_spec(d_model),
            _mod_spec(d_model, layer, 2),
            _layer_spec(norm_gain, layer),
            _mod_spec(d_model, layer, 4),
            _mod_spec(d_model, layer, 3),
            _mod_spec(d_model, layer, 5),
            resident(w_gate), resident(w_up), resident(w_down),
        ],
        out_specs=row_spec(d_model),
        out_shape=jax.ShapeDtypeStruct((batch, seq, d_model), F32),
        scratch_shapes=[pltpu.VMEM((ROW_TILE, d_ff), BF16)],
        compiler_params=pltpu.CompilerParams(
            dimension_semantics=("parallel", "parallel"), vmem_limit_bytes=VMEM_LIMIT),
        name="out_projection_swiglu",
    )(o_m, o_d, w_out, x, mod5, norm_gain, mod5, mod5, mod5, w_gate, w_up, w_down)


def kernel(x, c, positions, w_mod, b_mod, norm_mix, w_in, moba_q_norm, moba_k_norm, moba_out_norm,
           diff_q_norm, diff_k_norm, diff_lambda, diff_subln, w_out, norm_ffn, w_gate, w_up, w_down):
    batch, seq, d_model = x.shape
    depth = w_mod.shape[0]
    moba_width = d_model // 2
    diff_width = d_model // 2
    assert seq % ROW_TILE == 0 and seq % MOBA_BLOCK == 0 and ATT_TILE == MOBA_BLOCK
    assert seq // MOBA_BLOCK <= GATE_ROWS
    assert w_in.shape[2] == 3 * moba_width + 3 * diff_width and moba_width == 2 * MXU_DIM

    lam_inits = tuple(0.8 - 0.6 * math.exp(-0.3 * l) for l in range(depth))
    mod5 = _mod_call(c, w_mod, b_mod).reshape(depth, batch, N_MOD, 1, d_model)
    cs, lam = _trig_call(positions, diff_lambda, lam_inits)
    rope_expand, rope_base = _rope_expansion()

    head_of_lane = np.arange(MXU_DIM) // HEAD_DIM
    group_mean = jnp.asarray((head_of_lane[:, None] == head_of_lane[None, :]) / HEAD_DIM, BF16)
    reps = MXU_DIM // HEAD_DIM
    qk_scale = HEAD_DIM ** -0.5 * math.log2(math.e)
    gains = jnp.stack([jnp.tile(moba_q_norm, (1, reps)) * qk_scale, jnp.tile(moba_k_norm, (1, reps)),
                       jnp.tile(diff_q_norm, (1, reps)) * qk_scale, jnp.tile(diff_k_norm, (1, reps))],
                      axis=1)
    moba_out_gain = jnp.tile(moba_out_norm, (1, LANES // HEAD_DIM))[:, None, :]
    diff_out_gain = diff_subln[:, None, :]
    norm_mix3, norm_ffn3 = norm_mix[:, None, :], norm_ffn[:, None, :]

    w_in_b, w_out_b = w_in.astype(BF16), w_out.astype(BF16)
    w_gate_b, w_up_b, w_down_b = w_gate.astype(BF16), w_up.astype(BF16), w_down.astype(BF16)

    for l in range(depth):
        proj = _inproj_call(x, mod5, l, norm_mix3, w_in_b, group_mean, gains, cs, rope_expand, rope_base)
        o_m = _moba_call(proj, moba_out_gain, l, moba_width)
        o_d = _diff_call(proj, lam, l, diff_out_gain, diff_width, 3 * moba_width, 1.0 - lam_inits[l])
        x = _mlp_call(o_m, o_d, x, mod5, l, norm_ffn3, w_out_b, w_gate_b, w_up_b, w_down_b)
    return x
```

```python
import functools
import math

import numpy as np
import jax
import jax.numpy as jnp
from jax import lax
from jax.experimental import pallas as pl
from jax.experimental.pallas import tpu as pltpu

F32 = jnp.float32
BF16 = jnp.bfloat16
HIGHEST = lax.Precision.HIGHEST

LANES = 128
MXU_DIM = 256
VMEM_LIMIT = 56 * 1024 * 1024

HEAD_DIM = 64
ROPE_DIM = HEAD_DIM // 4
ROPE_HALF = ROPE_DIM // 2
ROPE_THETA = 500000.0
MOBA_BLOCK = 256
MOBA_TOPK = 3
N_MOD = 6
EPS = 1e-6

ROW_TILE = 512
ATT_TILE = 256
MOD_COL_TILE = 1536


def _dot(a, b):
    return jnp.dot(a, b, preferred_element_type=F32)


def _dot_nt(a, b):
    return lax.dot_general(a, b, (((1,), (1,)), ((), ())), preferred_element_type=F32)


def _mod_kernel(c_ref, w_ref, b_ref, o_ref):
    c = c_ref[...]
    cond = c / (1.0 + jnp.exp(-c))
    o_ref[...] = jnp.dot(cond, w_ref[...], preferred_element_type=F32, precision=HIGHEST) + b_ref[...]


def _mod_call(c, w_mod, b_mod):
    depth, d_model, n_out = w_mod.shape
    batch = c.shape[0]
    return pl.pallas_call(
        _mod_kernel,
        grid=(depth, n_out // MOD_COL_TILE),
        in_specs=[
            pl.BlockSpec((batch, d_model), lambda l, j: (0, 0)),
            pl.BlockSpec((None, d_model, MOD_COL_TILE), lambda l, j: (l, 0, j)),
            pl.BlockSpec((None, 1, MOD_COL_TILE), lambda l, j: (l, 0, j)),
        ],
        out_specs=pl.BlockSpec((None, batch, MOD_COL_TILE), lambda l, j: (l, 0, j)),
        out_shape=jax.ShapeDtypeStruct((depth, batch, n_out), F32),
        compiler_params=pltpu.CompilerParams(
            dimension_semantics=("arbitrary", "arbitrary"), vmem_limit_bytes=VMEM_LIMIT),
        name="adaln_mod",
    )(c, w_mod, b_mod.reshape(depth, 1, n_out))


N_PIECES = 3


def _trig_kernel(lam_inits, pos_ref, inv_ref, dl_ref, cs_ref, lam_ref):
    ang = pos_ref[...].astype(F32) * inv_ref[...]
    lane = lax.broadcasted_iota(jnp.int32, ang.shape, 1)
    cs = jnp.where((lane & (ROPE_DIM - 1)) < ROPE_HALF, jnp.cos(ang), jnp.sin(ang))
    for i in range(N_PIECES):
        piece = cs.astype(BF16)
        cs_ref[i] = piece
        cs = cs - piece.astype(F32)
    for l, lam_init in enumerate(lam_inits):
        lp = dl_ref[l]
        a = jnp.sum(lp[0:1] * lp[1:2], axis=-1, keepdims=True)
        b = jnp.sum(lp[2:3] * lp[3:4], axis=-1, keepdims=True)
        lam_ref[l] = jnp.broadcast_to(jnp.exp(a) - jnp.exp(b) + lam_init, (1, LANES))


def _trig_call(positions, diff_lambda, lam_inits):
    batch, seq = positions.shape
    depth = diff_lambda.shape[0]
    inv = ROPE_THETA ** (-jnp.arange(0, ROPE_DIM, 2, dtype=F32) / ROPE_DIM)
    n_rows = batch * seq * ROPE_DIM // LANES
    pos_rep = jnp.broadcast_to(positions[..., None], (batch, seq, ROPE_DIM)).reshape(n_rows, LANES)
    inv_row = jnp.tile(jnp.concatenate([inv, inv]), LANES // ROPE_DIM)[None, :]
    cs, lam = pl.pallas_call(
        functools.partial(_trig_kernel, lam_inits),
        out_shape=(jax.ShapeDtypeStruct((N_PIECES, n_rows, LANES), BF16),
                   jax.ShapeDtypeStruct((depth, 1, LANES), F32)),
        name="rope_trig_lambda",
    )(pos_rep, inv_row, diff_lambda)
    cs = cs.reshape(N_PIECES, batch, seq, ROPE_DIM)
    return jnp.concatenate([cs[i] for i in range(N_PIECES)], axis=-1), lam


def _rope_expansion():
    expand = np.zeros((ROPE_DIM, 3 * LANES), np.float32)
    base = np.zeros((1, 3 * LANES), np.float32)
    base[0, :LANES] = 1.0
    for h0 in range(0, LANES, HEAD_DIM):
        base[0, h0:h0 + ROPE_DIM] = 0.0
        for j in range(ROPE_HALF):
            expand[j, h0 + j] = 1.0
            expand[j, h0 + ROPE_HALF + j] = 1.0
            expand[ROPE_HALF + j, LANES + h0 + j] = -1.0
            expand[ROPE_HALF + j, 2 * LANES + h0 + ROPE_HALF + j] = 1.0
    return jnp.asarray(np.tile(expand, (N_PIECES, 1)), BF16), jnp.asarray(base)


def _adaln(x, norm_gain, scale, shift):
    ms = jnp.mean(x * x, axis=-1, keepdims=True)
    return (x * lax.rsqrt(ms + EPS)) * (norm_gain * (1.0 + scale)) + shift


def _inproj_kernel(x_ref, nm_ref, sc_ref, sh_ref, w_ref, gsum_ref, gains_ref, cs_ref, exp_ref, base_ref,
                   o_ref):
    h = _adaln(x_ref[...], nm_ref[...], sc_ref[...], sh_ref[...]).astype(BF16)
    tab = base_ref[...] + _dot(cs_ref[...], exp_ref[...])
    rope_c, rope_n, rope_p = tab[:, :LANES], tab[:, LANES:2 * LANES], tab[:, 2 * LANES:]
    group_mean = gsum_ref[...]
    n_chunks = w_ref.shape[1] // MXU_DIM
    gain_row = {0: 0, 1: 1, 3: 2, 4: 3}

    def project(c):
        return _dot(h, w_ref[:, c * MXU_DIM:(c + 1) * MXU_DIM])

    def finish(c, y):
        kind = c // 2
        if kind not in gain_row:
            o_ref[:, c * MXU_DIM:(c + 1) * MXU_DIM] = y.astype(BF16)
            return
        inv_rms = lax.rsqrt(_dot((y * y).astype(BF16), group_mean) + EPS)
        r = gain_row[kind]
        y = y * gains_ref[r:r + 1, :]
        for half in range(MXU_DIM // LANES):
            lanes = slice(half * LANES, (half + 1) * LANES)
            yh = y[:, lanes]
            yh = (yh * rope_c + pltpu.roll(yh, LANES - ROPE_HALF, 1) * rope_n
                  + pltpu.roll(yh, ROPE_HALF, 1) * rope_p)
            lo = c * MXU_DIM + half * LANES
            o_ref[:, lo:lo + LANES] = (yh * inv_rms[:, lanes]).astype(BF16)

    upcoming = project(0)
    for c in range(n_chunks):
        y = upcoming
        if c + 1 < n_chunks:
            upcoming = project(c + 1)
        finish(c, y)


def _mod_spec(d_model, layer, k):
    return pl.BlockSpec((None, None, None, 1, d_model), lambda b, i: (layer, b, k, 0, 0))


def _layer_spec(array, layer):
    return pl.BlockSpec((None,) + array.shape[1:], lambda b, i: (layer, 0, 0))


def _const_spec(array):
    return pl.BlockSpec(array.shape, lambda b, i: (0,) * array.ndim)


def _inproj_call(x, mod5, layer, norm_gain, w_in, group_mean, gains, cs, rope_expand, rope_base):
    batch, seq, d_model = x.shape
    n_out = w_in.shape[2]
    return pl.pallas_call(
        _inproj_kernel,
        grid=(batch, seq // ROW_TILE),
        in_specs=[
            pl.BlockSpec((None, ROW_TILE, d_model), lambda b, i: (b, i, 0)),
            _layer_spec(norm_gain, layer),
            _mod_spec(d_model, layer, 1),
            _mod_spec(d_model, layer, 0),
            _layer_spec(w_in, layer),
            _const_spec(group_mean),
            _layer_spec(gains, layer),
            pl.BlockSpec((None, ROW_TILE, N_PIECES * ROPE_DIM), lambda b, i: (b, i, 0)),
            _const_spec(rope_expand),
            _const_spec(rope_base),
        ],
        out_specs=pl.BlockSpec((None, ROW_TILE, n_out), lambda b, i: (b, i, 0)),
        out_shape=jax.ShapeDtypeStruct((batch, seq, n_out), BF16),
        compiler_params=pltpu.CompilerParams(
            dimension_semantics=("parallel", "parallel"), vmem_limit_bytes=VMEM_LIMIT),
        name="in_projection",
    )(x, norm_gain, mod5, mod5, w_in, group_mean, gains, cs, rope_expand, rope_base)


MASKED = -1e30
ONES_ROWS = 16
PIPE_SLOTS = 2


def _slot(c, half):
    return 2 * (c % PIPE_SLOTS) + half


def _slot_scratch(seq):
    return ([pltpu.VMEM((seq, ATT_TILE), F32) for _ in range(2 * PIPE_SLOTS)]
            + [pltpu.VMEM((seq, ATT_TILE), BF16) for _ in range(2 * PIPE_SLOTS)])


def _store_scores(s_ref, c, s_t):
    n_past = c * ATT_TILE
    own = s_t[n_past:, :]
    key = lax.broadcasted_iota(jnp.int32, own.shape, 0)
    query = lax.broadcasted_iota(jnp.int32, own.shape, 1)
    own = jnp.where(key <= query, own, -jnp.inf)
    s_ref[n_past:n_past + ATT_TILE, :] = own
    m = jnp.max(own, axis=0, keepdims=True)
    if c > 0:
        past = s_t[:n_past, :]
        s_ref[:n_past, :] = past
        m = jnp.maximum(m, jnp.max(past, axis=0, keepdims=True))
    return m


def _store_numerators(s_ref, p_ref, m, c):
    n_kv = (c + 1) * ATT_TILE
    p_ref[:n_kv, :] = jnp.exp2(s_ref[:n_kv, :] - m).astype(BF16)


def _weighted_values(p_ref, vt_ref, c, n_rows):
    n_kv = (c + 1) * ATT_TILE
    o_aug = _dot(vt_ref[:, :n_kv], p_ref[:n_kv, :])
    return o_aug[:n_rows] * (1.0 / o_aug[n_rows:n_rows + 1])


def _staged(n_tiles, scores_fn, output_fn, prepare_fn, s_refs, p_refs, ahead_fn=None):
    prepare_fn()
    maxes, ahead = {}, {}
    for t in range(-2, n_tiles + 1):
        if ahead_fn is not None and 0 <= t + 2 < n_tiles:
            for half in range(2):
                ahead[t + 2, half] = ahead_fn(t + 2, half)
        for half in range(2):
            if 0 <= t + 1 < n_tiles:
                maxes[t + 1, half] = scores_fn(t + 1, half, ahead.pop((t + 1, half), None))
            if 0 <= t < n_tiles:
                _store_numerators(s_refs[_slot(t, half)], p_refs[_slot(t, half)], maxes.pop((t, half)), t)
        if 0 <= t - 1 < n_tiles:
            output_fn(t - 1)


def _lane_halves():
    lane = lax.broadcasted_iota(jnp.int32, (1, LANES), 1)
    first = lane < HEAD_DIM
    return first, jnp.logical_not(first)


def _tile_rows(c):
    return slice(c * ATT_TILE, (c + 1) * ATT_TILE)


def _transposed_values(v_ref):
    return v_ref[...].astype(F32).T


KMEAN_ROWS = 16
GATE_ROWS = 8


def _moba_prepare(k_ref, v_ref, kaug_ref, kmean_ref, vt_ref):
    seq = k_ref.shape[0]
    n_blocks = seq // MOBA_BLOCK
    kaug_ref[:, :LANES] = k_ref[...]
    row_block = lax.broadcasted_iota(jnp.int32, (seq, LANES), 0) // MOBA_BLOCK
    lane = lax.broadcasted_iota(jnp.int32, (seq, LANES), 1)
    kaug_ref[:, LANES:] = jnp.where(row_block == lane, 1.0, 0.0).astype(BF16)
    means = [jnp.mean(k_ref[_tile_rows(j), :].astype(F32), axis=0, keepdims=True) for j in range(n_blocks)]
    means = jnp.concatenate(means + [jnp.zeros((KMEAN_ROWS - n_blocks, LANES), F32)], axis=0)
    hi = means.astype(BF16)
    lo = (means - hi.astype(F32)).astype(BF16)
    kmean_ref[...] = jnp.concatenate([hi, lo], axis=1)
    v_t = _transposed_values(v_ref)
    for head in range(LANES // HEAD_DIM):
        vt_ref[head, :HEAD_DIM, :] = v_t[head * HEAD_DIM:(head + 1) * HEAD_DIM].astype(BF16)
        vt_ref[head, HEAD_DIM:, :] = jnp.ones((ONES_ROWS, seq), BF16)


def _moba_block_bias(c, head, q_ref, kmean_ref):
    if c <= MOBA_TOPK:
        return None
    qh = _head_queries(c, head, q_ref)
    gate = _dot_nt(kmean_ref[...], jnp.concatenate([qh, qh], axis=1))[:GATE_ROWS, :]
    blk = lax.broadcasted_iota(jnp.int32, gate.shape, 0)
    cnt = jnp.zeros(gate.shape, jnp.int32)
    for m in range(c):
        gm = gate[m:m + 1, :]
        cnt = cnt + jnp.where(blk > m, jnp.where(gm >= gate, 1, 0), jnp.where(gm > gate, 1, 0))
    bias_t = jnp.where(blk < c, jnp.where(cnt < MOBA_TOPK, 0.0, MASKED), 0.0)
    bias_t = jnp.concatenate([bias_t, jnp.zeros((LANES - GATE_ROWS, gate.shape[1]), F32)], axis=0)
    return bias_t.T.astype(BF16)


def _head_queries(c, head, q_ref):
    q = q_ref[_tile_rows(c), :]
    return jnp.where(_lane_halves()[head], q, jnp.zeros_like(q))


def _moba_scores(c, head, bias, q_ref, k_ref, kaug_ref, s_refs):
    n_kv = (c + 1) * ATT_TILE
    qh = _head_queries(c, head, q_ref)
    if bias is None:
        s_t = _dot_nt(k_ref[:n_kv, :], qh)
    else:
        s_t = _dot_nt(kaug_ref[:n_kv, :], jnp.concatenate([qh, bias], axis=1))
    return _store_scores(s_refs[_slot(c, head)], c, s_t)


def _moba_output(c, p_refs, vt_ref, gn_ref, o_ref):
    outs = []
    for head in range(LANES // HEAD_DIM):
        o_h = _weighted_values(p_refs[_slot(c, head)], vt_ref.at[head], c, HEAD_DIM)
        ms = jnp.mean(o_h * o_h, axis=0, keepdims=True)
        outs.append(o_h * lax.rsqrt(ms + EPS))
    o_t = jnp.concatenate(outs, axis=0)
    o_ref[_tile_rows(c), :] = (o_t.T * gn_ref[...]).astype(BF16)


def _moba_kernel(q_ref, k_ref, v_ref, gn_ref, o_ref, kaug_ref, kmean_ref, vt_ref, *slot_refs):
    s_refs, p_refs = slot_refs[:2 * PIPE_SLOTS], slot_refs[2 * PIPE_SLOTS:]
    _staged(k_ref.shape[0] // ATT_TILE,
            functools.partial(_moba_scores, q_ref=q_ref, k_ref=k_ref, kaug_ref=kaug_ref, s_refs=s_refs),
            functools.partial(_moba_output, p_refs=p_refs, vt_ref=vt_ref, gn_ref=gn_ref, o_ref=o_ref),
            functools.partial(_moba_prepare, k_ref, v_ref, kaug_ref, kmean_ref, vt_ref), s_refs, p_refs,
            ahead_fn=functools.partial(_moba_block_bias, q_ref=q_ref, kmean_ref=kmean_ref))


def _moba_call(proj, out_gain, layer, width):
    batch, seq, _ = proj.shape
    n_pairs = width // LANES
    col_spec = lambda c0: pl.BlockSpec((None, seq, LANES), lambda b, p: (b, 0, c0 + p))
    return pl.pallas_call(
        _moba_kernel,
        grid=(batch, n_pairs),
        in_specs=[col_spec(0), col_spec(n_pairs), col_spec(2 * n_pairs),
                  pl.BlockSpec((None, 1, LANES), lambda b, p: (layer, 0, 0))],
        out_specs=col_spec(0),
        out_shape=jax.ShapeDtypeStruct((batch, seq, width), BF16),
        scratch_shapes=[pltpu.VMEM((seq, 2 * LANES), BF16), pltpu.VMEM((KMEAN_ROWS, 2 * LANES), BF16),
                        pltpu.VMEM((LANES // HEAD_DIM, HEAD_DIM + ONES_ROWS, seq), BF16),
                        *_slot_scratch(seq)],
        compiler_params=pltpu.CompilerParams(
            dimension_semantics=("parallel", "parallel"), vmem_limit_bytes=VMEM_LIMIT),
        name="moba_attention",
    )(proj, proj, proj, out_gain)


def _diff_prepare(v_ref, vt_ref):
    vt_ref[:LANES, :] = _transposed_values(v_ref).astype(BF16)
    vt_ref[LANES:, :] = jnp.ones((ONES_ROWS, v_ref.shape[0]), BF16)


def _diff_scores(c, comp, _, q_ref, k_ref, s_refs):
    q = q_ref[_tile_rows(c), :]
    qh = jnp.where(_lane_halves()[comp], q, jnp.zeros_like(q))
    return _store_scores(s_refs[_slot(c, comp)], c, _dot_nt(k_ref[:(c + 1) * ATT_TILE, :], qh))


def _diff_output(c, out_scale, lam_ref, p_refs, vt_ref, gn_ref, o_ref):
    a1, a2 = [_weighted_values(p_refs[_slot(c, comp)], vt_ref, c, LANES) for comp in range(2)]
    o_t = a1 - lam_ref[0:1, 0:1] * a2
    ms = jnp.mean(o_t * o_t, axis=0, keepdims=True)
    o_t = o_t * lax.rsqrt(ms + EPS)
    o_ref[_tile_rows(c), :] = ((o_t.T * gn_ref[...]) * out_scale).astype(BF16)


def _diff_kernel(out_scale, lam_ref, q_ref, k_ref, v_ref, gn_ref, o_ref, vt_ref, *slot_refs):
    s_refs, p_refs = slot_refs[:2 * PIPE_SLOTS], slot_refs[2 * PIPE_SLOTS:]
    _staged(k_ref.shape[0] // ATT_TILE,
            functools.partial(_diff_scores, q_ref=q_ref, k_ref=k_ref, s_refs=s_refs),
            functools.partial(_diff_output, out_scale=out_scale, lam_ref=lam_ref, p_refs=p_refs,
                              vt_ref=vt_ref, gn_ref=gn_ref, o_ref=o_ref),
            functools.partial(_diff_prepare, v_ref, vt_ref), s_refs, p_refs)


def _diff_call(proj, lam, layer, out_gain, width, col0, out_scale):
    batch, seq, _ = proj.shape
    n_heads = width // LANES
    c0 = col0 // LANES
    col_spec = lambda c: pl.BlockSpec((None, seq, LANES), lambda b, h: (b, 0, c + h))
    return pl.pallas_call(
        functools.partial(_diff_kernel, out_scale),
        grid=(batch, n_heads),
        in_specs=[pl.BlockSpec((None, 1, LANES), lambda b, h: (layer, 0, 0)),
                  col_spec(c0), col_spec(c0 + n_heads), col_spec(c0 + 2 * n_heads),
                  pl.BlockSpec((None, 1, LANES), lambda b, h: (layer, 0, 0))],
        out_specs=col_spec(0),
        out_shape=jax.ShapeDtypeStruct((batch, seq, width), BF16),
        scratch_shapes=[pltpu.VMEM((LANES + ONES_ROWS, seq), BF16),
                        *_slot_scratch(seq)],
        compiler_params=pltpu.CompilerParams(
            dimension_semantics=("parallel", "parallel"), vmem_limit_bytes=VMEM_LIMIT),
        name="diff_attention",
    )(lam, proj, proj, proj, out_gain)


def _mlp_kernel(om_ref, od_ref, wo_ref, x_ref, ga_ref, nf_ref, sc_ref, sh_ref, gf_ref, wg_ref, wu_ref,
                wd_ref, o_ref, act_ref):
    width = om_ref.shape[1]
    y = _dot(om_ref[...], wo_ref[:width, :]) + _dot(od_ref[...], wo_ref[width:, :])
    x = x_ref[...] + ga_ref[...] * y
    h = _adaln(x, nf_ref[...], sc_ref[...], sh_ref[...]).astype(BF16)
    for c in range(wg_ref.shape[1] // MXU_DIM):
        cols = slice(c * MXU_DIM, (c + 1) * MXU_DIM)
        g = _dot(h, wg_ref[:, cols])
        u = _dot(h, wu_ref[:, cols])
        act_ref[:, cols] = ((g / (1.0 + jnp.exp(-g))) * u).astype(BF16)
    o_ref[...] = x + gf_ref[...] * _dot(act_ref[...], wd_ref[...])


def _mlp_call(o_m, o_d, x, mod5, layer, norm_gain, w_out, w_gate, w_up, w_down):
    batch, seq, d_model = x.shape
    d_ff = w_gate.shape[2]
    row_spec = lambda cols: pl.BlockSpec((None, ROW_TILE, cols), lambda b, i: (b, i, 0))
    resident = lambda w: pl.BlockSpec((None,) + w.shape[1:], lambda b, i: (layer, 0, 0),
                                      pipeline_mode=pl.Buffered(1))
    return pl.pallas_call(
        _mlp_kernel,
        grid=(batch, seq // ROW_TILE),
        in_specs=[
            row_spec(o_m.shape[2]), row_spec(o_d.shape[2]),
            resident(w_out),
            row_spec(d_model),
            _mod_spec(d_model, layer, 2),
            _layer_spec(norm_gain, layer),
            _mod_spec(d_model, layer, 4),
            _mod_spec(d_model, layer, 3),
            _mod_spec(d_model, layer, 5),
            resident(w_gate), resident(w_up), resident(w_down),
        ],
        out_specs=row_spec(d_model),
        out_shape=jax.ShapeDtypeStruct((batch, seq, d_model), F32),
        scratch_shapes=[pltpu.VMEM((ROW_TILE, d_ff), BF16)],
        compiler_params=pltpu.CompilerParams(
            dimension_semantics=("parallel", "parallel"), vmem_limit_bytes=VMEM_LIMIT),
        name="out_projection_swiglu",
    )(o_m, o_d, w_out, x, mod5, norm_gain, mod5, mod5, mod5, w_gate, w_up, w_down)


def kernel(x, c, positions, w_mod, b_mod, norm_mix, w_in, moba_q_norm, moba_k_norm, moba_out_norm,
           diff_q_norm, diff_k_norm, diff_lambda, diff_subln, w_out, norm_ffn, w_gate, w_up, w_down):
    batch, seq, d_model = x.shape
    depth = w_mod.shape[0]
    moba_width = d_model // 2
    diff_width = d_model // 2
    assert seq % ROW_TILE == 0 and seq % MOBA_BLOCK == 0 and ATT_TILE == MOBA_BLOCK
    assert seq // MOBA_BLOCK <= GATE_ROWS
    assert w_in.shape[2] == 3 * moba_width + 3 * diff_width and moba_width == 2 * MXU_DIM

    lam_inits = tuple(0.8 - 0.6 * math.exp(-0.3 * l) for l in range(depth))
    mod5 = _mod_call(c, w_mod, b_mod).reshape(depth, batch, N_MOD, 1, d_model)
    cs, lam = _trig_call(positions, diff_lambda, lam_inits)
    rope_expand, rope_base = _rope_expansion()

    head_of_lane = np.arange(MXU_DIM) // HEAD_DIM
    group_mean = jnp.asarray((head_of_lane[:, None] == head_of_lane[None, :]) / HEAD_DIM, BF16)
    reps = MXU_DIM // HEAD_DIM
    qk_scale = HEAD_DIM ** -0.5 * math.log2(math.e)
    gains = jnp.stack([jnp.tile(moba_q_norm, (1, reps)) * qk_scale, jnp.tile(moba_k_norm, (1, reps)),
                       jnp.tile(diff_q_norm, (1, reps)) * qk_scale, jnp.tile(diff_k_norm, (1, reps))],
                      axis=1)
    moba_out_gain = jnp.tile(moba_out_norm, (1, LANES // HEAD_DIM))[:, None, :]
    diff_out_gain = diff_subln[:, None, :]
    norm_mix3, norm_ffn3 = norm_mix[:, None, :], norm_ffn[:, None, :]

    w_in_b, w_out_b = w_in.astype(BF16), w_out.astype(BF16)
    w_gate_b, w_up_b, w_down_b = w_gate.astype(BF16), w_up.astype(BF16), w_down.astype(BF16)

    for l in range(depth):
        proj = _inproj_call(x, mod5, l, norm_mix3, w_in_b, group_mean, gains, cs, rope_expand, rope_base)
        o_m = _moba_call(proj, moba_out_gain, l, moba_width)
        o_d = _diff_call(proj, lam, l, diff_out_gain, diff_width, 3 * moba_width, 1.0 - lam_inits[l])
        x = _mlp_call(o_m, o_d, x, mod5, l, norm_ffn3, w_out_b, w_gate_b, w_up_b, w_down_b)
    return x
```

```python
import functools
import math

import numpy as np
import jax
import jax.numpy as jnp
from jax import lax
from jax.experimental import pallas as pl
from jax.experimental.pallas import tpu as pltpu

F32 = jnp.float32
BF16 = jnp.bfloat16
HIGHEST = lax.Precision.HIGHEST

LANES = 128
MXU_DIM = 256
VMEM_LIMIT = 56 * 1024 * 1024

HEAD_DIM = 64
ROPE_DIM = HEAD_DIM // 4
ROPE_HALF = ROPE_DIM // 2
ROPE_THETA = 500000.0
MOBA_BLOCK = 256
MOBA_TOPK = 3
N_MOD = 6
EPS = 1e-6

ROW_TILE = 512
ATT_TILE = 256
MOD_COL_TILE = 1536


def _dot(a, b):
    return jnp.dot(a, b, preferred_element_type=F32)


def _dot_nt(a, b):
    return lax.dot_general(a, b, (((1,), (1,)), ((), ())), preferred_element_type=F32)


def _mod_kernel(c_ref, w_ref, b_ref, o_ref):
    c = c_ref[...]
    cond = c / (1.0 + jnp.exp(-c))
    o_ref[...] = jnp.dot(cond, w_ref[...], preferred_element_type=F32, precision=HIGHEST) + b_ref[...]


def _mod_call(c, w_mod, b_mod):
    depth, d_model, n_out = w_mod.shape
    batch = c.shape[0]
    return pl.pallas_call(
        _mod_kernel,
        grid=(depth, n_out // MOD_COL_TILE),
        in_specs=[
            pl.BlockSpec((batch, d_model), lambda l, j: (0, 0)),
            pl.BlockSpec((None, d_model, MOD_COL_TILE), lambda l, j: (l, 0, j)),
            pl.BlockSpec((None, 1, MOD_COL_TILE), lambda l, j: (l, 0, j)),
        ],
        out_specs=pl.BlockSpec((None, batch, MOD_COL_TILE), lambda l, j: (l, 0, j)),
        out_shape=jax.ShapeDtypeStruct((depth, batch, n_out), F32),
        compiler_params=pltpu.CompilerParams(
            dimension_semantics=("arbitrary", "arbitrary"), vmem_limit_bytes=VMEM_LIMIT),
        name="adaln_mod",
    )(c, w_mod, b_mod.reshape(depth, 1, n_out))


N_PIECES = 3


def _trig_kernel(lam_inits, pos_ref, inv_ref, dl_ref, cs_ref, lam_ref):
    ang = pos_ref[...].astype(F32) * inv_ref[...]
    lane = lax.broadcasted_iota(jnp.int32, ang.shape, 1)
    cs = jnp.where((lane & (ROPE_DIM - 1)) < ROPE_HALF, jnp.cos(ang), jnp.sin(ang))
    for i in range(N_PIECES):
        piece = cs.astype(BF16)
        cs_ref[i] = piece
        cs = cs - piece.astype(F32)
    for l, lam_init in enumerate(lam_inits):
        lp = dl_ref[l]
        a = jnp.sum(lp[0:1] * lp[1:2], axis=-1, keepdims=True)
        b = jnp.sum(lp[2:3] * lp[3:4], axis=-1, keepdims=True)
        lam_ref[l] = jnp.broadcast_to(jnp.exp(a) - jnp.exp(b) + lam_init, (1, LANES))


def _trig_call(positions, diff_lambda, lam_inits):
    batch, seq = positions.shape
    depth = diff_lambda.shape[0]
    inv = ROPE_THETA ** (-jnp.arange(0, ROPE_DIM, 2, dtype=F32) / ROPE_DIM)
    n_rows = batch * seq * ROPE_DIM // LANES
    pos_rep = jnp.broadcast_to(positions[..., None], (batch, seq, ROPE_DIM)).reshape(n_rows, LANES)
    inv_row = jnp.tile(jnp.concatenate([inv, inv]), LANES // ROPE_DIM)[None, :]
    cs, lam = pl.pallas_call(
        functools.partial(_trig_kernel, lam_inits),
        out_shape=(jax.ShapeDtypeStruct((N_PIECES, n_rows, LANES), BF16),
                   jax.ShapeDtypeStruct((depth, 1, LANES), F32)),
        name="rope_trig_lambda",
    )(pos_rep, inv_row, diff_lambda)
    cs = cs.reshape(N_PIECES, batch, seq, ROPE_DIM)
    return jnp.concatenate([cs[i] for i in range(N_PIECES)], axis=-1), lam


def _rope_expansion():
    expand = np.zeros((ROPE_DIM, 3 * LANES), np.float32)
    base = np.zeros((1, 3 * LANES), np.float32)
    base[0, :LANES] = 1.0
    for h0 in range(0, LANES, HEAD_DIM):
        base[0, h0:h0 + ROPE_DIM] = 0.0
        for j in range(ROPE_HALF):
            expand[j, h0 + j] = 1.0
            expand[j, h0 + ROPE_HALF + j] = 1.0
            expand[ROPE_HALF + j, LANES + h0 + j] = -1.0
            expand[ROPE_HALF + j, 2 * LANES + h0 + ROPE_HALF + j] = 1.0
    return jnp.asarray(np.tile(expand, (N_PIECES, 1)), BF16), jnp.asarray(base)


def _adaln(x, norm_gain, scale, shift):
    ms = jnp.mean(x * x, axis=-1, keepdims=True)
    return (x * lax.rsqrt(ms + EPS)) * (norm_gain * (1.0 + scale)) + shift


def _inproj_kernel(x_ref, nm_ref, sc_ref, sh_ref, w_ref, gsum_ref, gains_ref, cs_ref, exp_ref, base_ref,
                   o_ref):
    h = _adaln(x_ref[...], nm_ref[...], sc_ref[...], sh_ref[...]).astype(BF16)
    tab = base_ref[...] + _dot(cs_ref[...], exp_ref[...])
    rope_c, rope_n, rope_p = tab[:, :LANES], tab[:, LANES:2 * LANES], tab[:, 2 * LANES:]
    group_mean = gsum_ref[...]
    n_chunks = w_ref.shape[1] // MXU_DIM
    gain_row = {0: 0, 1: 1, 3: 2, 4: 3}

    def project(c):
        return _dot(h, w_ref[:, c * MXU_DIM:(c + 1) * MXU_DIM])

    def finish(c, y):
        kind = c // 2
        if kind not in gain_row:
            o_ref[:, c * MXU_DIM:(c + 1) * MXU_DIM] = y.astype(BF16)
            return
        inv_rms = lax.rsqrt(_dot((y * y).astype(BF16), group_mean) + EPS)
        r = gain_row[kind]
        y = y * gains_ref[r:r + 1, :]
        for half in range(MXU_DIM // LANES):
            lanes = slice(half * LANES, (half + 1) * LANES)
            yh = y[:, lanes]
            yh = (yh * rope_c + pltpu.roll(yh, LANES - ROPE_HALF, 1) * rope_n
                  + pltpu.roll(yh, ROPE_HALF, 1) * rope_p)
            lo = c * MXU_DIM + half * LANES
            o_ref[:, lo:lo + LANES] = (yh * inv_rms[:, lanes]).astype(BF16)

    upcoming = project(0)
    for c in range(n_chunks):
        y = upcoming
        if c + 1 < n_chunks:
            upcoming = project(c + 1)
        finish(c, y)


def _mod_spec(d_model, layer, k):
    return pl.BlockSpec((None, None, None, 1, d_model), lambda b, i: (layer, b, k, 0, 0))


def _layer_spec(array, layer):
    return pl.BlockSpec((None,) + array.shape[1:], lambda b, i: (layer, 0, 0))


def _const_spec(array):
    return pl.BlockSpec(array.shape, lambda b, i: (0,) * array.ndim)


def _inproj_call(x, mod5, layer, norm_gain, w_in, group_mean, gains, cs, rope_expand, rope_base):
    batch, seq, d_model = x.shape
    n_out = w_in.shape[2]
    return pl.pallas_call(
        _inproj_kernel,
        grid=(batch, seq // ROW_TILE),
        in_specs=[
            pl.BlockSpec((None, ROW_TILE, d_model), lambda b, i: (b, i, 0)),
            _layer_spec(norm_gain, layer),
            _mod_spec(d_model, layer, 1),
            _mod_spec(d_model, layer, 0),
            _layer_spec(w_in, layer),
            _const_spec(group_mean),
            _layer_spec(gains, layer),
            pl.BlockSpec((None, ROW_TILE, N_PIECES * ROPE_DIM), lambda b, i: (b, i, 0)),
            _const_spec(rope_expand),
            _const_spec(rope_base),
        ],
        out_specs=pl.BlockSpec((None, ROW_TILE, n_out), lambda b, i: (b, i, 0)),
        out_shape=jax.ShapeDtypeStruct((batch, seq, n_out), BF16),
        compiler_params=pltpu.CompilerParams(
            dimension_semantics=("parallel", "parallel"), vmem_limit_bytes=VMEM_LIMIT),
        name="in_projection",
    )(x, norm_gain, mod5, mod5, w_in, group_mean, gains, cs, rope_expand, rope_base)


MASKED = -1e30
ONES_ROWS = 16
PIPE_SLOTS = 2


def _slot(i, half):
    return 2 * (i % PIPE_SLOTS) + half


def _slot_scratch(seq):
    return ([pltpu.VMEM((seq, ATT_TILE), F32) for _ in range(2 * PIPE_SLOTS)]
            + [pltpu.VMEM((seq, ATT_TILE), BF16) for _ in range(2 * PIPE_SLOTS)])


def _store_scores(s_ref, c, s_t):
    n_past = c * ATT_TILE
    own = s_t[n_past:, :]
    key = lax.broadcasted_iota(jnp.int32, own.shape, 0)
    query = lax.broadcasted_iota(jnp.int32, own.shape, 1)
    own = jnp.where(key <= query, own, -jnp.inf)
    s_ref[n_past:n_past + ATT_TILE, :] = own
    m = jnp.max(own, axis=0, keepdims=True)
    if c > 0:
        past = s_t[:n_past, :]
        s_ref[:n_past, :] = past
        m = jnp.maximum(m, jnp.max(past, axis=0, keepdims=True))
    return m


def _store_numerators(s_ref, p_ref, m, c):
    n_kv = (c + 1) * ATT_TILE
    p_ref[:n_kv, :] = jnp.exp2(s_ref[:n_kv, :] - m).astype(BF16)


def _weighted_values(p_ref, vt_ref, c, n_rows):
    n_kv = (c + 1) * ATT_TILE
    o_aug = _dot(vt_ref[:, :n_kv], p_ref[:n_kv, :])
    return o_aug[:n_rows] * (1.0 / o_aug[n_rows:n_rows + 1])


def _tile_order(n_tiles):
    return list(range(0, n_tiles, 2)) + list(range(1, n_tiles, 2))[::-1]


def _staged(n_tiles, scores_fn, output_fn, prepare_fn, s_refs, p_refs, ahead_fn=None):
    prepare_fn()
    order = _tile_order(n_tiles)
    tile = lambda i: order[i] if 0 <= i < n_tiles else None
    maxes, ahead = {}, {}
    for i in range(-2, n_tiles + 1):
        if ahead_fn is not None and tile(i + 2) is not None:
            for half in range(2):
                ahead[i + 2, half] = ahead_fn(tile(i + 2), half)
        for half in range(2):
            if tile(i + 1) is not None:
                maxes[i + 1, half] = scores_fn(tile(i + 1), half, ahead.pop((i + 1, half), None),
                                               s_refs[_slot(i + 1, half)])
            if tile(i) is not None:
                _store_numerators(s_refs[_slot(i, half)], p_refs[_slot(i, half)], maxes.pop((i, half)), tile(i))
        if tile(i - 1) is not None:
            output_fn(tile(i - 1), [p_refs[_slot(i - 1, half)] for half in range(2)])


def _lane_halves():
    lane = lax.broadcasted_iota(jnp.int32, (1, LANES), 1)
    first = lane < HEAD_DIM
    return first, jnp.logical_not(first)


def _tile_rows(c):
    return slice(c * ATT_TILE, (c + 1) * ATT_TILE)


def _transposed_values(v_ref):
    return v_ref[...].astype(F32).T


KMEAN_ROWS = 16
GATE_ROWS = 8


def _moba_prepare(k_ref, v_ref, kaug_ref, kmean_ref, vt_ref):
    seq = k_ref.shape[0]
    n_blocks = seq // MOBA_BLOCK
    kaug_ref[:, :LANES] = k_ref[...]
    row_block = lax.broadcasted_iota(jnp.int32, (seq, LANES), 0) // MOBA_BLOCK
    lane = lax.broadcasted_iota(jnp.int32, (seq, LANES), 1)
    kaug_ref[:, LANES:] = jnp.where(row_block == lane, 1.0, 0.0).astype(BF16)
    means = [jnp.mean(k_ref[_tile_rows(j), :].astype(F32), axis=0, keepdims=True) for j in range(n_blocks)]
    means = jnp.concatenate(means + [jnp.zeros((KMEAN_ROWS - n_blocks, LANES), F32)], axis=0)
    hi = means.astype(BF16)
    lo = (means - hi.astype(F32)).astype(BF16)
    kmean_ref[...] = jnp.concatenate([hi, lo], axis=1)
    v_t = _transposed_values(v_ref)
    for head in range(LANES // HEAD_DIM):
        vt_ref[head, :HEAD_DIM, :] = v_t[head * HEAD_DIM:(head + 1) * HEAD_DIM].astype(BF16)
        vt_ref[head, HEAD_DIM:, :] = jnp.ones((ONES_ROWS, seq), BF16)


def _moba_block_bias(c, head, q_ref, kmean_ref):
    if c <= MOBA_TOPK:
        return None
    qh = _head_queries(c, head, q_ref)
    gate = _dot_nt(kmean_ref[...], jnp.concatenate([qh, qh], axis=1))[:GATE_ROWS, :]
    blk = lax.broadcasted_iota(jnp.int32, gate.shape, 0)
    cnt = jnp.zeros(gate.shape, jnp.int32)
    for m in range(c):
        gm = gate[m:m + 1, :]
        cnt = cnt + jnp.where(blk > m, jnp.where(gm >= gate, 1, 0), jnp.where(gm > gate, 1, 0))
    bias_t = jnp.where(blk < c, jnp.where(cnt < MOBA_TOPK, 0.0, MASKED), 0.0)
    bias_t = jnp.concatenate([bias_t, jnp.zeros((LANES - GATE_ROWS, gate.shape[1]), F32)], axis=0)
    return bias_t.T.astype(BF16)


def _head_queries(c, head, q_ref):
    q = q_ref[_tile_rows(c), :]
    return jnp.where(_lane_halves()[head], q, jnp.zeros_like(q))


def _moba_scores(c, head, bias, s_ref, q_ref, k_ref, kaug_ref):
    n_kv = (c + 1) * ATT_TILE
    qh = _head_queries(c, head, q_ref)
    if bias is None:
        s_t = _dot_nt(k_ref[:n_kv, :], qh)
    else:
        s_t = _dot_nt(kaug_ref[:n_kv, :], jnp.concatenate([qh, bias], axis=1))
    return _store_scores(s_ref, c, s_t)


def _moba_output(c, p_refs, vt_ref, gn_ref, o_ref):
    outs = []
    for head in range(LANES // HEAD_DIM):
        o_h = _weighted_values(p_refs[head], vt_ref.at[head], c, HEAD_DIM)
        ms = jnp.mean(o_h * o_h, axis=0, keepdims=True)
        outs.append(o_h * lax.rsqrt(ms + EPS))
    o_t = jnp.concatenate(outs, axis=0)
    o_ref[_tile_rows(c), :] = (o_t.T * gn_ref[...]).astype(BF16)


def _moba_kernel(q_ref, k_ref, v_ref, gn_ref, o_ref, kaug_ref, kmean_ref, vt_ref, *slot_refs):
    s_refs, p_refs = slot_refs[:2 * PIPE_SLOTS], slot_refs[2 * PIPE_SLOTS:]
    _staged(k_ref.shape[0] // ATT_TILE,
            functools.partial(_moba_scores, q_ref=q_ref, k_ref=k_ref, kaug_ref=kaug_ref),
            functools.partial(_moba_output, vt_ref=vt_ref, gn_ref=gn_ref, o_ref=o_ref),
            functools.partial(_moba_prepare, k_ref, v_ref, kaug_ref, kmean_ref, vt_ref), s_refs, p_refs,
            ahead_fn=functools.partial(_moba_block_bias, q_ref=q_ref, kmean_ref=kmean_ref))


def _moba_call(proj, out_gain, layer, width):
    batch, seq, _ = proj.shape
    n_pairs = width // LANES
    col_spec = lambda c0: pl.BlockSpec((None, seq, LANES), lambda b, p: (b, 0, c0 + p))
    return pl.pallas_call(
        _moba_kernel,
        grid=(batch, n_pairs),
        in_specs=[col_spec(0), col_spec(n_pairs), col_spec(2 * n_pairs),
                  pl.BlockSpec((None, 1, LANES), lambda b, p: (layer, 0, 0))],
        out_specs=col_spec(0),
        out_shape=jax.ShapeDtypeStruct((batch, seq, width), BF16),
        scratch_shapes=[pltpu.VMEM((seq, 2 * LANES), BF16), pltpu.VMEM((KMEAN_ROWS, 2 * LANES), BF16),
                        pltpu.VMEM((LANES // HEAD_DIM, HEAD_DIM + ONES_ROWS, seq), BF16),
                        *_slot_scratch(seq)],
        compiler_params=pltpu.CompilerParams(
            dimension_semantics=("parallel", "parallel"), vmem_limit_bytes=VMEM_LIMIT),
        name="moba_attention",
    )(proj, proj, proj, out_gain)


def _diff_prepare(v_ref, vt_ref):
    vt_ref[:LANES, :] = _transposed_values(v_ref).astype(BF16)
    vt_ref[LANES:, :] = jnp.ones((ONES_ROWS, v_ref.shape[0]), BF16)


def _diff_scores(c, comp, _, s_ref, q_ref, k_ref):
    q = q_ref[_tile_rows(c), :]
    qh = jnp.where(_lane_halves()[comp], q, jnp.zeros_like(q))
    return _store_scores(s_ref, c, _dot_nt(k_ref[:(c + 1) * ATT_TILE, :], qh))


def _diff_output(c, p_refs, out_scale, lam_ref, vt_ref, gn_ref, o_ref):
    a1, a2 = [_weighted_values(p_ref, vt_ref, c, LANES) for p_ref in p_refs]
    o_t = a1 - lam_ref[0:1, 0:1] * a2
    ms = jnp.mean(o_t * o_t, axis=0, keepdims=True)
    o_t = o_t * lax.rsqrt(ms + EPS)
    o_ref[_tile_rows(c), :] = ((o_t.T * gn_ref[...]) * out_scale).astype(BF16)


def _diff_kernel(out_scale, lam_ref, q_ref, k_ref, v_ref, gn_ref, o_ref, vt_ref, *slot_refs):
    s_refs, p_refs = slot_refs[:2 * PIPE_SLOTS], slot_refs[2 * PIPE_SLOTS:]
    _staged(k_ref.shape[0] // ATT_TILE,
            functools.partial(_diff_scores, q_ref=q_ref, k_ref=k_ref),
            functools.partial(_diff_output, out_scale=out_scale, lam_ref=lam_ref, vt_ref=vt_ref,
                              gn_ref=gn_ref, o_ref=o_ref),
            functools.partial(_diff_prepare, v_ref, vt_ref), s_refs, p_refs)


def _diff_call(proj, lam, layer, out_gain, width, col0, out_scale):
    batch, seq, _ = proj.shape
    n_heads = width // LANES
    c0 = col0 // LANES
    col_spec = lambda c: pl.BlockSpec((None, seq, LANES), lambda b, h: (b, 0, c + h))
    return pl.pallas_call(
        functools.partial(_diff_kernel, out_scale),
        grid=(batch, n_heads),
        in_specs=[pl.BlockSpec((None, 1, LANES), lambda b, h: (layer, 0, 0)),
                  col_spec(c0), col_spec(c0 + n_heads), col_spec(c0 + 2 * n_heads),
                  pl.BlockSpec((None, 1, LANES), lambda b, h: (layer, 0, 0))],
        out_specs=col_spec(0),
        out_shape=jax.ShapeDtypeStruct((batch, seq, width), BF16),
        scratch_shapes=[pltpu.VMEM((LANES + ONES_ROWS, seq), BF16),
                        *_slot_scratch(seq)],
        compiler_params=pltpu.CompilerParams(
            dimension_semantics=("parallel", "parallel"), vmem_limit_bytes=VMEM_LIMIT),
        name="diff_attention",
    )(lam, proj, proj, proj, out_gain)


def _mlp_kernel(om_ref, od_ref, wo_ref, x_ref, ga_ref, nf_ref, sc_ref, sh_ref, gf_ref, wg_ref, wu_ref,
                wd_ref, o_ref, act_ref):
    width = om_ref.shape[1]
    y = _dot(om_ref[...], wo_ref[:width, :]) + _dot(od_ref[...], wo_ref[width:, :])
    x = x_ref[...] + ga_ref[...] * y
    h = _adaln(x, nf_ref[...], sc_ref[...], sh_ref[...]).astype(BF16)
    for c in range(wg_ref.shape[1] // MXU_DIM):
        cols = slice(c * MXU_DIM, (c + 1) * MXU_DIM)
        g = _dot(h, wg_ref[:, cols])
        u = _dot(h, wu_ref[:, cols])
        act_ref[:, cols] = ((g / (1.0 + jnp.exp(-g))) * u).astype(BF16)
    o_ref[...] = x + gf_ref[...] * _dot(act_ref[...], wd_ref[...])


def _mlp_call(o_m, o_d, x, mod5, layer, norm_gain, w_out, w_gate, w_up, w_down):
    batch, seq, d_model = x.shape
    d_ff = w_gate.shape[2]
    row_spec = lambda cols: pl.BlockSpec((None, ROW_TILE, cols), lambda b, i: (b, i, 0))
    resident = lambda w: pl.BlockSpec((None,) + w.shape[1:], lambda b, i: (layer, 0, 0),
                                      pipeline_mode=pl.Buffered(1))
    return pl.pallas_call(
        _mlp_kernel,
        grid=(batch, seq // ROW_TILE),
        in_specs=[
            row_spec(o_m.shape[2]), row_spec(o_d.shape[2]),
            resident(w_out),
            row_spec(d_model),
            _mod_spec(d_model, layer, 2),
            _layer_spec(norm_gain, layer),
            _mod_spec(d_model, layer, 4),
            _mod_spec(d_model, layer, 3),
            _mod_spec(d_model, layer, 5),
            resident(w_gate), resident(w_up), resident(w_down),
        ],
        out_specs=row_spec(d_model),
        out_shape=jax.ShapeDtypeStruct((batch, seq, d_model), F32),
        scratch_shapes=[pltpu.VMEM((ROW_TILE, d_ff), BF16)],
        compiler_params=pltpu.CompilerParams(
            dimension_semantics=("parallel", "parallel"), vmem_limit_bytes=VMEM_LIMIT),
        name="out_projection_swiglu",
    )(o_m, o_d, w_out, x, mod5, norm_gain, mod5, mod5, mod5, w_gate, w_up, w_down)


def kernel(x, c, positions, w_mod, b_mod, norm_mix, w_in, moba_q_norm, moba_k_norm, moba_out_norm,
           diff_q_norm, diff_k_norm, diff_lambda, diff_subln, w_out, norm_ffn, w_gate, w_up, w_down):
    batch, seq, d_model = x.shape
    depth = w_mod.shape[0]
    moba_width = d_model // 2
    diff_width = d_model // 2
    assert seq % ROW_TILE == 0 and seq % MOBA_BLOCK == 0 and ATT_TILE == MOBA_BLOCK
    assert seq // MOBA_BLOCK <= GATE_ROWS
    assert w_in.shape[2] == 3 * moba_width + 3 * diff_width and moba_width == 2 * MXU_DIM

    lam_inits = tuple(0.8 - 0.6 * math.exp(-0.3 * l) for l in range(depth))
    mod5 = _mod_call(c, w_mod, b_mod).reshape(depth, batch, N_MOD, 1, d_model)
    cs, lam = _trig_call(positions, diff_lambda, lam_inits)
    rope_expand, rope_base = _rope_expansion()

    head_of_lane = np.arange(MXU_DIM) // HEAD_DIM
    group_mean = jnp.asarray((head_of_lane[:, None] == head_of_lane[None, :]) / HEAD_DIM, BF16)
    reps = MXU_DIM // HEAD_DIM
    qk_scale = HEAD_DIM ** -0.5 * math.log2(math.e)
    gains = jnp.stack([jnp.tile(moba_q_norm, (1, reps)) * qk_scale, jnp.tile(moba_k_norm, (1, reps)),
                       jnp.tile(diff_q_norm, (1, reps)) * qk_scale, jnp.tile(diff_k_norm, (1, reps))],
                      axis=1)
    moba_out_gain = jnp.tile(moba_out_norm, (1, LANES // HEAD_DIM))[:, None, :]
    diff_out_gain = diff_subln[:, None, :]
    norm_mix3, norm_ffn3 = norm_mix[:, None, :], norm_ffn[:, None, :]

    w_in_b, w_out_b = w_in.astype(BF16), w_out.astype(BF16)
    w_gate_b, w_up_b, w_down_b = w_gate.astype(BF16), w_up.astype(BF16), w_down.astype(BF16)

    for l in range(depth):
        proj = _inproj_call(x, mod5, l, norm_mix3, w_in_b, group_mean, gains, cs, rope_expand, rope_base)
        o_m = _moba_call(proj, moba_out_gain, l, moba_width)
        o_d = _diff_call(proj, lam, l, diff_out_gain, diff_width, 3 * moba_width, 1.0 - lam_inits[l])
        x = _mlp_call(o_m, o_d, x, mod5, l, norm_ffn3, w_out_b, w_gate_b, w_up_b, w_down_b)
    return x
```

```python
import functools
import math

import numpy as np
import jax
import jax.numpy as jnp
from jax import lax
from jax.experimental import pallas as pl
from jax.experimental.pallas import tpu as pltpu

F32 = jnp.float32
BF16 = jnp.bfloat16
HIGHEST = lax.Precision.HIGHEST

LANES = 128
MXU_DIM = 256
VMEM_LIMIT = 56 * 1024 * 1024

HEAD_DIM = 64
ROPE_DIM = HEAD_DIM // 4
ROPE_HALF = ROPE_DIM // 2
ROPE_THETA = 500000.0
MOBA_BLOCK = 256
MOBA_TOPK = 3
N_MOD = 6
EPS = 1e-6

ROW_TILE = 512
ATT_TILE = 256
MOD_COL_TILE = 1536


def _dot(a, b):
    return jnp.dot(a, b, preferred_element_type=F32)


def _dot_nt(a, b):
    return lax.dot_general(a, b, (((1,), (1,)), ((), ())), preferred_element_type=F32)


def _mod_kernel(c_ref, w_ref, b_ref, o_ref):
    c = c_ref[...]
    cond = c / (1.0 + jnp.exp(-c))
    o_ref[...] = jnp.dot(cond, w_ref[...], preferred_element_type=F32, precision=HIGHEST) + b_ref[...]


def _mod_call(c, w_mod, b_mod):
    depth, d_model, n_out = w_mod.shape
    batch = c.shape[0]
    return pl.pallas_call(
        _mod_kernel,
        grid=(depth, n_out // MOD_COL_TILE),
        in_specs=[
            pl.BlockSpec((batch, d_model), lambda l, j: (0, 0)),
            pl.BlockSpec((None, d_model, MOD_COL_TILE), lambda l, j: (l, 0, j)),
            pl.BlockSpec((None, 1, MOD_COL_TILE), lambda l, j: (l, 0, j)),
        ],
        out_specs=pl.BlockSpec((None, batch, MOD_COL_TILE), lambda l, j: (l, 0, j)),
        out_shape=jax.ShapeDtypeStruct((depth, batch, n_out), F32),
        compiler_params=pltpu.CompilerParams(
            dimension_semantics=("arbitrary", "arbitrary"), vmem_limit_bytes=VMEM_LIMIT),
        name="adaln_mod",
    )(c, w_mod, b_mod.reshape(depth, 1, n_out))


N_PIECES = 3


def _trig_kernel(lam_inits, pos_ref, inv_ref, dl_ref, cs_ref, lam_ref):
    ang = pos_ref[...].astype(F32) * inv_ref[...]
    lane = lax.broadcasted_iota(jnp.int32, ang.shape, 1)
    cs = jnp.where((lane & (ROPE_DIM - 1)) < ROPE_HALF, jnp.cos(ang), jnp.sin(ang))
    for i in range(N_PIECES):
        piece = cs.astype(BF16)
        cs_ref[i] = piece
        cs = cs - piece.astype(F32)
    for l, lam_init in enumerate(lam_inits):
        lp = dl_ref[l]
        a = jnp.sum(lp[0:1] * lp[1:2], axis=-1, keepdims=True)
        b = jnp.sum(lp[2:3] * lp[3:4], axis=-1, keepdims=True)
        lam_ref[l] = jnp.broadcast_to(jnp.exp(a) - jnp.exp(b) + lam_init, (1, LANES))


def _trig_call(positions, diff_lambda, lam_inits):
    batch, seq = positions.shape
    depth = diff_lambda.shape[0]
    inv = ROPE_THETA ** (-jnp.arange(0, ROPE_DIM, 2, dtype=F32) / ROPE_DIM)
    n_rows = batch * seq * ROPE_DIM // LANES
    pos_rep = jnp.broadcast_to(positions[..., None], (batch, seq, ROPE_DIM)).reshape(n_rows, LANES)
    inv_row = jnp.tile(jnp.concatenate([inv, inv]), LANES // ROPE_DIM)[None, :]
    cs, lam = pl.pallas_call(
        functools.partial(_trig_kernel, lam_inits),
        out_shape=(jax.ShapeDtypeStruct((N_PIECES, n_rows, LANES), BF16),
                   jax.ShapeDtypeStruct((depth, 1, LANES), F32)),
        name="rope_trig_lambda",
    )(pos_rep, inv_row, diff_lambda)
    cs = cs.reshape(N_PIECES, batch, seq, ROPE_DIM)
    return jnp.concatenate([cs[i] for i in range(N_PIECES)], axis=-1), lam


def _rope_expansion():
    expand = np.zeros((ROPE_DIM, 3 * LANES), np.float32)
    base = np.zeros((1, 3 * LANES), np.float32)
    base[0, :LANES] = 1.0
    for h0 in range(0, LANES, HEAD_DIM):
        base[0, h0:h0 + ROPE_DIM] = 0.0
        for j in range(ROPE_HALF):
            expand[j, h0 + j] = 1.0
            expand[j, h0 + ROPE_HALF + j] = 1.0
            expand[ROPE_HALF + j, LANES + h0 + j] = -1.0
            expand[ROPE_HALF + j, 2 * LANES + h0 + ROPE_HALF + j] = 1.0
    return jnp.asarray(np.tile(expand, (N_PIECES, 1)), BF16), jnp.asarray(base)


def _adaln(x, norm_gain, scale, shift):
    ms = jnp.mean(x * x, axis=-1, keepdims=True)
    return (x * lax.rsqrt(ms + EPS)) * (norm_gain * (1.0 + scale)) + shift


def _inproj_kernel(x_ref, nm_ref, sc_ref, sh_ref, w_ref, gsum_ref, gains_ref, cs_ref, exp_ref, base_ref,
                   o_ref):
    h = _adaln(x_ref[...], nm_ref[...], sc_ref[...], sh_ref[...]).astype(BF16)
    tab = base_ref[...] + _dot(cs_ref[...], exp_ref[...])
    rope_c, rope_n, rope_p = tab[:, :LANES], tab[:, LANES:2 * LANES], tab[:, 2 * LANES:]
    group_mean = gsum_ref[...]
    n_chunks = w_ref.shape[1] // MXU_DIM
    gain_row = {0: 0, 1: 1, 3: 2, 4: 3}

    def project(c):
        return _dot(h, w_ref[:, c * MXU_DIM:(c + 1) * MXU_DIM])

    def finish(c, y):
        kind = c // 2
        if kind not in gain_row:
            o_ref[:, c * MXU_DIM:(c + 1) * MXU_DIM] = y.astype(BF16)
            return
        inv_rms = lax.rsqrt(_dot((y * y).astype(BF16), group_mean) + EPS)
        r = gain_row[kind]
        y = y * gains_ref[r:r + 1, :]
        for half in range(MXU_DIM // LANES):
            lanes = slice(half * LANES, (half + 1) * LANES)
            yh = y[:, lanes]
            yh = (yh * rope_c + pltpu.roll(yh, LANES - ROPE_HALF, 1) * rope_n
                  + pltpu.roll(yh, ROPE_HALF, 1) * rope_p)
            lo = c * MXU_DIM + half * LANES
            o_ref[:, lo:lo + LANES] = (yh * inv_rms[:, lanes]).astype(BF16)

    upcoming = project(0)
    for c in range(n_chunks):
        y = upcoming
        if c + 1 < n_chunks:
            upcoming = project(c + 1)
        finish(c, y)


def _mod_spec(d_model, layer, k):
    return pl.BlockSpec((None, None, None, 1, d_model), lambda b, i: (layer, b, k, 0, 0))


def _layer_spec(array, layer):
    return pl.BlockSpec((None,) + array.shape[1:], lambda b, i: (layer, 0, 0))


def _const_spec(array):
    return pl.BlockSpec(array.shape, lambda b, i: (0,) * array.ndim)


def _inproj_call(x, mod5, layer, norm_gain, w_in, group_mean, gains, cs, rope_expand, rope_base):
    batch, seq, d_model = x.shape
    n_out = w_in.shape[2]
    return pl.pallas_call(
        _inproj_kernel,
        grid=(batch, seq // ROW_TILE),
        in_specs=[
            pl.BlockSpec((None, ROW_TILE, d_model), lambda b, i: (b, i, 0)),
            _layer_spec(norm_gain, layer),
            _mod_spec(d_model, layer, 1),
            _mod_spec(d_model, layer, 0),
            _layer_spec(w_in, layer),
            _const_spec(group_mean),
            _layer_spec(gains, layer),
            pl.BlockSpec((None, ROW_TILE, N_PIECES * ROPE_DIM), lambda b, i: (b, i, 0)),
            _const_spec(rope_expand),
            _const_spec(rope_base),
        ],
        out_specs=pl.BlockSpec((None, ROW_TILE, n_out), lambda b, i: (b, i, 0)),
        out_shape=jax.ShapeDtypeStruct((batch, seq, n_out), BF16),
        compiler_params=pltpu.CompilerParams(
            dimension_semantics=("parallel", "parallel"), vmem_limit_bytes=VMEM_LIMIT),
        name="in_projection",
    )(x, norm_gain, mod5, mod5, w_in, group_mean, gains, cs, rope_expand, rope_base)


MASKED = -1e30
ONES_ROWS = 16
PIPE_SLOTS = 2
GROUPS = 2


def _slot(i, half):
    return 2 * (i % PIPE_SLOTS) + half


def _slot_scratch(seq):
    return ([pltpu.VMEM((seq, ATT_TILE), F32) for _ in range(2 * PIPE_SLOTS)]
            + [pltpu.VMEM((seq, ATT_TILE), BF16) for _ in range(2 * PIPE_SLOTS)])


def _store_scores(s_ref, c, s_t):
    n_past = c * ATT_TILE
    own = s_t[n_past:, :]
    key = lax.broadcasted_iota(jnp.int32, own.shape, 0)
    query = lax.broadcasted_iota(jnp.int32, own.shape, 1)
    own = jnp.where(key <= query, own, -jnp.inf)
    s_ref[n_past:n_past + ATT_TILE, :] = own
    m = jnp.max(own, axis=0, keepdims=True)
    if c > 0:
        past = s_t[:n_past, :]
        s_ref[:n_past, :] = past
        m = jnp.maximum(m, jnp.max(past, axis=0, keepdims=True))
    return m


def _store_numerators(s_ref, p_ref, m, c):
    n_kv = (c + 1) * ATT_TILE
    p_ref[:n_kv, :] = jnp.exp2(s_ref[:n_kv, :] - m).astype(BF16)


def _weighted_values(p_ref, vt_ref, c, n_rows):
    n_kv = (c + 1) * ATT_TILE
    o_aug = _dot(vt_ref[:, :n_kv], p_ref[:n_kv, :])
    return o_aug[:n_rows] * (1.0 / o_aug[n_rows:n_rows + 1])


def _tile_order(n_tiles):
    return list(range(0, n_tiles, 2)) + list(range(1, n_tiles, 2))[::-1]


def _staged(groups, n_tiles, s_refs, p_refs):
    for group in groups:
        group["prepare"]()
    jobs = [(group, c) for group in groups for c in _tile_order(n_tiles)]
    job = lambda i: jobs[i] if 0 <= i < len(jobs) else None
    maxes, ahead = {}, {}
    for i in range(-2, len(jobs) + 1):
        if job(i + 2) is not None and "ahead" in job(i + 2)[0]:
            group, c = job(i + 2)
            for half in range(2):
                ahead[i + 2, half] = group["ahead"](c, half)
        for half in range(2):
            if job(i + 1) is not None:
                group, c = job(i + 1)
                maxes[i + 1, half] = group["scores"](c, half, ahead.pop((i + 1, half), None),
                                                     s_refs[_slot(i + 1, half)])
            if job(i) is not None:
                _store_numerators(s_refs[_slot(i, half)], p_refs[_slot(i, half)], maxes.pop((i, half)),
                                  job(i)[1])
        if job(i - 1) is not None:
            group, c = job(i - 1)
            group["output"](c, [p_refs[_slot(i - 1, half)] for half in range(2)])


def _lane_block(ref, g):
    return ref.at[:, g * LANES:(g + 1) * LANES]


def _lane_halves():
    lane = lax.broadcasted_iota(jnp.int32, (1, LANES), 1)
    first = lane < HEAD_DIM
    return first, jnp.logical_not(first)


def _tile_rows(c):
    return slice(c * ATT_TILE, (c + 1) * ATT_TILE)


def _transposed_values(v_ref):
    return v_ref[...].astype(F32).T


KMEAN_ROWS = 16
GATE_ROWS = 8


def _moba_prepare(k_ref, v_ref, kaug_ref, kmean_ref, vt_ref):
    seq = k_ref.shape[0]
    n_blocks = seq // MOBA_BLOCK
    kaug_ref[:, :LANES] = k_ref[...]
    row_block = lax.broadcasted_iota(jnp.int32, (seq, LANES), 0) // MOBA_BLOCK
    lane = lax.broadcasted_iota(jnp.int32, (seq, LANES), 1)
    kaug_ref[:, LANES:] = jnp.where(row_block == lane, 1.0, 0.0).astype(BF16)
    means = [jnp.mean(k_ref[_tile_rows(j), :].astype(F32), axis=0, keepdims=True) for j in range(n_blocks)]
    means = jnp.concatenate(means + [jnp.zeros((KMEAN_ROWS - n_blocks, LANES), F32)], axis=0)
    hi = means.astype(BF16)
    lo = (means - hi.astype(F32)).astype(BF16)
    kmean_ref[...] = jnp.concatenate([hi, lo], axis=1)
    v_t = _transposed_values(v_ref)
    for head in range(LANES // HEAD_DIM):
        vt_ref[head, :HEAD_DIM, :] = v_t[head * HEAD_DIM:(head + 1) * HEAD_DIM].astype(BF16)
        vt_ref[head, HEAD_DIM:, :] = jnp.ones((ONES_ROWS, seq), BF16)


def _moba_block_bias(c, head, q_ref, kmean_ref):
    if c <= MOBA_TOPK:
        return None
    qh = _head_queries(c, head, q_ref)
    gate = _dot_nt(kmean_ref[...], jnp.concatenate([qh, qh], axis=1))[:GATE_ROWS, :]
    blk = lax.broadcasted_iota(jnp.int32, gate.shape, 0)
    cnt = jnp.zeros(gate.shape, jnp.int32)
    for m in range(c):
        gm = gate[m:m + 1, :]
        cnt = cnt + jnp.where(blk > m, jnp.where(gm >= gate, 1, 0), jnp.where(gm > gate, 1, 0))
    bias_t = jnp.where(blk < c, jnp.where(cnt < MOBA_TOPK, 0.0, MASKED), 0.0)
    bias_t = jnp.concatenate([bias_t, jnp.zeros((LANES - GATE_ROWS, gate.shape[1]), F32)], axis=0)
    return bias_t.T.astype(BF16)


def _head_queries(c, head, q_ref):
    q = q_ref[_tile_rows(c), :]
    return jnp.where(_lane_halves()[head], q, jnp.zeros_like(q))


def _moba_scores(c, head, bias, s_ref, q_ref, k_ref, kaug_ref):
    n_kv = (c + 1) * ATT_TILE
    qh = _head_queries(c, head, q_ref)
    if bias is None:
        s_t = _dot_nt(k_ref[:n_kv, :], qh)
    else:
        s_t = _dot_nt(kaug_ref[:n_kv, :], jnp.concatenate([qh, bias], axis=1))
    return _store_scores(s_ref, c, s_t)


def _moba_output(c, p_refs, vt_ref, gn_ref, o_ref):
    outs = []
    for head in range(LANES // HEAD_DIM):
        o_h = _weighted_values(p_refs[head], vt_ref.at[head], c, HEAD_DIM)
        ms = jnp.mean(o_h * o_h, axis=0, keepdims=True)
        outs.append(o_h * lax.rsqrt(ms + EPS))
    o_t = jnp.concatenate(outs, axis=0)
    o_ref[_tile_rows(c), :] = (o_t.T * gn_ref[...]).astype(BF16)


def _moba_kernel(q_ref, k_ref, v_ref, gn_ref, o_ref, kaug_ref, kmean_ref, vt_ref, *slot_refs):
    s_refs, p_refs = slot_refs[:2 * PIPE_SLOTS], slot_refs[2 * PIPE_SLOTS:]
    groups = []
    for g in range(GROUPS):
        q_g, k_g, v_g, o_g = [_lane_block(ref, g) for ref in (q_ref, k_ref, v_ref, o_ref)]
        kaug_g, kmean_g, vt_g = kaug_ref.at[g], kmean_ref.at[g], vt_ref.at[g]
        groups.append(dict(
            prepare=functools.partial(_moba_prepare, k_g, v_g, kaug_g, kmean_g, vt_g),
            ahead=functools.partial(_moba_block_bias, q_ref=q_g, kmean_ref=kmean_g),
            scores=functools.partial(_moba_scores, q_ref=q_g, k_ref=k_g, kaug_ref=kaug_g),
            output=functools.partial(_moba_output, vt_ref=vt_g, gn_ref=gn_ref, o_ref=o_g)))
    _staged(groups, k_ref.shape[0] // ATT_TILE, s_refs, p_refs)


def _moba_call(proj, out_gain, layer, width):
    batch, seq, _ = proj.shape
    n_steps = width // (GROUPS * LANES)
    col_spec = lambda c0: pl.BlockSpec((None, seq, GROUPS * LANES), lambda b, p: (b, 0, c0 + p))
    return pl.pallas_call(
        _moba_kernel,
        grid=(batch, n_steps),
        in_specs=[col_spec(0), col_spec(n_steps), col_spec(2 * n_steps),
                  pl.BlockSpec((None, 1, LANES), lambda b, p: (layer, 0, 0))],
        out_specs=col_spec(0),
        out_shape=jax.ShapeDtypeStruct((batch, seq, width), BF16),
        scratch_shapes=[pltpu.VMEM((GROUPS, seq, 2 * LANES), BF16),
                        pltpu.VMEM((GROUPS, KMEAN_ROWS, 2 * LANES), BF16),
                        pltpu.VMEM((GROUPS, LANES // HEAD_DIM, HEAD_DIM + ONES_ROWS, seq), BF16),
                        *_slot_scratch(seq)],
        compiler_params=pltpu.CompilerParams(
            dimension_semantics=("parallel", "parallel"), vmem_limit_bytes=VMEM_LIMIT),
        name="moba_attention",
    )(proj, proj, proj, out_gain)


def _diff_prepare(v_ref, vt_ref):
    vt_ref[:LANES, :] = _transposed_values(v_ref).astype(BF16)
    vt_ref[LANES:, :] = jnp.ones((ONES_ROWS, v_ref.shape[0]), BF16)


def _diff_scores(c, comp, _, s_ref, q_ref, k_ref):
    q = q_ref[_tile_rows(c), :]
    qh = jnp.where(_lane_halves()[comp], q, jnp.zeros_like(q))
    return _store_scores(s_ref, c, _dot_nt(k_ref[:(c + 1) * ATT_TILE, :], qh))


def _diff_output(c, p_refs, out_scale, lam_ref, vt_ref, gn_ref, o_ref):
    a1, a2 = [_weighted_values(p_ref, vt_ref, c, LANES) for p_ref in p_refs]
    o_t = a1 - lam_ref[0:1, 0:1] * a2
    ms = jnp.mean(o_t * o_t, axis=0, keepdims=True)
    o_t = o_t * lax.rsqrt(ms + EPS)
    o_ref[_tile_rows(c), :] = ((o_t.T * gn_ref[...]) * out_scale).astype(BF16)


def _diff_kernel(out_scale, lam_ref, q_ref, k_ref, v_ref, gn_ref, o_ref, vt_ref, *slot_refs):
    s_refs, p_refs = slot_refs[:2 * PIPE_SLOTS], slot_refs[2 * PIPE_SLOTS:]
    groups = []
    for g in range(GROUPS):
        q_g, k_g, v_g, o_g = [_lane_block(ref, g) for ref in (q_ref, k_ref, v_ref, o_ref)]
        groups.append(dict(
            prepare=functools.partial(_diff_prepare, v_g, vt_ref.at[g]),
            scores=functools.partial(_diff_scores, q_ref=q_g, k_ref=k_g),
            output=functools.partial(_diff_output, out_scale=out_scale, lam_ref=lam_ref, vt_ref=vt_ref.at[g],
                                     gn_ref=gn_ref, o_ref=o_g)))
    _staged(groups, k_ref.shape[0] // ATT_TILE, s_refs, p_refs)


def _diff_call(proj, lam, layer, out_gain, width, col0, out_scale):
    batch, seq, _ = proj.shape
    n_steps = width // (GROUPS * LANES)
    c0 = col0 // (GROUPS * LANES)
    col_spec = lambda c: pl.BlockSpec((None, seq, GROUPS * LANES), lambda b, h: (b, 0, c + h))
    return pl.pallas_call(
        functools.partial(_diff_kernel, out_scale),
        grid=(batch, n_steps),
        in_specs=[pl.BlockSpec((None, 1, LANES), lambda b, h: (layer, 0, 0)),
                  col_spec(c0), col_spec(c0 + n_steps), col_spec(c0 + 2 * n_steps),
                  pl.BlockSpec((None, 1, LANES), lambda b, h: (layer, 0, 0))],
        out_specs=col_spec(0),
        out_shape=jax.ShapeDtypeStruct((batch, seq, width), BF16),
        scratch_shapes=[pltpu.VMEM((GROUPS, LANES + ONES_ROWS, seq), BF16),
                        *_slot_scratch(seq)],
        compiler_params=pltpu.CompilerParams(
            dimension_semantics=("parallel", "parallel"), vmem_limit_bytes=VMEM_LIMIT),
        name="diff_attention",
    )(lam, proj, proj, proj, out_gain)


def _mlp_kernel(om_ref, od_ref, wo_ref, x_ref, ga_ref, nf_ref, sc_ref, sh_ref, gf_ref, wg_ref, wu_ref,
                wd_ref, o_ref, act_ref):
    width = om_ref.shape[1]
    y = _dot(om_ref[...], wo_ref[:width, :]) + _dot(od_ref[...], wo_ref[width:, :])
    x = x_ref[...] + ga_ref[...] * y
    h = _adaln(x, nf_ref[...], sc_ref[...], sh_ref[...]).astype(BF16)
    for c in range(wg_ref.shape[1] // MXU_DIM):
        cols = slice(c * MXU_DIM, (c + 1) * MXU_DIM)
        g = _dot(h, wg_ref[:, cols])
        u = _dot(h, wu_ref[:, cols])
        act_ref[:, cols] = ((g / (1.0 + jnp.exp(-g))) * u).astype(BF16)
    o_ref[...] = x + gf_ref[...] * _dot(act_ref[...], wd_ref[...])


def _mlp_call(o_m, o_d, x, mod5, layer, norm_gain, w_out, w_gate, w_up, w_down):
    batch, seq, d_model = x.shape
    d_ff = w_gate.shape[2]
    row_spec = lambda cols: pl.BlockSpec((None, ROW_TILE, cols), lambda b, i: (b, i, 0))
    resident = lambda w: pl.BlockSpec((None,) + w.shape[1:], lambda b, i: (layer, 0, 0),
                                      pipeline_mode=pl.Buffered(1))
    return pl.pallas_call(
        _mlp_kernel,
        grid=(batch, seq // ROW_TILE),
        in_specs=[
            row_spec(o_m.shape[2]), row_spec(o_d.shape[2]),
            resident(w_out),
            row_spec(d_model),
            _mod_spec(d_model, layer, 2),
            _layer_spec(norm_gain, layer),
            _mod_spec(d_model, layer, 4),
            _mod_spec(d_model, layer, 3),
            _mod_spec(d_model, layer, 5),
            resident(w_gate), resident(w_up), resident(w_down),
        ],
        out_specs=row_spec(d_model),
        out_shape=jax.ShapeDtypeStruct((batch, seq, d_model), F32),
        scratch_shapes=[pltpu.VMEM((ROW_TILE, d_ff), BF16)],
        compiler_params=pltpu.CompilerParams(
            dimension_semantics=("parallel", "parallel"), vmem_limit_bytes=VMEM_LIMIT),
        name="out_projection_swiglu",
    )(o_m, o_d, w_out, x, mod5, norm_gain, mod5, mod5, mod5, w_gate, w_up, w_down)


def kernel(x, c, positions, w_mod, b_mod, norm_mix, w_in, moba_q_norm, moba_k_norm, moba_out_norm,
           diff_q_norm, diff_k_norm, diff_lambda, diff_subln, w_out, norm_ffn, w_gate, w_up, w_down):
    batch, seq, d_model = x.shape
    depth = w_mod.shape[0]
    moba_width = d_model // 2
    diff_width = d_model // 2
    assert seq % ROW_TILE == 0 and seq % MOBA_BLOCK == 0 and ATT_TILE == MOBA_BLOCK
    assert seq // MOBA_BLOCK <= GATE_ROWS
    assert w_in.shape[2] == 3 * moba_width + 3 * diff_width and moba_width == 2 * MXU_DIM

    lam_inits = tuple(0.8 - 0.6 * math.exp(-0.3 * l) for l in range(depth))
    mod5 = _mod_call(c, w_mod, b_mod).reshape(depth, batch, N_MOD, 1, d_model)
    cs, lam = _trig_call(positions, diff_lambda, lam_inits)
    rope_expand, rope_base = _rope_expansion()

    head_of_lane = np.arange(MXU_DIM) // HEAD_DIM
    group_mean = jnp.asarray((head_of_lane[:, None] == head_of_lane[None, :]) / HEAD_DIM, BF16)
    reps = MXU_DIM // HEAD_DIM
    qk_scale = HEAD_DIM ** -0.5 * math.log2(math.e)
    gains = jnp.stack([jnp.tile(moba_q_norm, (1, reps)) * qk_scale, jnp.tile(moba_k_norm, (1, reps)),
                       jnp.tile(diff_q_norm, (1, reps)) * qk_scale, jnp.tile(diff_k_norm, (1, reps))],
                      axis=1)
    moba_out_gain = jnp.tile(moba_out_norm, (1, LANES // HEAD_DIM))[:, None, :]
    diff_out_gain = diff_subln[:, None, :]
    norm_mix3, norm_ffn3 = norm_mix[:, None, :], norm_ffn[:, None, :]

    w_in_b, w_out_b = w_in.astype(BF16), w_out.astype(BF16)
    w_gate_b, w_up_b, w_down_b = w_gate.astype(BF16), w_up.astype(BF16), w_down.astype(BF16)

    for l in range(depth):
        proj = _inproj_call(x, mod5, l, norm_mix3, w_in_b, group_mean, gains, cs, rope_expand, rope_base)
        o_m = _moba_call(proj, moba_out_gain, l, moba_width)
        o_d = _diff_call(proj, lam, l, diff_out_gain, diff_width, 3 * moba_width, 1.0 - lam_inits[l])
        x = _mlp_call(o_m, o_d, x, mod5, l, norm_ffn3, w_out_b, w_gate_b, w_up_b, w_down_b)
    return x
```

```python
import functools
import math

import numpy as np
import jax
import jax.numpy as jnp
from jax import lax
from jax.experimental import pallas as pl
from jax.experimental.pallas import tpu as pltpu

F32 = jnp.float32
BF16 = jnp.bfloat16
HIGHEST = lax.Precision.HIGHEST

LANES = 128
MXU_DIM = 256
VMEM_LIMIT = 56 * 1024 * 1024

HEAD_DIM = 64
ROPE_DIM = HEAD_DIM // 4
ROPE_HALF = ROPE_DIM // 2
ROPE_THETA = 500000.0
MOBA_BLOCK = 256
MOBA_TOPK = 3
N_MOD = 6
EPS = 1e-6

ROW_TILE = 512
ATT_TILE = 256
MOD_COL_TILE = 1536


def _dot(a, b):
    return jnp.dot(a, b, preferred_element_type=F32)


def _mod_kernel(c_ref, w_ref, b_ref, o_ref):
    c = c_ref[...]
    cond = c / (1.0 + jnp.exp(-c))
    o_ref[...] = jnp.dot(cond, w_ref[...], preferred_element_type=F32, precision=HIGHEST) + b_ref[...]


def _mod_call(c, w_mod, b_mod):
    depth, d_model, n_out = w_mod.shape
    batch = c.shape[0]
    return pl.pallas_call(
        _mod_kernel,
        grid=(depth, n_out // MOD_COL_TILE),
        in_specs=[
            pl.BlockSpec((batch, d_model), lambda l, j: (0, 0)),
            pl.BlockSpec((None, d_model, MOD_COL_TILE), lambda l, j: (l, 0, j)),
            pl.BlockSpec((None, 1, MOD_COL_TILE), lambda l, j: (l, 0, j)),
        ],
        out_specs=pl.BlockSpec((None, batch, MOD_COL_TILE), lambda l, j: (l, 0, j)),
        out_shape=jax.ShapeDtypeStruct((depth, batch, n_out), F32),
        compiler_params=pltpu.CompilerParams(
            dimension_semantics=("arbitrary", "arbitrary"), vmem_limit_bytes=VMEM_LIMIT),
        name="adaln_mod",
    )(c, w_mod, b_mod.reshape(depth, 1, n_out))


N_PIECES = 3


def _trig_kernel(lam_inits, pos_ref, inv_ref, dl_ref, cs_ref, lam_ref):
    ang = pos_ref[...].astype(F32) * inv_ref[...]
    lane = lax.broadcasted_iota(jnp.int32, ang.shape, 1)
    cs = jnp.where((lane & (ROPE_DIM - 1)) < ROPE_HALF, jnp.cos(ang), jnp.sin(ang))
    for i in range(N_PIECES):
        piece = cs.astype(BF16)
        cs_ref[i] = piece
        cs = cs - piece.astype(F32)
    for l, lam_init in enumerate(lam_inits):
        lp = dl_ref[l]
        a = jnp.sum(lp[0:1] * lp[1:2], axis=-1, keepdims=True)
        b = jnp.sum(lp[2:3] * lp[3:4], axis=-1, keepdims=True)
        lam_ref[l] = jnp.broadcast_to(jnp.exp(a) - jnp.exp(b) + lam_init, (1, LANES))


def _trig_call(positions, diff_lambda, lam_inits):
    batch, seq = positions.shape
    depth = diff_lambda.shape[0]
    inv = ROPE_THETA ** (-jnp.arange(0, ROPE_DIM, 2, dtype=F32) / ROPE_DIM)
    n_rows = batch * seq * ROPE_DIM // LANES
    pos_rep = jnp.broadcast_to(positions[..., None], (batch, seq, ROPE_DIM)).reshape(n_rows, LANES)
    inv_row = jnp.tile(jnp.concatenate([inv, inv]), LANES // ROPE_DIM)[None, :]
    cs, lam = pl.pallas_call(
        functools.partial(_trig_kernel, lam_inits),
        out_shape=(jax.ShapeDtypeStruct((N_PIECES, n_rows, LANES), BF16),
                   jax.ShapeDtypeStruct((depth, 1, LANES), F32)),
        name="rope_trig_lambda",
    )(pos_rep, inv_row, diff_lambda)
    cs = cs.reshape(N_PIECES, batch, seq, ROPE_DIM)
    return jnp.concatenate([cs[i] for i in range(N_PIECES)], axis=-1), lam


def _rope_expansion():
    expand = np.zeros((ROPE_DIM, 3 * LANES), np.float32)
    base = np.zeros((1, 3 * LANES), np.float32)
    base[0, :LANES] = 1.0
    for h0 in range(0, LANES, HEAD_DIM):
        base[0, h0:h0 + ROPE_DIM] = 0.0
        for j in range(ROPE_HALF):
            expand[j, h0 + j] = 1.0
            expand[j, h0 + ROPE_HALF + j] = 1.0
            expand[ROPE_HALF + j, LANES + h0 + j] = -1.0
            expand[ROPE_HALF + j, 2 * LANES + h0 + ROPE_HALF + j] = 1.0
    return jnp.asarray(np.tile(expand, (N_PIECES, 1)), BF16), jnp.asarray(base)


def _adaln(x, norm_gain, scale, shift):
    ms = jnp.mean(x * x, axis=-1, keepdims=True)
    return (x * lax.rsqrt(ms + EPS)) * (norm_gain * (1.0 + scale)) + shift


def _inproj_kernel(x_ref, nm_ref, sc_ref, sh_ref, w_ref, gsum_ref, gains_ref, cs_ref, exp_ref, base_ref,
                   o_ref):
    h = _adaln(x_ref[...], nm_ref[...], sc_ref[...], sh_ref[...]).astype(BF16)
    tab = base_ref[...] + _dot(cs_ref[...], exp_ref[...])
    rope_c, rope_n, rope_p = tab[:, :LANES], tab[:, LANES:2 * LANES], tab[:, 2 * LANES:]
    group_mean = gsum_ref[...]
    n_chunks = w_ref.shape[1] // MXU_DIM
    gain_row = {0: 0, 1: 1, 3: 2, 4: 3}

    def project(c):
        return _dot(h, w_ref[:, c * MXU_DIM:(c + 1) * MXU_DIM])

    def finish(c, y):
        kind = c // 2
        if kind not in gain_row:
            o_ref[:, c * MXU_DIM:(c + 1) * MXU_DIM] = y.astype(BF16)
            return
        inv_rms = lax.rsqrt(_dot((y * y).astype(BF16), group_mean) + EPS)
        r = gain_row[kind]
        y = y * gains_ref[r:r + 1, :]
        for half in range(MXU_DIM // LANES):
            lanes = slice(half * LANES, (half + 1) * LANES)
            yh = y[:, lanes]
            yh = (yh * rope_c + pltpu.roll(yh, LANES - ROPE_HALF, 1) * rope_n
                  + pltpu.roll(yh, ROPE_HALF, 1) * rope_p)
            lo = c * MXU_DIM + half * LANES
            o_ref[:, lo:lo + LANES] = (yh * inv_rms[:, lanes]).astype(BF16)

    upcoming = project(0)
    for c in range(n_chunks):
        y = upcoming
        if c + 1 < n_chunks:
            upcoming = project(c + 1)
        finish(c, y)


def _mod_spec(d_model, layer, k):
    return pl.BlockSpec((None, None, None, 1, d_model), lambda b, i: (layer, b, k, 0, 0))


def _layer_spec(array, layer):
    return pl.BlockSpec((None,) + array.shape[1:], lambda b, i: (layer, 0, 0))


def _const_spec(array):
    return pl.BlockSpec(array.shape, lambda b, i: (0,) * array.ndim)


def _inproj_call(x, mod5, layer, norm_gain, w_in, group_mean, gains, cs, rope_expand, rope_base):
    batch, seq, d_model = x.shape
    n_out = w_in.shape[2]
    return pl.pallas_call(
        _inproj_kernel,
        grid=(batch, seq // ROW_TILE),
        in_specs=[
            pl.BlockSpec((None, ROW_TILE, d_model), lambda b, i: (b, i, 0)),
            _layer_spec(norm_gain, layer),
            _mod_spec(d_model, layer, 1),
            _mod_spec(d_model, layer, 0),
            _layer_spec(w_in, layer),
            _const_spec(group_mean),
            _layer_spec(gains, layer),
            pl.BlockSpec((None, ROW_TILE, N_PIECES * ROPE_DIM), lambda b, i: (b, i, 0)),
            _const_spec(rope_expand),
            _const_spec(rope_base),
        ],
        out_specs=pl.BlockSpec((None, ROW_TILE, n_out), lambda b, i: (b, i, 0)),
        out_shape=jax.ShapeDtypeStruct((batch, seq, n_out), BF16),
        compiler_params=pltpu.CompilerParams(
            dimension_semantics=("parallel", "parallel"), vmem_limit_bytes=VMEM_LIMIT),
        name="in_projection",
    )(x, norm_gain, mod5, mod5, w_in, group_mean, gains, cs, rope_expand, rope_base)


MASKED = -1e30
ONES_ROWS = 16
PIPE_SLOTS = 2
GROUPS = 2


def _slot(i, half):
    return 2 * (i % PIPE_SLOTS) + half


def _slot_scratch(seq):
    return ([pltpu.VMEM((seq, ATT_TILE), F32) for _ in range(2 * PIPE_SLOTS)]
            + [pltpu.VMEM((seq, ATT_TILE), BF16) for _ in range(2 * PIPE_SLOTS)])


def _store_scores(s_ref, c, s_t):
    n_past = c * ATT_TILE
    own = s_t[n_past:, :]
    key = lax.broadcasted_iota(jnp.int32, own.shape, 0)
    query = lax.broadcasted_iota(jnp.int32, own.shape, 1)
    own = jnp.where(key <= query, own, -jnp.inf)
    s_ref[n_past:n_past + ATT_TILE, :] = own
    m = jnp.max(own, axis=0, keepdims=True)
    if c > 0:
        past = s_t[:n_past, :]
        s_ref[:n_past, :] = past
        m = jnp.maximum(m, jnp.max(past, axis=0, keepdims=True))
    return m


def _store_numerators(s_ref, p_ref, m, c):
    n_kv = (c + 1) * ATT_TILE
    p_ref[:n_kv, :] = jnp.exp2(s_ref[:n_kv, :] - m).astype(BF16)


def _weighted_values(p_ref, vt_ref, c, n_rows):
    n_kv = (c + 1) * ATT_TILE
    o_aug = _dot(vt_ref[:, :n_kv], p_ref[:n_kv, :])
    return o_aug[:n_rows] * (1.0 / o_aug[n_rows:n_rows + 1])


def _tile_order(n_tiles):
    return list(range(0, n_tiles, 2)) + list(range(1, n_tiles, 2))[::-1]


def _staged(groups, n_tiles, s_refs, p_refs):
    for group in groups:
        group["prepare"]()
    jobs = [(group, c) for group in groups for c in _tile_order(n_tiles)]
    job = lambda i: jobs[i] if 0 <= i < len(jobs) else None
    maxes, ahead = {}, {}
    for i in range(-2, len(jobs) + 1):
        if job(i + 2) is not None and "ahead" in job(i + 2)[0]:
            group, c = job(i + 2)
            for half in range(2):
                ahead[i + 2, half] = group["ahead"](c, half)
        for half in range(2):
            if job(i + 1) is not None:
                group, c = job(i + 1)
                maxes[i + 1, half] = group["scores"](c, half, ahead.pop((i + 1, half), None),
                                                     s_refs[_slot(i + 1, half)])
            if job(i) is not None:
                _store_numerators(s_refs[_slot(i, half)], p_refs[_slot(i, half)], maxes.pop((i, half)),
                                  job(i)[1])
        if job(i - 1) is not None:
            group, c = job(i - 1)
            group["output"](c, [p_refs[_slot(i - 1, half)] for half in range(2)])


def _lane_block(ref, g):
    return ref.at[:, g * LANES:(g + 1) * LANES]


def _tile_rows(c):
    return slice(c * ATT_TILE, (c + 1) * ATT_TILE)


def _transposed(ref):
    return ref[...].astype(F32).T


def _half_rows(qt_ref, c, half):
    q_t = qt_ref[:, _tile_rows(c)]
    row = lax.broadcasted_iota(jnp.int32, q_t.shape, 0)
    mine = (row < HEAD_DIM) if half == 0 else (row >= HEAD_DIM)
    return jnp.where(mine, q_t, jnp.zeros_like(q_t))


KMEAN_ROWS = 16
GATE_ROWS = 8


def _moba_prepare(q_ref, k_ref, v_ref, qt_ref, kaug_ref, kmean_ref, vt_ref):
    seq = k_ref.shape[0]
    n_blocks = seq // MOBA_BLOCK
    kaug_ref[:, :LANES] = k_ref[...]
    row_block = lax.broadcasted_iota(jnp.int32, (seq, LANES), 0) // MOBA_BLOCK
    lane = lax.broadcasted_iota(jnp.int32, (seq, LANES), 1)
    kaug_ref[:, LANES:] = jnp.where(row_block == lane, 1.0, 0.0).astype(BF16)
    means = [jnp.mean(k_ref[_tile_rows(j), :].astype(F32), axis=0, keepdims=True) for j in range(n_blocks)]
    means = jnp.concatenate(means + [jnp.zeros((KMEAN_ROWS - n_blocks, LANES), F32)], axis=0)
    hi = means.astype(BF16)
    lo = (means - hi.astype(F32)).astype(BF16)
    kmean_ref[...] = jnp.concatenate([hi, lo], axis=1)
    qt_ref[...] = _transposed(q_ref).astype(BF16)
    v_t = _transposed(v_ref)
    for head in range(LANES // HEAD_DIM):
        vt_ref[head, :HEAD_DIM, :] = v_t[head * HEAD_DIM:(head + 1) * HEAD_DIM].astype(BF16)
        vt_ref[head, HEAD_DIM:, :] = jnp.ones((ONES_ROWS, seq), BF16)


def _moba_block_bias(c, head, qt_ref, kmean_ref):
    if c <= MOBA_TOPK:
        return None
    qh_t = _half_rows(qt_ref, c, head)
    gate = _dot(kmean_ref[...], jnp.concatenate([qh_t, qh_t], axis=0))[:GATE_ROWS, :]
    blk = lax.broadcasted_iota(jnp.int32, gate.shape, 0)
    cnt = jnp.zeros(gate.shape, jnp.int32)
    for m in range(c):
        gm = gate[m:m + 1, :]
        cnt = cnt + jnp.where(blk > m, jnp.where(gm >= gate, 1, 0), jnp.where(gm > gate, 1, 0))
    bias_t = jnp.where(blk < c, jnp.where(cnt < MOBA_TOPK, 0.0, MASKED), 0.0)
    bias_t = jnp.concatenate([bias_t, jnp.zeros((LANES - GATE_ROWS, gate.shape[1]), F32)], axis=0)
    return bias_t.astype(BF16)


def _moba_scores(c, head, bias_t, s_ref, qt_ref, k_ref, kaug_ref):
    n_kv = (c + 1) * ATT_TILE
    qh_t = _half_rows(qt_ref, c, head)
    if bias_t is None:
        s_t = _dot(k_ref[:n_kv, :], qh_t)
    else:
        s_t = _dot(kaug_ref[:n_kv, :], jnp.concatenate([qh_t, bias_t], axis=0))
    return _store_scores(s_ref, c, s_t)


def _moba_output(c, p_refs, vt_ref, gn_ref, o_ref):
    outs = []
    for head in range(LANES // HEAD_DIM):
        o_h = _weighted_values(p_refs[head], vt_ref.at[head], c, HEAD_DIM)
        ms = jnp.mean(o_h * o_h, axis=0, keepdims=True)
        outs.append(o_h * lax.rsqrt(ms + EPS))
    o_t = jnp.concatenate(outs, axis=0)
    o_ref[_tile_rows(c), :] = (o_t.T * gn_ref[...]).astype(BF16)


def _moba_kernel(q_ref, k_ref, v_ref, gn_ref, o_ref, qt_ref, kaug_ref, kmean_ref, vt_ref, *slot_refs):
    s_refs, p_refs = slot_refs[:2 * PIPE_SLOTS], slot_refs[2 * PIPE_SLOTS:]
    groups = []
    for g in range(GROUPS):
        q_g, k_g, v_g, o_g = [_lane_block(ref, g) for ref in (q_ref, k_ref, v_ref, o_ref)]
        qt_g, kaug_g, kmean_g, vt_g = qt_ref.at[g], kaug_ref.at[g], kmean_ref.at[g], vt_ref.at[g]
        groups.append(dict(
            prepare=functools.partial(_moba_prepare, q_g, k_g, v_g, qt_g, kaug_g, kmean_g, vt_g),
            ahead=functools.partial(_moba_block_bias, qt_ref=qt_g, kmean_ref=kmean_g),
            scores=functools.partial(_moba_scores, qt_ref=qt_g, k_ref=k_g, kaug_ref=kaug_g),
            output=functools.partial(_moba_output, vt_ref=vt_g, gn_ref=gn_ref, o_ref=o_g)))
    _staged(groups, k_ref.shape[0] // ATT_TILE, s_refs, p_refs)


def _moba_call(proj, out_gain, layer, width):
    batch, seq, _ = proj.shape
    n_steps = width // (GROUPS * LANES)
    col_spec = lambda c0: pl.BlockSpec((None, seq, GROUPS * LANES), lambda b, p: (b, 0, c0 + p))
    return pl.pallas_call(
        _moba_kernel,
        grid=(batch, n_steps),
        in_specs=[col_spec(0), col_spec(n_steps), col_spec(2 * n_steps),
                  pl.BlockSpec((None, 1, LANES), lambda b, p: (layer, 0, 0))],
        out_specs=col_spec(0),
        out_shape=jax.ShapeDtypeStruct((batch, seq, width), BF16),
        scratch_shapes=[pltpu.VMEM((GROUPS, LANES, seq), BF16),
                        pltpu.VMEM((GROUPS, seq, 2 * LANES), BF16),
                        pltpu.VMEM((GROUPS, KMEAN_ROWS, 2 * LANES), BF16),
                        pltpu.VMEM((GROUPS, LANES // HEAD_DIM, HEAD_DIM + ONES_ROWS, seq), BF16),
                        *_slot_scratch(seq)],
        compiler_params=pltpu.CompilerParams(
            dimension_semantics=("parallel", "parallel"), vmem_limit_bytes=VMEM_LIMIT),
        name="moba_attention",
    )(proj, proj, proj, out_gain)


def _diff_prepare(q_ref, v_ref, qt_ref, vt_ref):
    qt_ref[...] = _transposed(q_ref).astype(BF16)
    vt_ref[:LANES, :] = _transposed(v_ref).astype(BF16)
    vt_ref[LANES:, :] = jnp.ones((ONES_ROWS, v_ref.shape[0]), BF16)


def _diff_scores(c, comp, _, s_ref, qt_ref, k_ref):
    return _store_scores(s_ref, c, _dot(k_ref[:(c + 1) * ATT_TILE, :], _half_rows(qt_ref, c, comp)))


def _diff_output(c, p_refs, out_scale, lam_ref, vt_ref, gn_ref, o_ref):
    a1, a2 = [_weighted_values(p_ref, vt_ref, c, LANES) for p_ref in p_refs]
    o_t = a1 - lam_ref[0:1, 0:1] * a2
    ms = jnp.mean(o_t * o_t, axis=0, keepdims=True)
    o_t = o_t * lax.rsqrt(ms + EPS)
    o_ref[_tile_rows(c), :] = ((o_t.T * gn_ref[...]) * out_scale).astype(BF16)


def _diff_kernel(out_scale, lam_ref, q_ref, k_ref, v_ref, gn_ref, o_ref, qt_ref, vt_ref, *slot_refs):
    s_refs, p_refs = slot_refs[:2 * PIPE_SLOTS], slot_refs[2 * PIPE_SLOTS:]
    groups = []
    for g in range(GROUPS):
        q_g, k_g, v_g, o_g = [_lane_block(ref, g) for ref in (q_ref, k_ref, v_ref, o_ref)]
        groups.append(dict(
            prepare=functools.partial(_diff_prepare, q_g, v_g, qt_ref.at[g], vt_ref.at[g]),
            scores=functools.partial(_diff_scores, qt_ref=qt_ref.at[g], k_ref=k_g),
            output=functools.partial(_diff_output, out_scale=out_scale, lam_ref=lam_ref, vt_ref=vt_ref.at[g],
                                     gn_ref=gn_ref, o_ref=o_g)))
    _staged(groups, k_ref.shape[0] // ATT_TILE, s_refs, p_refs)


def _diff_call(proj, lam, layer, out_gain, width, col0, out_scale):
    batch, seq, _ = proj.shape
    n_steps = width // (GROUPS * LANES)
    c0 = col0 // (GROUPS * LANES)
    col_spec = lambda c: pl.BlockSpec((None, seq, GROUPS * LANES), lambda b, h: (b, 0, c + h))
    return pl.pallas_call(
        functools.partial(_diff_kernel, out_scale),
        grid=(batch, n_steps),
        in_specs=[pl.BlockSpec((None, 1, LANES), lambda b, h: (layer, 0, 0)),
                  col_spec(c0), col_spec(c0 + n_steps), col_spec(c0 + 2 * n_steps),
                  pl.BlockSpec((None, 1, LANES), lambda b, h: (layer, 0, 0))],
        out_specs=col_spec(0),
        out_shape=jax.ShapeDtypeStruct((batch, seq, width), BF16),
        scratch_shapes=[pltpu.VMEM((GROUPS, LANES, seq), BF16),
                        pltpu.VMEM((GROUPS, LANES + ONES_ROWS, seq), BF16),
                        *_slot_scratch(seq)],
        compiler_params=pltpu.CompilerParams(
            dimension_semantics=("parallel", "parallel"), vmem_limit_bytes=VMEM_LIMIT),
        name="diff_attention",
    )(lam, proj, proj, proj, out_gain)


def _mlp_kernel(om_ref, od_ref, wo_ref, x_ref, ga_ref, nf_ref, sc_ref, sh_ref, gf_ref, wg_ref, wu_ref,
                wd_ref, o_ref, act_ref):
    width = om_ref.shape[1]
    y = _dot(om_ref[...], wo_ref[:width, :]) + _dot(od_ref[...], wo_ref[width:, :])
    x = x_ref[...] + ga_ref[...] * y
    h = _adaln(x, nf_ref[...], sc_ref[...], sh_ref[...]).astype(BF16)
    for c in range(wg_ref.shape[1] // MXU_DIM):
        cols = slice(c * MXU_DIM, (c + 1) * MXU_DIM)
        g = _dot(h, wg_ref[:, cols])
        u = _dot(h, wu_ref[:, cols])
        act_ref[:, cols] = ((g / (1.0 + jnp.exp(-g))) * u).astype(BF16)
    o_ref[...] = x + gf_ref[...] * _dot(act_ref[...], wd_ref[...])


def _mlp_call(o_m, o_d, x, mod5, layer, norm_gain, w_out, w_gate, w_up, w_down):
    batch, seq, d_model = x.shape
    d_ff = w_gate.shape[2]
    row_spec = lambda cols: pl.BlockSpec((None, ROW_TILE, cols), lambda b, i: (b, i, 0))
    resident = lambda w: pl.BlockSpec((None,) + w.shape[1:], lambda b, i: (layer, 0, 0),
                                      pipeline_mode=pl.Buffered(1))
    return pl.pallas_call(
        _mlp_kernel,
        grid=(batch, seq // ROW_TILE),
        in_specs=[
            row_spec(o_m.shape[2]), row_spec(o_d.shape[2]),
            resident(w_out),
            row_spec(d_model),
            _mod_spec(d_model, layer, 2),
            _layer_spec(norm_gain, layer),
            _mod_spec(d_model, layer, 4),
            _mod_spec(d_model, layer, 3),
            _mod_spec(d_model, layer, 5),
            resident(w_gate), resident(w_up), resident(w_down),
        ],
        out_specs=row_spec(d_model),
        out_shape=jax.ShapeDtypeStruct((batch, seq, d_model), F32),
        scratch_shapes=[pltpu.VMEM((ROW_TILE, d_ff), BF16)],
        compiler_params=pltpu.CompilerParams(
            dimension_semantics=("parallel", "parallel"), vmem_limit_bytes=VMEM_LIMIT),
        name="out_projection_swiglu",
    )(o_m, o_d, w_out, x, mod5, norm_gain, mod5, mod5, mod5, w_gate, w_up, w_down)


def kernel(x, c, positions, w_mod, b_mod, norm_mix, w_in, moba_q_norm, moba_k_norm, moba_out_norm,
           diff_q_norm, diff_k_norm, diff_lambda, diff_subln, w_out, norm_ffn, w_gate, w_up, w_down):
    batch, seq, d_model = x.shape
    depth = w_mod.shape[0]
    moba_width = d_model // 2
    diff_width = d_model // 2
    assert seq % ROW_TILE == 0 and seq % MOBA_BLOCK == 0 and ATT_TILE == MOBA_BLOCK
    assert seq // MOBA_BLOCK <= GATE_ROWS
    assert w_in.shape[2] == 3 * moba_width + 3 * diff_width and moba_width == 2 * MXU_DIM

    lam_inits = tuple(0.8 - 0.6 * math.exp(-0.3 * l) for l in range(depth))
    mod5 = _mod_call(c, w_mod, b_mod).reshape(depth, batch, N_MOD, 1, d_model)
    cs, lam = _trig_call(positions, diff_lambda, lam_inits)
    rope_expand, rope_base = _rope_expansion()

    head_of_lane = np.arange(MXU_DIM) // HEAD_DIM
    group_mean = jnp.asarray((head_of_lane[:, None] == head_of_lane[None, :]) / HEAD_DIM, BF16)
    reps = MXU_DIM // HEAD_DIM
    qk_scale = HEAD_DIM ** -0.5 * math.log2(math.e)
    gains = jnp.stack([jnp.tile(moba_q_norm, (1, reps)) * qk_scale, jnp.tile(moba_k_norm, (1, reps)),
                       jnp.tile(diff_q_norm, (1, reps)) * qk_scale, jnp.tile(diff_k_norm, (1, reps))],
                      axis=1)
    moba_out_gain = jnp.tile(moba_out_norm, (1, LANES // HEAD_DIM))[:, None, :]
    diff_out_gain = diff_subln[:, None, :]
    norm_mix3, norm_ffn3 = norm_mix[:, None, :], norm_ffn[:, None, :]

    w_in_b, w_out_b = w_in.astype(BF16), w_out.astype(BF16)
    w_gate_b, w_up_b, w_down_b = w_gate.astype(BF16), w_up.astype(BF16), w_down.astype(BF16)

    for l in range(depth):
        proj = _inproj_call(x, mod5, l, norm_mix3, w_in_b, group_mean, gains, cs, rope_expand, rope_base)
        o_m = _moba_call(proj, moba_out_gain, l, moba_width)
        o_d = _diff_call(proj, lam, l, diff_out_gain, diff_width, 3 * moba_width, 1.0 - lam_inits[l])
        x = _mlp_call(o_m, o_d, x, mod5, l, norm_ffn3, w_out_b, w_gate_b, w_up_b, w_down_b)
    return x
```

```python
import functools
import math

import numpy as np
import jax
import jax.numpy as jnp
from jax import lax
from jax.experimental import pallas as pl
from jax.experimental.pallas import tpu as pltpu

F32 = jnp.float32
BF16 = jnp.bfloat16
HIGHEST = lax.Precision.HIGHEST

LANES = 128
MXU_DIM = 256
VMEM_LIMIT = 56 * 1024 * 1024

HEAD_DIM = 64
ROPE_DIM = HEAD_DIM // 4
ROPE_HALF = ROPE_DIM // 2
ROPE_THETA = 500000.0
MOBA_BLOCK = 256
MOBA_TOPK = 3
N_MOD = 6
EPS = 1e-6

ROW_TILE = 512
MLP_ROW_TILE = 1024
ATT_TILE = 256
MOD_COL_TILE = 1536


def _dot(a, b):
    return jnp.dot(a, b, preferred_element_type=F32)


def _mod_kernel(c_ref, w_ref, b_ref, o_ref):
    c = c_ref[...]
    cond = c / (1.0 + jnp.exp(-c))
    o_ref[...] = jnp.dot(cond, w_ref[...], preferred_element_type=F32, precision=HIGHEST) + b_ref[...]


def _mod_call(c, w_mod, b_mod):
    depth, d_model, n_out = w_mod.shape
    batch = c.shape[0]
    return pl.pallas_call(
        _mod_kernel,
        grid=(depth, n_out // MOD_COL_TILE),
        in_specs=[
            pl.BlockSpec((batch, d_model), lambda l, j: (0, 0)),
            pl.BlockSpec((None, d_model, MOD_COL_TILE), lambda l, j: (l, 0, j)),
            pl.BlockSpec((None, 1, MOD_COL_TILE), lambda l, j: (l, 0, j)),
        ],
        out_specs=pl.BlockSpec((None, batch, MOD_COL_TILE), lambda l, j: (l, 0, j)),
        out_shape=jax.ShapeDtypeStruct((depth, batch, n_out), F32),
        compiler_params=pltpu.CompilerParams(
            dimension_semantics=("arbitrary", "arbitrary"), vmem_limit_bytes=VMEM_LIMIT),
        name="adaln_mod",
    )(c, w_mod, b_mod.reshape(depth, 1, n_out))


N_PIECES = 3


def _trig_kernel(lam_inits, pos_ref, inv_ref, dl_ref, cs_ref, lam_ref):
    ang = pos_ref[...].astype(F32) * inv_ref[...]
    lane = lax.broadcasted_iota(jnp.int32, ang.shape, 1)
    cs = jnp.where((lane & (ROPE_DIM - 1)) < ROPE_HALF, jnp.cos(ang), jnp.sin(ang))
    for i in range(N_PIECES):
        piece = cs.astype(BF16)
        cs_ref[i] = piece
        cs = cs - piece.astype(F32)
    for l, lam_init in enumerate(lam_inits):
        lp = dl_ref[l]
        a = jnp.sum(lp[0:1] * lp[1:2], axis=-1, keepdims=True)
        b = jnp.sum(lp[2:3] * lp[3:4], axis=-1, keepdims=True)
        lam_ref[l] = jnp.broadcast_to(jnp.exp(a) - jnp.exp(b) + lam_init, (1, LANES))


def _trig_call(positions, diff_lambda, lam_inits):
    batch, seq = positions.shape
    depth = diff_lambda.shape[0]
    inv = ROPE_THETA ** (-jnp.arange(0, ROPE_DIM, 2, dtype=F32) / ROPE_DIM)
    n_rows = batch * seq * ROPE_DIM // LANES
    pos_rep = jnp.broadcast_to(positions[..., None], (batch, seq, ROPE_DIM)).reshape(n_rows, LANES)
    inv_row = jnp.tile(jnp.concatenate([inv, inv]), LANES // ROPE_DIM)[None, :]
    cs, lam = pl.pallas_call(
        functools.partial(_trig_kernel, lam_inits),
        out_shape=(jax.ShapeDtypeStruct((N_PIECES, n_rows, LANES), BF16),
                   jax.ShapeDtypeStruct((depth, 1, LANES), F32)),
        name="rope_trig_lambda",
    )(pos_rep, inv_row, diff_lambda)
    cs = cs.reshape(N_PIECES, batch, seq, ROPE_DIM)
    return jnp.concatenate([cs[i] for i in range(N_PIECES)], axis=-1), lam


def _rope_expansion():
    expand = np.zeros((ROPE_DIM, 3 * LANES), np.float32)
    base = np.zeros((1, 3 * LANES), np.float32)
    base[0, :LANES] = 1.0
    for h0 in range(0, LANES, HEAD_DIM):
        base[0, h0:h0 + ROPE_DIM] = 0.0
        for j in range(ROPE_HALF):
            expand[j, h0 + j] = 1.0
            expand[j, h0 + ROPE_HALF + j] = 1.0
            expand[ROPE_HALF + j, LANES + h0 + j] = -1.0
            expand[ROPE_HALF + j, 2 * LANES + h0 + ROPE_HALF + j] = 1.0
    return jnp.asarray(np.tile(expand, (N_PIECES, 1)), BF16), jnp.asarray(base)


def _adaln(x, norm_gain, scale, shift):
    ms = jnp.mean(x * x, axis=-1, keepdims=True)
    return (x * lax.rsqrt(ms + EPS)) * (norm_gain * (1.0 + scale)) + shift


def _inproj_kernel(x_ref, nm_ref, sc_ref, sh_ref, w_ref, gsum_ref, gains_ref, cs_ref, exp_ref, base_ref,
                   o_ref):
    h = _adaln(x_ref[...], nm_ref[...], sc_ref[...], sh_ref[...]).astype(BF16)
    tab = base_ref[...] + _dot(cs_ref[...], exp_ref[...])
    rope_c, rope_n, rope_p = tab[:, :LANES], tab[:, LANES:2 * LANES], tab[:, 2 * LANES:]
    group_mean = gsum_ref[...]
    n_chunks = w_ref.shape[1] // MXU_DIM
    gain_row = {0: 0, 1: 1, 3: 2, 4: 3}

    def project(c):
        return _dot(h, w_ref[:, c * MXU_DIM:(c + 1) * MXU_DIM])

    def finish(c, y):
        kind = c // 2
        if kind not in gain_row:
            o_ref[:, c * MXU_DIM:(c + 1) * MXU_DIM] = y.astype(BF16)
            return
        inv_rms = lax.rsqrt(_dot((y * y).astype(BF16), group_mean) + EPS)
        r = gain_row[kind]
        y = y * gains_ref[r:r + 1, :]
        for half in range(MXU_DIM // LANES):
            lanes = slice(half * LANES, (half + 1) * LANES)
            yh = y[:, lanes]
            yh = (yh * rope_c + pltpu.roll(yh, LANES - ROPE_HALF, 1) * rope_n
                  + pltpu.roll(yh, ROPE_HALF, 1) * rope_p)
            lo = c * MXU_DIM + half * LANES
            o_ref[:, lo:lo + LANES] = (yh * inv_rms[:, lanes]).astype(BF16)

    upcoming = project(0)
    for c in range(n_chunks):
        y = upcoming
        if c + 1 < n_chunks:
            upcoming = project(c + 1)
        finish(c, y)


def _mod_spec(d_model, layer, k):
    return pl.BlockSpec((None, None, None, 1, d_model), lambda b, i: (layer, b, k, 0, 0))


def _layer_spec(array, layer):
    return pl.BlockSpec((None,) + array.shape[1:], lambda b, i: (layer, 0, 0))


def _const_spec(array):
    return pl.BlockSpec(array.shape, lambda b, i: (0,) * array.ndim)


def _inproj_call(x, mod5, layer, norm_gain, w_in, group_mean, gains, cs, rope_expand, rope_base):
    batch, seq, d_model = x.shape
    n_out = w_in.shape[2]
    return pl.pallas_call(
        _inproj_kernel,
        grid=(batch, seq // ROW_TILE),
        in_specs=[
            pl.BlockSpec((None, ROW_TILE, d_model), lambda b, i: (b, i, 0)),
            _layer_spec(norm_gain, layer),
            _mod_spec(d_model, layer, 1),
            _mod_spec(d_model, layer, 0),
            _layer_spec(w_in, layer),
            _const_spec(group_mean),
            _layer_spec(gains, layer),
            pl.BlockSpec((None, ROW_TILE, N_PIECES * ROPE_DIM), lambda b, i: (b, i, 0)),
            _const_spec(rope_expand),
            _const_spec(rope_base),
        ],
        out_specs=pl.BlockSpec((None, ROW_TILE, n_out), lambda b, i: (b, i, 0)),
        out_shape=jax.ShapeDtypeStruct((batch, seq, n_out), BF16),
        compiler_params=pltpu.CompilerParams(
            dimension_semantics=("parallel", "parallel"), vmem_limit_bytes=VMEM_LIMIT),
        name="in_projection",
    )(x, norm_gain, mod5, mod5, w_in, group_mean, gains, cs, rope_expand, rope_base)


MASKED = -1e30
ONES_ROWS = 16
PIPE_SLOTS = 2
GROUPS = 2


def _slot(i, half):
    return 2 * (i % PIPE_SLOTS) + half


def _slot_scratch(seq):
    return ([pltpu.VMEM((seq, ATT_TILE), F32) for _ in range(2 * PIPE_SLOTS)]
            + [pltpu.VMEM((seq, ATT_TILE), BF16) for _ in range(2 * PIPE_SLOTS)])


def _store_scores(s_ref, c, s_t):
    n_past = c * ATT_TILE
    own = s_t[n_past:, :]
    key = lax.broadcasted_iota(jnp.int32, own.shape, 0)
    query = lax.broadcasted_iota(jnp.int32, own.shape, 1)
    own = jnp.where(key <= query, own, -jnp.inf)
    s_ref[n_past:n_past + ATT_TILE, :] = own
    m = jnp.max(own, axis=0, keepdims=True)
    if c > 0:
        past = s_t[:n_past, :]
        s_ref[:n_past, :] = past
        m = jnp.maximum(m, jnp.max(past, axis=0, keepdims=True))
    return m


def _store_numerators(s_ref, p_ref, m, c):
    n_kv = (c + 1) * ATT_TILE
    p_ref[:n_kv, :] = jnp.exp2(s_ref[:n_kv, :] - m).astype(BF16)


def _weighted_values(p_ref, vt_ref, c, n_rows):
    n_kv = (c + 1) * ATT_TILE
    o_aug = _dot(vt_ref[:, :n_kv], p_ref[:n_kv, :])
    return o_aug[:n_rows] * (1.0 / o_aug[n_rows:n_rows + 1])


def _tile_order(n_tiles):
    return list(range(0, n_tiles, 2)) + list(range(1, n_tiles, 2))[::-1]


def _staged(groups, n_tiles, s_refs, p_refs):
    for group in groups:
        group["prepare"]()
    jobs = [(group, c) for group in groups for c in _tile_order(n_tiles)]
    job = lambda i: jobs[i] if 0 <= i < len(jobs) else None
    maxes, ahead = {}, {}
    for i in range(-2, len(jobs) + 1):
        if job(i + 2) is not None and "ahead" in job(i + 2)[0]:
            group, c = job(i + 2)
            for half in range(2):
                ahead[i + 2, half] = group["ahead"](c, half)
        for half in range(2):
            if job(i + 1) is not None:
                group, c = job(i + 1)
                maxes[i + 1, half] = group["scores"](c, half, ahead.pop((i + 1, half), None),
                                                     s_refs[_slot(i + 1, half)])
            if job(i) is not None:
                _store_numerators(s_refs[_slot(i, half)], p_refs[_slot(i, half)], maxes.pop((i, half)),
                                  job(i)[1])
        if job(i - 1) is not None:
            group, c = job(i - 1)
            group["output"](c, [p_refs[_slot(i - 1, half)] for half in range(2)])


def _lane_block(ref, g):
    return ref.at[:, g * LANES:(g + 1) * LANES]


def _tile_rows(c):
    return slice(c * ATT_TILE, (c + 1) * ATT_TILE)


def _transposed(ref):
    return ref[...].astype(F32).T


def _half_rows(qt_ref, c, half):
    q_t = qt_ref[:, _tile_rows(c)]
    row = lax.broadcasted_iota(jnp.int32, q_t.shape, 0)
    mine = (row < HEAD_DIM) if half == 0 else (row >= HEAD_DIM)
    return jnp.where(mine, q_t, jnp.zeros_like(q_t))


KMEAN_ROWS = 16
GATE_ROWS = 8


def _moba_prepare(q_ref, k_ref, v_ref, qt_ref, kaug_ref, kmean_ref, vt_ref):
    seq = k_ref.shape[0]
    n_blocks = seq // MOBA_BLOCK
    kaug_ref[:, :LANES] = k_ref[...]
    row_block = lax.broadcasted_iota(jnp.int32, (seq, LANES), 0) // MOBA_BLOCK
    lane = lax.broadcasted_iota(jnp.int32, (seq, LANES), 1)
    kaug_ref[:, LANES:] = jnp.where(row_block == lane, 1.0, 0.0).astype(BF16)
    means = [jnp.mean(k_ref[_tile_rows(j), :].astype(F32), axis=0, keepdims=True) for j in range(n_blocks)]
    means = jnp.concatenate(means + [jnp.zeros((KMEAN_ROWS - n_blocks, LANES), F32)], axis=0)
    hi = means.astype(BF16)
    lo = (means - hi.astype(F32)).astype(BF16)
    kmean_ref[...] = jnp.concatenate([hi, lo], axis=1)
    qt_ref[...] = _transposed(q_ref).astype(BF16)
    v_t = _transposed(v_ref)
    for head in range(LANES // HEAD_DIM):
        vt_ref[head, :HEAD_DIM, :] = v_t[head * HEAD_DIM:(head + 1) * HEAD_DIM].astype(BF16)
        vt_ref[head, HEAD_DIM:, :] = jnp.ones((ONES_ROWS, seq), BF16)


def _moba_block_bias(c, head, qt_ref, kmean_ref):
    if c <= MOBA_TOPK:
        return None
    qh_t = _half_rows(qt_ref, c, head)
    gate = _dot(kmean_ref[...], jnp.concatenate([qh_t, qh_t], axis=0))[:GATE_ROWS, :]
    blk = lax.broadcasted_iota(jnp.int32, gate.shape, 0)
    cnt = jnp.zeros(gate.shape, jnp.int32)
    for m in range(c):
        gm = gate[m:m + 1, :]
        cnt = cnt + jnp.where(blk > m, jnp.where(gm >= gate, 1, 0), jnp.where(gm > gate, 1, 0))
    bias_t = jnp.where(blk < c, jnp.where(cnt < MOBA_TOPK, 0.0, MASKED), 0.0)
    bias_t = jnp.concatenate([bias_t, jnp.zeros((LANES - GATE_ROWS, gate.shape[1]), F32)], axis=0)
    return bias_t.astype(BF16)


def _moba_scores(c, head, bias_t, s_ref, qt_ref, k_ref, kaug_ref):
    n_kv = (c + 1) * ATT_TILE
    qh_t = _half_rows(qt_ref, c, head)
    if bias_t is None:
        s_t = _dot(k_ref[:n_kv, :], qh_t)
    else:
        s_t = _dot(kaug_ref[:n_kv, :], jnp.concatenate([qh_t, bias_t], axis=0))
    return _store_scores(s_ref, c, s_t)


def _moba_output(c, p_refs, vt_ref, gn_ref, o_ref):
    outs = []
    for head in range(LANES // HEAD_DIM):
        o_h = _weighted_values(p_refs[head], vt_ref.at[head], c, HEAD_DIM)
        ms = jnp.mean(o_h * o_h, axis=0, keepdims=True)
        outs.append(o_h * lax.rsqrt(ms + EPS))
    o_t = jnp.concatenate(outs, axis=0)
    o_ref[_tile_rows(c), :] = (o_t.T * gn_ref[...]).astype(BF16)


def _moba_kernel(q_ref, k_ref, v_ref, gn_ref, o_ref, qt_ref, kaug_ref, kmean_ref, vt_ref, *slot_refs):
    s_refs, p_refs = slot_refs[:2 * PIPE_SLOTS], slot_refs[2 * PIPE_SLOTS:]
    groups = []
    for g in range(GROUPS):
        q_g, k_g, v_g, o_g = [_lane_block(ref, g) for ref in (q_ref, k_ref, v_ref, o_ref)]
        qt_g, kaug_g, kmean_g, vt_g = qt_ref.at[g], kaug_ref.at[g], kmean_ref.at[g], vt_ref.at[g]
        groups.append(dict(
            prepare=functools.partial(_moba_prepare, q_g, k_g, v_g, qt_g, kaug_g, kmean_g, vt_g),
            ahead=functools.partial(_moba_block_bias, qt_ref=qt_g, kmean_ref=kmean_g),
            scores=functools.partial(_moba_scores, qt_ref=qt_g, k_ref=k_g, kaug_ref=kaug_g),
            output=functools.partial(_moba_output, vt_ref=vt_g, gn_ref=gn_ref, o_ref=o_g)))
    _staged(groups, k_ref.shape[0] // ATT_TILE, s_refs, p_refs)


def _moba_call(proj, out_gain, layer, width):
    batch, seq, _ = proj.shape
    n_steps = width // (GROUPS * LANES)
    col_spec = lambda c0: pl.BlockSpec((None, seq, GROUPS * LANES), lambda b, p: (b, 0, c0 + p))
    return pl.pallas_call(
        _moba_kernel,
        grid=(batch, n_steps),
        in_specs=[col_spec(0), col_spec(n_steps), col_spec(2 * n_steps),
                  pl.BlockSpec((None, 1, LANES), lambda b, p: (layer, 0, 0))],
        out_specs=col_spec(0),
        out_shape=jax.ShapeDtypeStruct((batch, seq, width), BF16),
        scratch_shapes=[pltpu.VMEM((GROUPS, LANES, seq), BF16),
                        pltpu.VMEM((GROUPS, seq, 2 * LANES), BF16),
                        pltpu.VMEM((GROUPS, KMEAN_ROWS, 2 * LANES), BF16),
                        pltpu.VMEM((GROUPS, LANES // HEAD_DIM, HEAD_DIM + ONES_ROWS, seq), BF16),
                        *_slot_scratch(seq)],
        compiler_params=pltpu.CompilerParams(
            dimension_semantics=("parallel", "parallel"), vmem_limit_bytes=VMEM_LIMIT),
        name="moba_attention",
    )(proj, proj, proj, out_gain)


def _diff_prepare(q_ref, v_ref, qt_ref, vt_ref):
    qt_ref[...] = _transposed(q_ref).astype(BF16)
    vt_ref[:LANES, :] = _transposed(v_ref).astype(BF16)
    vt_ref[LANES:, :] = jnp.ones((ONES_ROWS, v_ref.shape[0]), BF16)


def _diff_scores(c, comp, _, s_ref, qt_ref, k_ref):
    return _store_scores(s_ref, c, _dot(k_ref[:(c + 1) * ATT_TILE, :], _half_rows(qt_ref, c, comp)))


def _diff_output(c, p_refs, out_scale, lam_ref, vt_ref, gn_ref, o_ref):
    a1, a2 = [_weighted_values(p_ref, vt_ref, c, LANES) for p_ref in p_refs]
    o_t = a1 - lam_ref[0:1, 0:1] * a2
    ms = jnp.mean(o_t * o_t, axis=0, keepdims=True)
    o_t = o_t * lax.rsqrt(ms + EPS)
    o_ref[_tile_rows(c), :] = ((o_t.T * gn_ref[...]) * out_scale).astype(BF16)


def _diff_kernel(out_scale, lam_ref, q_ref, k_ref, v_ref, gn_ref, o_ref, qt_ref, vt_ref, *slot_refs):
    s_refs, p_refs = slot_refs[:2 * PIPE_SLOTS], slot_refs[2 * PIPE_SLOTS:]
    groups = []
    for g in range(GROUPS):
        q_g, k_g, v_g, o_g = [_lane_block(ref, g) for ref in (q_ref, k_ref, v_ref, o_ref)]
        groups.append(dict(
            prepare=functools.partial(_diff_prepare, q_g, v_g, qt_ref.at[g], vt_ref.at[g]),
            scores=functools.partial(_diff_scores, qt_ref=qt_ref.at[g], k_ref=k_g),
            output=functools.partial(_diff_output, out_scale=out_scale, lam_ref=lam_ref, vt_ref=vt_ref.at[g],
                                     gn_ref=gn_ref, o_ref=o_g)))
    _staged(groups, k_ref.shape[0] // ATT_TILE, s_refs, p_refs)


def _diff_call(proj, lam, layer, out_gain, width, col0, out_scale):
    batch, seq, _ = proj.shape
    n_steps = width // (GROUPS * LANES)
    c0 = col0 // (GROUPS * LANES)
    col_spec = lambda c: pl.BlockSpec((None, seq, GROUPS * LANES), lambda b, h: (b, 0, c + h))
    return pl.pallas_call(
        functools.partial(_diff_kernel, out_scale),
        grid=(batch, n_steps),
        in_specs=[pl.BlockSpec((None, 1, LANES), lambda b, h: (layer, 0, 0)),
                  col_spec(c0), col_spec(c0 + n_steps), col_spec(c0 + 2 * n_steps),
                  pl.BlockSpec((None, 1, LANES), lambda b, h: (layer, 0, 0))],
        out_specs=col_spec(0),
        out_shape=jax.ShapeDtypeStruct((batch, seq, width), BF16),
        scratch_shapes=[pltpu.VMEM((GROUPS, LANES, seq), BF16),
                        pltpu.VMEM((GROUPS, LANES + ONES_ROWS, seq), BF16),
                        *_slot_scratch(seq)],
        compiler_params=pltpu.CompilerParams(
            dimension_semantics=("parallel", "parallel"), vmem_limit_bytes=VMEM_LIMIT),
        name="diff_attention",
    )(lam, proj, proj, proj, out_gain)


def _mlp_kernel(om_ref, od_ref, wo_ref, x_ref, ga_ref, nf_ref, sc_ref, sh_ref, gf_ref, wg_ref, wu_ref,
                wd_ref, o_ref, act_ref):
    width = om_ref.shape[1]
    y = _dot(om_ref[...], wo_ref[:width, :]) + _dot(od_ref[...], wo_ref[width:, :])
    x = x_ref[...] + ga_ref[...] * y
    h = _adaln(x, nf_ref[...], sc_ref[...], sh_ref[...]).astype(BF16)
    for c in range(wg_ref.shape[1] // MXU_DIM):
        cols = slice(c * MXU_DIM, (c + 1) * MXU_DIM)
        g = _dot(h, wg_ref[:, cols])
        u = _dot(h, wu_ref[:, cols])
        act_ref[:, cols] = ((g / (1.0 + jnp.exp(-g))) * u).astype(BF16)
    o_ref[...] = x + gf_ref[...] * _dot(act_ref[...], wd_ref[...])


def _mlp_call(o_m, o_d, x, mod5, layer, norm_gain, w_out, w_gate, w_up, w_down):
    batch, seq, d_model = x.shape
    d_ff = w_gate.shape[2]
    row_spec = lambda cols: pl.BlockSpec((None, MLP_ROW_TILE, cols), lambda b, i: (b, i, 0))
    resident = lambda w: pl.BlockSpec((None,) + w.shape[1:], lambda b, i: (layer, 0, 0),
                                      pipeline_mode=pl.Buffered(1))
    return pl.pallas_call(
        _mlp_kernel,
        grid=(batch, seq // MLP_ROW_TILE),
        in_specs=[
            row_spec(o_m.shape[2]), row_spec(o_d.shape[2]),
            resident(w_out),
            row_spec(d_model),
            _mod_spec(d_model, layer, 2),
            _layer_spec(norm_gain, layer),
            _mod_spec(d_model, layer, 4),
            _mod_spec(d_model, layer, 3),
            _mod_spec(d_model, layer, 5),
            resident(w_gate), resident(w_up), resident(w_down),
        ],
        out_specs=row_spec(d_model),
        out_shape=jax.ShapeDtypeStruct((batch, seq, d_model), F32),
        scratch_shapes=[pltpu.VMEM((MLP_ROW_TILE, d_ff), BF16)],
        compiler_params=pltpu.CompilerParams(
            dimension_semantics=("parallel", "parallel"), vmem_limit_bytes=VMEM_LIMIT),
        name="out_projection_swiglu",
    )(o_m, o_d, w_out, x, mod5, norm_gain, mod5, mod5, mod5, w_gate, w_up, w_down)


def kernel(x, c, positions, w_mod, b_mod, norm_mix, w_in, moba_q_norm, moba_k_norm, moba_out_norm,
           diff_q_norm, diff_k_norm, diff_lambda, diff_subln, w_out, norm_ffn, w_gate, w_up, w_down):
    batch, seq, d_model = x.shape
    depth = w_mod.shape[0]
    moba_width = d_model // 2
    diff_width = d_model // 2
    assert seq % ROW_TILE == 0 and seq % MLP_ROW_TILE == 0 and seq % MOBA_BLOCK == 0
    assert ATT_TILE == MOBA_BLOCK
    assert seq // MOBA_BLOCK <= GATE_ROWS
    assert w_in.shape[2] == 3 * moba_width + 3 * diff_width and moba_width == 2 * MXU_DIM

    lam_inits = tuple(0.8 - 0.6 * math.exp(-0.3 * l) for l in range(depth))
    mod5 = _mod_call(c, w_mod, b_mod).reshape(depth, batch, N_MOD, 1, d_model)
    cs, lam = _trig_call(positions, diff_lambda, lam_inits)
    rope_expand, rope_base = _rope_expansion()

    head_of_lane = np.arange(MXU_DIM) // HEAD_DIM
    group_mean = jnp.asarray((head_of_lane[:, None] == head_of_lane[None, :]) / HEAD_DIM, BF16)
    reps = MXU_DIM // HEAD_DIM
    qk_scale = HEAD_DIM ** -0.5 * math.log2(math.e)
    gains = jnp.stack([jnp.tile(moba_q_norm, (1, reps)) * qk_scale, jnp.tile(moba_k_norm, (1, reps)),
                       jnp.tile(diff_q_norm, (1, reps)) * qk_scale, jnp.tile(diff_k_norm, (1, reps))],
                      axis=1)
    moba_out_gain = jnp.tile(moba_out_norm, (1, LANES // HEAD_DIM))[:, None, :]
    diff_out_gain = diff_subln[:, None, :]
    norm_mix3, norm_ffn3 = norm_mix[:, None, :], norm_ffn[:, None, :]

    w_in_b, w_out_b = w_in.astype(BF16), w_out.astype(BF16)
    w_gate_b, w_up_b, w_down_b = w_gate.astype(BF16), w_up.astype(BF16), w_down.astype(BF16)

    for l in range(depth):
        proj = _inproj_call(x, mod5, l, norm_mix3, w_in_b, group_mean, gains, cs, rope_expand, rope_base)
        o_m = _moba_call(proj, moba_out_gain, l, moba_width)
        o_d = _diff_call(proj, lam, l, diff_out_gain, diff_width, 3 * moba_width, 1.0 - lam_inits[l])
        x = _mlp_call(o_m, o_d, x, mod5, l, norm_ffn3, w_out_b, w_gate_b, w_up_b, w_down_b)
    return x
```

```python
import functools
import math

import numpy as np
import jax
import jax.numpy as jnp
from jax import lax
from jax.experimental import pallas as pl
from jax.experimental.pallas import tpu as pltpu

F32 = jnp.float32
BF16 = jnp.bfloat16

LANES = 128
MXU_DIM = 256
VMEM_LIMIT = 56 * 1024 * 1024

HEAD_DIM = 64
ROPE_DIM = HEAD_DIM // 4
ROPE_HALF = ROPE_DIM // 2
ROPE_THETA = 500000.0
MOBA_BLOCK = 256
MOBA_TOPK = 3
N_MOD = 6
EPS = 1e-6

ROW_TILE = 512
MLP_ROW_TILE = 1024
ATT_TILE = 256
MOD_COL_TILE = 1536


def _dot(a, b):
    return jnp.dot(a, b, preferred_element_type=F32)


N_PIECES = 3


def _bf16_pieces(x, n):
    pieces = []
    for _ in range(n):
        piece = x.astype(BF16).astype(F32)
        pieces.append(piece)
        x = x - piece
    return pieces


def _mod_kernel(c_ref, w_ref, b_ref, o_ref):
    c = c_ref[...]
    batch = c.shape[0]
    cond = jnp.concatenate(_bf16_pieces(c / (1.0 + jnp.exp(-c)), N_PIECES), axis=0).astype(BF16)
    w_hi, w_lo = [piece.astype(BF16) for piece in _bf16_pieces(w_ref[...], 2)]
    y_hi = _dot(cond, w_hi)
    y_lo = _dot(cond[:2 * batch], w_lo)
    o_ref[...] = ((y_hi[2 * batch:] + y_lo[batch:]) + (y_hi[batch:2 * batch] + y_lo[:batch])
                  + y_hi[:batch] + b_ref[...])


def _mod_call(c, w_mod, b_mod):
    depth, d_model, n_out = w_mod.shape
    batch = c.shape[0]
    return pl.pallas_call(
        _mod_kernel,
        grid=(depth, n_out // MOD_COL_TILE),
        in_specs=[
            pl.BlockSpec((batch, d_model), lambda l, j: (0, 0)),
            pl.BlockSpec((None, d_model, MOD_COL_TILE), lambda l, j: (l, 0, j)),
            pl.BlockSpec((None, 1, MOD_COL_TILE), lambda l, j: (l, 0, j)),
        ],
        out_specs=pl.BlockSpec((None, batch, MOD_COL_TILE), lambda l, j: (l, 0, j)),
        out_shape=jax.ShapeDtypeStruct((depth, batch, n_out), F32),
        compiler_params=pltpu.CompilerParams(
            dimension_semantics=("arbitrary", "arbitrary"), vmem_limit_bytes=VMEM_LIMIT),
        name="adaln_mod",
    )(c, w_mod, b_mod.reshape(depth, 1, n_out))


def _trig_kernel(lam_inits, pos_ref, inv_ref, dl_ref, cs_ref, lam_ref):
    ang = pos_ref[...].astype(F32) * inv_ref[...]
    lane = lax.broadcasted_iota(jnp.int32, ang.shape, 1)
    cs = jnp.where((lane & (ROPE_DIM - 1)) < ROPE_HALF, jnp.cos(ang), jnp.sin(ang))
    for i, piece in enumerate(_bf16_pieces(cs, N_PIECES)):
        cs_ref[i] = piece.astype(BF16)
    for l, lam_init in enumerate(lam_inits):
        lp = dl_ref[l]
        a = jnp.sum(lp[0:1] * lp[1:2], axis=-1, keepdims=True)
        b = jnp.sum(lp[2:3] * lp[3:4], axis=-1, keepdims=True)
        lam_ref[l] = jnp.broadcast_to(jnp.exp(a) - jnp.exp(b) + lam_init, (1, LANES))


def _trig_call(positions, diff_lambda, lam_inits):
    batch, seq = positions.shape
    depth = diff_lambda.shape[0]
    inv = ROPE_THETA ** (-jnp.arange(0, ROPE_DIM, 2, dtype=F32) / ROPE_DIM)
    n_rows = batch * seq * ROPE_DIM // LANES
    pos_rep = jnp.broadcast_to(positions[..., None], (batch, seq, ROPE_DIM)).reshape(n_rows, LANES)
    inv_row = jnp.tile(jnp.concatenate([inv, inv]), LANES // ROPE_DIM)[None, :]
    cs, lam = pl.pallas_call(
        functools.partial(_trig_kernel, lam_inits),
        out_shape=(jax.ShapeDtypeStruct((N_PIECES, n_rows, LANES), BF16),
                   jax.ShapeDtypeStruct((depth, 1, LANES), F32)),
        name="rope_trig_lambda",
    )(pos_rep, inv_row, diff_lambda)
    cs = cs.reshape(N_PIECES, batch, seq, ROPE_DIM)
    return jnp.concatenate([cs[i] for i in range(N_PIECES)], axis=-1), lam


def _rope_expansion():
    expand = np.zeros((ROPE_DIM, 3 * LANES), np.float32)
    base = np.zeros((1, 3 * LANES), np.float32)
    base[0, :LANES] = 1.0
    for h0 in range(0, LANES, HEAD_DIM):
        base[0, h0:h0 + ROPE_DIM] = 0.0
        for j in range(ROPE_HALF):
            expand[j, h0 + j] = 1.0
            expand[j, h0 + ROPE_HALF + j] = 1.0
            expand[ROPE_HALF + j, LANES + h0 + j] = -1.0
            expand[ROPE_HALF + j, 2 * LANES + h0 + ROPE_HALF + j] = 1.0
    return jnp.asarray(np.tile(expand, (N_PIECES, 1)), BF16), jnp.asarray(base)


def _adaln(x, norm_gain, scale, shift):
    ms = jnp.mean(x * x, axis=-1, keepdims=True)
    return (x * lax.rsqrt(ms + EPS)) * (norm_gain * (1.0 + scale)) + shift


def _inproj_kernel(x_ref, nm_ref, sc_ref, sh_ref, w_ref, gsum_ref, gains_ref, cs_ref, exp_ref, base_ref,
                   o_ref):
    h = _adaln(x_ref[...], nm_ref[...], sc_ref[...], sh_ref[...]).astype(BF16)
    tab = base_ref[...] + _dot(cs_ref[...], exp_ref[...])
    rope_c, rope_n, rope_p = tab[:, :LANES], tab[:, LANES:2 * LANES], tab[:, 2 * LANES:]
    group_mean = gsum_ref[...]
    n_chunks = w_ref.shape[1] // MXU_DIM
    gain_row = {0: 0, 1: 1, 3: 2, 4: 3}

    def project(c):
        return _dot(h, w_ref[:, c * MXU_DIM:(c + 1) * MXU_DIM])

    def finish(c, y):
        kind = c // 2
        if kind not in gain_row:
            o_ref[:, c * MXU_DIM:(c + 1) * MXU_DIM] = y.astype(BF16)
            return
        inv_rms = lax.rsqrt(_dot((y * y).astype(BF16), group_mean) + EPS)
        r = gain_row[kind]
        y = y * gains_ref[r:r + 1, :]
        for half in range(MXU_DIM // LANES):
            lanes = slice(half * LANES, (half + 1) * LANES)
            yh = y[:, lanes]
            yh = (yh * rope_c + pltpu.roll(yh, LANES - ROPE_HALF, 1) * rope_n
                  + pltpu.roll(yh, ROPE_HALF, 1) * rope_p)
            lo = c * MXU_DIM + half * LANES
            o_ref[:, lo:lo + LANES] = (yh * inv_rms[:, lanes]).astype(BF16)

    upcoming = project(0)
    for c in range(n_chunks):
        y = upcoming
        if c + 1 < n_chunks:
            upcoming = project(c + 1)
        finish(c, y)


def _mod_spec(d_model, layer, k):
    return pl.BlockSpec((None, None, None, 1, d_model), lambda b, i: (layer, b, k, 0, 0))


def _layer_spec(array, layer):
    return pl.BlockSpec((None,) + array.shape[1:], lambda b, i: (layer, 0, 0))


def _const_spec(array):
    return pl.BlockSpec(array.shape, lambda b, i: (0,) * array.ndim)


def _inproj_call(x, mod5, layer, norm_gain, w_in, group_mean, gains, cs, rope_expand, rope_base):
    batch, seq, d_model = x.shape
    n_out = w_in.shape[2]
    return pl.pallas_call(
        _inproj_kernel,
        grid=(batch, seq // ROW_TILE),
        in_specs=[
            pl.BlockSpec((None, ROW_TILE, d_model), lambda b, i: (b, i, 0)),
            _layer_spec(norm_gain, layer),
            _mod_spec(d_model, layer, 1),
            _mod_spec(d_model, layer, 0),
            _layer_spec(w_in, layer),
            _const_spec(group_mean),
            _layer_spec(gains, layer),
            pl.BlockSpec((None, ROW_TILE, N_PIECES * ROPE_DIM), lambda b, i: (b, i, 0)),
            _const_spec(rope_expand),
            _const_spec(rope_base),
        ],
        out_specs=pl.BlockSpec((None, ROW_TILE, n_out), lambda b, i: (b, i, 0)),
        out_shape=jax.ShapeDtypeStruct((batch, seq, n_out), BF16),
        compiler_params=pltpu.CompilerParams(
            dimension_semantics=("parallel", "parallel"), vmem_limit_bytes=VMEM_LIMIT),
        name="in_projection",
    )(x, norm_gain, mod5, mod5, w_in, group_mean, gains, cs, rope_expand, rope_base)


MASKED = -1e30
ONES_ROWS = 16
PIPE_SLOTS = 2
GROUPS = 2


def _slot(i, half):
    return 2 * (i % PIPE_SLOTS) + half


def _slot_scratch(seq):
    return ([pltpu.VMEM((seq, ATT_TILE), F32) for _ in range(2 * PIPE_SLOTS)]
            + [pltpu.VMEM((seq, ATT_TILE), BF16) for _ in range(2 * PIPE_SLOTS)])


def _store_scores(s_ref, c, s_t):
    n_past = c * ATT_TILE
    own = s_t[n_past:, :]
    key = lax.broadcasted_iota(jnp.int32, own.shape, 0)
    query = lax.broadcasted_iota(jnp.int32, own.shape, 1)
    own = jnp.where(key <= query, own, -jnp.inf)
    s_ref[n_past:n_past + ATT_TILE, :] = own
    m = jnp.max(own, axis=0, keepdims=True)
    if c > 0:
        past = s_t[:n_past, :]
        s_ref[:n_past, :] = past
        m = jnp.maximum(m, jnp.max(past, axis=0, keepdims=True))
    return m


def _store_numerators(s_ref, p_ref, m, c):
    n_kv = (c + 1) * ATT_TILE
    p_ref[:n_kv, :] = jnp.exp2(s_ref[:n_kv, :] - m).astype(BF16)


def _weighted_values(p_ref, vt_ref, c, n_rows):
    n_kv = (c + 1) * ATT_TILE
    o_aug = _dot(vt_ref[:, :n_kv], p_ref[:n_kv, :])
    return o_aug[:n_rows] * (1.0 / o_aug[n_rows:n_rows + 1])


def _tile_order(n_tiles):
    return list(range(0, n_tiles, 2)) + list(range(1, n_tiles, 2))[::-1]


def _staged(groups, n_tiles, s_refs, p_refs):
    for group in groups:
        group["prepare"]()
    jobs = [(group, c) for group in groups for c in _tile_order(n_tiles)]
    job = lambda i: jobs[i] if 0 <= i < len(jobs) else None
    maxes, ahead = {}, {}
    for i in range(-2, len(jobs) + 1):
        if job(i + 2) is not None and "ahead" in job(i + 2)[0]:
            group, c = job(i + 2)
            for half in range(2):
                ahead[i + 2, half] = group["ahead"](c, half)
        for half in range(2):
            if job(i + 1) is not None:
                group, c = job(i + 1)
                maxes[i + 1, half] = group["scores"](c, half, ahead.pop((i + 1, half), None),
                                                     s_refs[_slot(i + 1, half)])
            if job(i) is not None:
                _store_numerators(s_refs[_slot(i, half)], p_refs[_slot(i, half)], maxes.pop((i, half)),
                                  job(i)[1])
        if job(i - 1) is not None:
            group, c = job(i - 1)
            group["output"](c, [p_refs[_slot(i - 1, half)] for half in range(2)])


def _lane_block(ref, g):
    return ref.at[:, g * LANES:(g + 1) * LANES]


def _tile_rows(c):
    return slice(c * ATT_TILE, (c + 1) * ATT_TILE)


def _transposed(ref):
    return ref[...].astype(F32).T


def _half_rows(qt_ref, c, half):
    q_t = qt_ref[:, _tile_rows(c)]
    row = lax.broadcasted_iota(jnp.int32, q_t.shape, 0)
    mine = (row < HEAD_DIM) if half == 0 else (row >= HEAD_DIM)
    return jnp.where(mine, q_t, jnp.zeros_like(q_t))


KMEAN_ROWS = 16
GATE_ROWS = 8


def _moba_prepare(q_ref, k_ref, v_ref, qt_ref, kaug_ref, kmean_ref, vt_ref):
    seq = k_ref.shape[0]
    n_blocks = seq // MOBA_BLOCK
    kaug_ref[:, :LANES] = k_ref[...]
    row_block = lax.broadcasted_iota(jnp.int32, (seq, LANES), 0) // MOBA_BLOCK
    lane = lax.broadcasted_iota(jnp.int32, (seq, LANES), 1)
    kaug_ref[:, LANES:] = jnp.where(row_block == lane, 1.0, 0.0).astype(BF16)
    means = [jnp.mean(k_ref[_tile_rows(j), :].astype(F32), axis=0, keepdims=True) for j in range(n_blocks)]
    means = jnp.concatenate(means + [jnp.zeros((KMEAN_ROWS - n_blocks, LANES), F32)], axis=0)
    hi = means.astype(BF16)
    lo = (means - hi.astype(F32)).astype(BF16)
    kmean_ref[...] = jnp.concatenate([hi, lo], axis=1)
    qt_ref[...] = _transposed(q_ref).astype(BF16)
    v_t = _transposed(v_ref)
    for head in range(LANES // HEAD_DIM):
        vt_ref[head, :HEAD_DIM, :] = v_t[head * HEAD_DIM:(head + 1) * HEAD_DIM].astype(BF16)
        vt_ref[head, HEAD_DIM:, :] = jnp.ones((ONES_ROWS, seq), BF16)


def _moba_block_bias(c, head, qt_ref, kmean_ref):
    if c <= MOBA_TOPK:
        return None
    qh_t = _half_rows(qt_ref, c, head)
    gate = _dot(kmean_ref[...], jnp.concatenate([qh_t, qh_t], axis=0))[:GATE_ROWS, :]
    blk = lax.broadcasted_iota(jnp.int32, gate.shape, 0)
    cnt = jnp.zeros(gate.shape, jnp.int32)
    for m in range(c):
        gm = gate[m:m + 1, :]
        cnt = cnt + jnp.where(blk > m, jnp.where(gm >= gate, 1, 0), jnp.where(gm > gate, 1, 0))
    bias_t = jnp.where(blk < c, jnp.where(cnt < MOBA_TOPK, 0.0, MASKED), 0.0)
    bias_t = jnp.concatenate([bias_t, jnp.zeros((LANES - GATE_ROWS, gate.shape[1]), F32)], axis=0)
    return bias_t.astype(BF16)


def _moba_scores(c, head, bias_t, s_ref, qt_ref, k_ref, kaug_ref):
    n_kv = (c + 1) * ATT_TILE
    qh_t = _half_rows(qt_ref, c, head)
    if bias_t is None:
        s_t = _dot(k_ref[:n_kv, :], qh_t)
    else:
        s_t = _dot(kaug_ref[:n_kv, :], jnp.concatenate([qh_t, bias_t], axis=0))
    return _store_scores(s_ref, c, s_t)


def _moba_output(c, p_refs, vt_ref, gn_ref, o_ref):
    outs = []
    for head in range(LANES // HEAD_DIM):
        o_h = _weighted_values(p_refs[head], vt_ref.at[head], c, HEAD_DIM)
        ms = jnp.mean(o_h * o_h, axis=0, keepdims=True)
        outs.append(o_h * lax.rsqrt(ms + EPS))
    o_t = jnp.concatenate(outs, axis=0)
    o_ref[_tile_rows(c), :] = (o_t.T * gn_ref[...]).astype(BF16)


def _moba_kernel(q_ref, k_ref, v_ref, gn_ref, o_ref, qt_ref, kaug_ref, kmean_ref, vt_ref, *slot_refs):
    s_refs, p_refs = slot_refs[:2 * PIPE_SLOTS], slot_refs[2 * PIPE_SLOTS:]
    groups = []
    for g in range(GROUPS):
        q_g, k_g, v_g, o_g = [_lane_block(ref, g) for ref in (q_ref, k_ref, v_ref, o_ref)]
        qt_g, kaug_g, kmean_g, vt_g = qt_ref.at[g], kaug_ref.at[g], kmean_ref.at[g], vt_ref.at[g]
        groups.append(dict(
            prepare=functools.partial(_moba_prepare, q_g, k_g, v_g, qt_g, kaug_g, kmean_g, vt_g),
            ahead=functools.partial(_moba_block_bias, qt_ref=qt_g, kmean_ref=kmean_g),
            scores=functools.partial(_moba_scores, qt_ref=qt_g, k_ref=k_g, kaug_ref=kaug_g),
            output=functools.partial(_moba_output, vt_ref=vt_g, gn_ref=gn_ref, o_ref=o_g)))
    _staged(groups, k_ref.shape[0] // ATT_TILE, s_refs, p_refs)


def _moba_call(proj, out_gain, layer, width):
    batch, seq, _ = proj.shape
    n_steps = width // (GROUPS * LANES)
    col_spec = lambda c0: pl.BlockSpec((None, seq, GROUPS * LANES), lambda b, p: (b, 0, c0 + p))
    return pl.pallas_call(
        _moba_kernel,
        grid=(batch, n_steps),
        in_specs=[col_spec(0), col_spec(n_steps), col_spec(2 * n_steps),
                  pl.BlockSpec((None, 1, LANES), lambda b, p: (layer, 0, 0))],
        out_specs=col_spec(0),
        out_shape=jax.ShapeDtypeStruct((batch, seq, width), BF16),
        scratch_shapes=[pltpu.VMEM((GROUPS, LANES, seq), BF16),
                        pltpu.VMEM((GROUPS, seq, 2 * LANES), BF16),
                        pltpu.VMEM((GROUPS, KMEAN_ROWS, 2 * LANES), BF16),
                        pltpu.VMEM((GROUPS, LANES // HEAD_DIM, HEAD_DIM + ONES_ROWS, seq), BF16),
                        *_slot_scratch(seq)],
        compiler_params=pltpu.CompilerParams(
            dimension_semantics=("parallel", "parallel"), vmem_limit_bytes=VMEM_LIMIT),
        name="moba_attention",
    )(proj, proj, proj, out_gain)


def _diff_prepare(q_ref, v_ref, qt_ref, vt_ref):
    qt_ref[...] = _transposed(q_ref).astype(BF16)
    vt_ref[:LANES, :] = _transposed(v_ref).astype(BF16)
    vt_ref[LANES:, :] = jnp.ones((ONES_ROWS, v_ref.shape[0]), BF16)


def _diff_scores(c, comp, _, s_ref, qt_ref, k_ref):
    return _store_scores(s_ref, c, _dot(k_ref[:(c + 1) * ATT_TILE, :], _half_rows(qt_ref, c, comp)))


def _diff_output(c, p_refs, out_scale, lam_ref, vt_ref, gn_ref, o_ref):
    a1, a2 = [_weighted_values(p_ref, vt_ref, c, LANES) for p_ref in p_refs]
    o_t = a1 - lam_ref[0:1, 0:1] * a2
    ms = jnp.mean(o_t * o_t, axis=0, keepdims=True)
    o_t = o_t * lax.rsqrt(ms + EPS)
    o_ref[_tile_rows(c), :] = ((o_t.T * gn_ref[...]) * out_scale).astype(BF16)


def _diff_kernel(out_scale, lam_ref, q_ref, k_ref, v_ref, gn_ref, o_ref, qt_ref, vt_ref, *slot_refs):
    s_refs, p_refs = slot_refs[:2 * PIPE_SLOTS], slot_refs[2 * PIPE_SLOTS:]
    groups = []
    for g in range(GROUPS):
        q_g, k_g, v_g, o_g = [_lane_block(ref, g) for ref in (q_ref, k_ref, v_ref, o_ref)]
        groups.append(dict(
            prepare=functools.partial(_diff_prepare, q_g, v_g, qt_ref.at[g], vt_ref.at[g]),
            scores=functools.partial(_diff_scores, qt_ref=qt_ref.at[g], k_ref=k_g),
            output=functools.partial(_diff_output, out_scale=out_scale, lam_ref=lam_ref, vt_ref=vt_ref.at[g],
                                     gn_ref=gn_ref, o_ref=o_g)))
    _staged(groups, k_ref.shape[0] // ATT_TILE, s_refs, p_refs)


def _diff_call(proj, lam, layer, out_gain, width, col0, out_scale):
    batch, seq, _ = proj.shape
    n_steps = width // (GROUPS * LANES)
    c0 = col0 // (GROUPS * LANES)
    col_spec = lambda c: pl.BlockSpec((None, seq, GROUPS * LANES), lambda b, h: (b, 0, c + h))
    return pl.pallas_call(
        functools.partial(_diff_kernel, out_scale),
        grid=(batch, n_steps),
        in_specs=[pl.BlockSpec((None, 1, LANES), lambda b, h: (layer, 0, 0)),
                  col_spec(c0), col_spec(c0 + n_steps), col_spec(c0 + 2 * n_steps),
                  pl.BlockSpec((None, 1, LANES), lambda b, h: (layer, 0, 0))],
        out_specs=col_spec(0),
        out_shape=jax.ShapeDtypeStruct((batch, seq, width), BF16),
        scratch_shapes=[pltpu.VMEM((GROUPS, LANES, seq), BF16),
                        pltpu.VMEM((GROUPS, LANES + ONES_ROWS, seq), BF16),
                        *_slot_scratch(seq)],
        compiler_params=pltpu.CompilerParams(
            dimension_semantics=("parallel", "parallel"), vmem_limit_bytes=VMEM_LIMIT),
        name="diff_attention",
    )(lam, proj, proj, proj, out_gain)


def _mlp_kernel(om_ref, od_ref, wo_ref, x_ref, ga_ref, nf_ref, sc_ref, sh_ref, gf_ref, wg_ref, wu_ref,
                wd_ref, o_ref, act_ref):
    width = om_ref.shape[1]
    y = _dot(om_ref[...], wo_ref[:width, :]) + _dot(od_ref[...], wo_ref[width:, :])
    x = x_ref[...] + ga_ref[...] * y
    h = _adaln(x, nf_ref[...], sc_ref[...], sh_ref[...]).astype(BF16)
    for c in range(wg_ref.shape[1] // MXU_DIM):
        cols = slice(c * MXU_DIM, (c + 1) * MXU_DIM)
        g = _dot(h, wg_ref[:, cols])
        u = _dot(h, wu_ref[:, cols])
        act_ref[:, cols] = ((g / (1.0 + jnp.exp(-g))) * u).astype(BF16)
    o_ref[...] = x + gf_ref[...] * _dot(act_ref[...], wd_ref[...])


def _mlp_call(o_m, o_d, x, mod5, layer, norm_gain, w_out, w_gate, w_up, w_down):
    batch, seq, d_model = x.shape
    d_ff = w_gate.shape[2]
    row_spec = lambda cols: pl.BlockSpec((None, MLP_ROW_TILE, cols), lambda b, i: (b, i, 0))
    resident = lambda w: pl.BlockSpec((None,) + w.shape[1:], lambda b, i: (layer, 0, 0),
                                      pipeline_mode=pl.Buffered(1))
    return pl.pallas_call(
        _mlp_kernel,
        grid=(batch, seq // MLP_ROW_TILE),
        in_specs=[
            row_spec(o_m.shape[2]), row_spec(o_d.shape[2]),
            resident(w_out),
            row_spec(d_model),
            _mod_spec(d_model, layer, 2),
            _layer_spec(norm_gain, layer),
            _mod_spec(d_model, layer, 4),
            _mod_spec(d_model, layer, 3),
            _mod_spec(d_model, layer, 5),
            resident(w_gate), resident(w_up), resident(w_down),
        ],
        out_specs=row_spec(d_model),
        out_shape=jax.ShapeDtypeStruct((batch, seq, d_model), F32),
        scratch_shapes=[pltpu.VMEM((MLP_ROW_TILE, d_ff), BF16)],
        compiler_params=pltpu.CompilerParams(
            dimension_semantics=("parallel", "parallel"), vmem_limit_bytes=VMEM_LIMIT),
        name="out_projection_swiglu",
    )(o_m, o_d, w_out, x, mod5, norm_gain, mod5, mod5, mod5, w_gate, w_up, w_down)


def kernel(x, c, positions, w_mod, b_mod, norm_mix, w_in, moba_q_norm, moba_k_norm, moba_out_norm,
           diff_q_norm, diff_k_norm, diff_lambda, diff_subln, w_out, norm_ffn, w_gate, w_up, w_down):
    batch, seq, d_model = x.shape
    depth = w_mod.shape[0]
    moba_width = d_model // 2
    diff_width = d_model // 2
    assert seq % ROW_TILE == 0 and seq % MLP_ROW_TILE == 0 and seq % MOBA_BLOCK == 0
    assert ATT_TILE == MOBA_BLOCK
    assert seq // MOBA_BLOCK <= GATE_ROWS
    assert w_in.shape[2] == 3 * moba_width + 3 * diff_width and moba_width == 2 * MXU_DIM

    lam_inits = tuple(0.8 - 0.6 * math.exp(-0.3 * l) for l in range(depth))
    mod5 = _mod_call(c, w_mod, b_mod).reshape(depth, batch, N_MOD, 1, d_model)
    cs, lam = _trig_call(positions, diff_lambda, lam_inits)
    rope_expand, rope_base = _rope_expansion()

    head_of_lane = np.arange(MXU_DIM) // HEAD_DIM
    group_mean = jnp.asarray((head_of_lane[:, None] == head_of_lane[None, :]) / HEAD_DIM, BF16)
    reps = MXU_DIM // HEAD_DIM
    qk_scale = HEAD_DIM ** -0.5 * math.log2(math.e)
    gains = jnp.stack([jnp.tile(moba_q_norm, (1, reps)) * qk_scale, jnp.tile(moba_k_norm, (1, reps)),
                       jnp.tile(diff_q_norm, (1, reps)) * qk_scale, jnp.tile(diff_k_norm, (1, reps))],
                      axis=1)
    moba_out_gain = jnp.tile(moba_out_norm, (1, LANES // HEAD_DIM))[:, None, :]
    diff_out_gain = diff_subln[:, None, :]
    norm_mix3, norm_ffn3 = norm_mix[:, None, :], norm_ffn[:, None, :]

    w_in_b, w_out_b = w_in.astype(BF16), w_out.astype(BF16)
    w_gate_b, w_up_b, w_down_b = w_gate.astype(BF16), w_up.astype(BF16), w_down.astype(BF16)

    for l in range(depth):
        proj = _inproj_call(x, mod5, l, norm_mix3, w_in_b, group_mean, gains, cs, rope_expand, rope_base)
        o_m = _moba_call(proj, moba_out_gain, l, moba_width)
        o_d = _diff_call(proj, lam, l, diff_out_gain, diff_width, 3 * moba_width, 1.0 - lam_inits[l])
        x = _mlp_call(o_m, o_d, x, mod5, l, norm_ffn3, w_out_b, w_gate_b, w_up_b, w_down_b)
    return x
```

```python
import functools
import math

import numpy as np
import jax
import jax.numpy as jnp
from jax import lax
from jax.experimental import pallas as pl
from jax.experimental.pallas import tpu as pltpu

F32 = jnp.float32
BF16 = jnp.bfloat16

LANES = 128
MXU_DIM = 256
VMEM_LIMIT = 56 * 1024 * 1024

HEAD_DIM = 64
ROPE_DIM = HEAD_DIM // 4
ROPE_HALF = ROPE_DIM // 2
ROPE_THETA = 500000.0
MOBA_BLOCK = 256
MOBA_TOPK = 3
N_MOD = 6
EPS = 1e-6

ROW_TILE = 512
MLP_ROW_TILE = 1024
ATT_TILE = 256
MOD_COL_TILE = 1536


def _dot(a, b):
    return jnp.dot(a, b, preferred_element_type=F32)


N_PIECES = 3


def _bf16_pieces(x, n):
    pieces = []
    for _ in range(n):
        piece = x.astype(BF16).astype(F32)
        pieces.append(piece)
        x = x - piece
    return pieces


def _mod_kernel(c_ref, w_ref, b_ref, o_ref):
    c = c_ref[...]
    batch = c.shape[0]
    cond = jnp.concatenate(_bf16_pieces(c / (1.0 + jnp.exp(-c)), N_PIECES), axis=0).astype(BF16)
    w_hi, w_lo = [piece.astype(BF16) for piece in _bf16_pieces(w_ref[...], 2)]
    y_hi = _dot(cond, w_hi)
    y_lo = _dot(cond[:2 * batch], w_lo)
    o_ref[...] = ((y_hi[2 * batch:] + y_lo[batch:]) + (y_hi[batch:2 * batch] + y_lo[:batch])
                  + y_hi[:batch] + b_ref[...])


def _mod_call(c, w_mod, b_mod):
    depth, d_model, n_out = w_mod.shape
    batch = c.shape[0]
    return pl.pallas_call(
        _mod_kernel,
        grid=(depth, n_out // MOD_COL_TILE),
        in_specs=[
            pl.BlockSpec((batch, d_model), lambda l, j: (0, 0)),
            pl.BlockSpec((None, d_model, MOD_COL_TILE), lambda l, j: (l, 0, j)),
            pl.BlockSpec((None, 1, MOD_COL_TILE), lambda l, j: (l, 0, j)),
        ],
        out_specs=pl.BlockSpec((None, batch, MOD_COL_TILE), lambda l, j: (l, 0, j)),
        out_shape=jax.ShapeDtypeStruct((depth, batch, n_out), F32),
        compiler_params=pltpu.CompilerParams(
            dimension_semantics=("arbitrary", "arbitrary"), vmem_limit_bytes=VMEM_LIMIT),
        name="adaln_mod",
    )(c, w_mod, b_mod.reshape(depth, 1, n_out))


def _trig_kernel(lam_inits, pos_ref, inv_ref, dl_ref, cs_ref, lam_ref):
    ang = pos_ref[...].astype(F32) * inv_ref[...]
    lane = lax.broadcasted_iota(jnp.int32, ang.shape, 1)
    cs = jnp.where((lane & (ROPE_DIM - 1)) < ROPE_HALF, jnp.cos(ang), jnp.sin(ang))
    for i, piece in enumerate(_bf16_pieces(cs, N_PIECES)):
        cs_ref[i] = piece.astype(BF16)
    for l, lam_init in enumerate(lam_inits):
        lp = dl_ref[l]
        a = jnp.sum(lp[0:1] * lp[1:2], axis=-1, keepdims=True)
        b = jnp.sum(lp[2:3] * lp[3:4], axis=-1, keepdims=True)
        lam_ref[l] = jnp.broadcast_to(jnp.exp(a) - jnp.exp(b) + lam_init, (1, LANES))


def _trig_call(positions, diff_lambda, lam_inits):
    batch, seq = positions.shape
    depth = diff_lambda.shape[0]
    inv = ROPE_THETA ** (-jnp.arange(0, ROPE_DIM, 2, dtype=F32) / ROPE_DIM)
    n_rows = batch * seq * ROPE_DIM // LANES
    pos_rep = jnp.broadcast_to(positions[..., None], (batch, seq, ROPE_DIM)).reshape(n_rows, LANES)
    inv_row = jnp.tile(jnp.concatenate([inv, inv]), LANES // ROPE_DIM)[None, :]
    cs, lam = pl.pallas_call(
        functools.partial(_trig_kernel, lam_inits),
        out_shape=(jax.ShapeDtypeStruct((N_PIECES, n_rows, LANES), BF16),
                   jax.ShapeDtypeStruct((depth, 1, LANES), F32)),
        name="rope_trig_lambda",
    )(pos_rep, inv_row, diff_lambda)
    cs = cs.reshape(N_PIECES, batch, seq, ROPE_DIM)
    return jnp.concatenate([cs[i] for i in range(N_PIECES)], axis=-1), lam


def _rope_expansion():
    expand = np.zeros((ROPE_DIM, 3 * LANES), np.float32)
    base = np.zeros((1, 3 * LANES), np.float32)
    base[0, :LANES] = 1.0
    for h0 in range(0, LANES, HEAD_DIM):
        base[0, h0:h0 + ROPE_DIM] = 0.0
        for j in range(ROPE_HALF):
            expand[j, h0 + j] = 1.0
            expand[j, h0 + ROPE_HALF + j] = 1.0
            expand[ROPE_HALF + j, LANES + h0 + j] = -1.0
            expand[ROPE_HALF + j, 2 * LANES + h0 + ROPE_HALF + j] = 1.0
    return jnp.asarray(np.tile(expand, (N_PIECES, 1)), BF16), jnp.asarray(base)


def _adaln(x, norm_gain, scale, shift):
    ms = jnp.mean(x * x, axis=-1, keepdims=True)
    return (x * lax.rsqrt(ms + EPS)) * (norm_gain * (1.0 + scale)) + shift


def _inproj_kernel(x_ref, nm_ref, sc_ref, sh_ref, w_ref, gsum_ref, gains_ref, cs_ref, exp_ref, base_ref,
                   o_ref):
    h = _adaln(x_ref[...], nm_ref[...], sc_ref[...], sh_ref[...]).astype(BF16)
    tab = base_ref[...] + _dot(cs_ref[...], exp_ref[...])
    rope_c, rope_n, rope_p = tab[:, :LANES], tab[:, LANES:2 * LANES], tab[:, 2 * LANES:]
    group_mean = gsum_ref[...]
    n_chunks = w_ref.shape[1] // MXU_DIM
    gain_row = {0: 0, 1: 1, 3: 2, 4: 3}

    def project(c):
        return _dot(h, w_ref[:, c * MXU_DIM:(c + 1) * MXU_DIM])

    def finish(c, y):
        kind = c // 2
        if kind not in gain_row:
            o_ref[:, c * MXU_DIM:(c + 1) * MXU_DIM] = y.astype(BF16)
            return
        inv_rms = lax.rsqrt(_dot((y * y).astype(BF16), group_mean) + EPS)
        r = gain_row[kind]
        y = y * gains_ref[r:r + 1, :]
        for half in range(MXU_DIM // LANES):
            lanes = slice(half * LANES, (half + 1) * LANES)
            yh = y[:, lanes]
            yh = (yh * rope_c + pltpu.roll(yh, LANES - ROPE_HALF, 1) * rope_n
                  + pltpu.roll(yh, ROPE_HALF, 1) * rope_p)
            lo = c * MXU_DIM + half * LANES
            o_ref[:, lo:lo + LANES] = (yh * inv_rms[:, lanes]).astype(BF16)

    upcoming = project(0)
    for c in range(n_chunks):
        y = upcoming
        if c + 1 < n_chunks:
            upcoming = project(c + 1)
        finish(c, y)


def _mod_spec(d_model, layer, k):
    return pl.BlockSpec((None, None, None, 1, d_model), lambda b, i: (layer, b, k, 0, 0))


def _layer_spec(array, layer):
    return pl.BlockSpec((None,) + array.shape[1:], lambda b, i: (layer, 0, 0))


def _const_spec(array):
    return pl.BlockSpec(array.shape, lambda b, i: (0,) * array.ndim)


def _inproj_call(x, mod5, layer, norm_gain, w_in, group_mean, gains, cs, rope_expand, rope_base):
    batch, seq, d_model = x.shape
    n_out = w_in.shape[2]
    return pl.pallas_call(
        _inproj_kernel,
        grid=(batch, seq // ROW_TILE),
        in_specs=[
            pl.BlockSpec((None, ROW_TILE, d_model), lambda b, i: (b, i, 0)),
            _layer_spec(norm_gain, layer),
            _mod_spec(d_model, layer, 1),
            _mod_spec(d_model, layer, 0),
            _layer_spec(w_in, layer),
            _const_spec(group_mean),
            _layer_spec(gains, layer),
            pl.BlockSpec((None, ROW_TILE, N_PIECES * ROPE_DIM), lambda b, i: (b, i, 0)),
            _const_spec(rope_expand),
            _const_spec(rope_base),
        ],
        out_specs=pl.BlockSpec((None, ROW_TILE, n_out), lambda b, i: (b, i, 0)),
        out_shape=jax.ShapeDtypeStruct((batch, seq, n_out), BF16),
        compiler_params=pltpu.CompilerParams(
            dimension_semantics=("parallel", "parallel"), vmem_limit_bytes=VMEM_LIMIT),
        name="in_projection",
    )(x, norm_gain, mod5, mod5, w_in, group_mean, gains, cs, rope_expand, rope_base)


MASKED = -1e30
ONES_ROWS = 16
PIPE_SLOTS = 2
GROUPS = 2


def _slot(i, half):
    return 2 * (i % PIPE_SLOTS) + half


def _slot_scratch(seq):
    return ([pltpu.VMEM((seq, ATT_TILE), F32) for _ in range(2 * PIPE_SLOTS)]
            + [pltpu.VMEM((seq, ATT_TILE), BF16) for _ in range(2 * PIPE_SLOTS)])


def _store_scores(s_ref, c, s_t):
    n_past = c * ATT_TILE
    own = s_t[n_past:, :]
    key = lax.broadcasted_iota(jnp.int32, own.shape, 0)
    query = lax.broadcasted_iota(jnp.int32, own.shape, 1)
    own = jnp.where(key <= query, own, -jnp.inf)
    s_ref[n_past:n_past + ATT_TILE, :] = own
    m = jnp.max(own, axis=0, keepdims=True)
    if c > 0:
        past = s_t[:n_past, :]
        s_ref[:n_past, :] = past
        m = jnp.maximum(m, jnp.max(past, axis=0, keepdims=True))
    return m


def _store_numerators(s_ref, p_ref, m, c):
    n_kv = (c + 1) * ATT_TILE
    p_ref[:n_kv, :] = jnp.exp2(s_ref[:n_kv, :] - m).astype(BF16)


def _weighted_values(p_ref, vt_ref, c, n_rows):
    n_kv = (c + 1) * ATT_TILE
    o_aug = _dot(vt_ref[:, :n_kv], p_ref[:n_kv, :])
    return o_aug[:n_rows] * (1.0 / o_aug[n_rows:n_rows + 1])


def _tile_order(n_tiles):
    return list(range(0, n_tiles, 2)) + list(range(1, n_tiles, 2))[::-1]


def _staged(groups, n_tiles, s_refs, p_refs):
    for group in groups:
        group["prepare"]()
    jobs = [(group, c) for group in groups for c in _tile_order(n_tiles)]
    job = lambda i: jobs[i] if 0 <= i < len(jobs) else None
    maxes, ahead = {}, {}
    for i in range(-2, len(jobs) + 1):
        if job(i + 2) is not None and "ahead" in job(i + 2)[0]:
            group, c = job(i + 2)
            for half in range(2):
                ahead[i + 2, half] = group["ahead"](c, half)
        for half in range(2):
            if job(i + 1) is not None:
                group, c = job(i + 1)
                maxes[i + 1, half] = group["scores"](c, half, ahead.pop((i + 1, half), None),
                                                     s_refs[_slot(i + 1, half)])
            if job(i) is not None:
                _store_numerators(s_refs[_slot(i, half)], p_refs[_slot(i, half)], maxes.pop((i, half)),
                                  job(i)[1])
        if job(i - 1) is not None:
            group, c = job(i - 1)
            group["output"](c, [p_refs[_slot(i - 1, half)] for half in range(2)])


def _lane_block(ref, g):
    return ref.at[:, g * LANES:(g + 1) * LANES]


def _tile_rows(c):
    return slice(c * ATT_TILE, (c + 1) * ATT_TILE)


def _transposed(ref):
    return ref[...].astype(F32).T


def _half_rows(qt_ref, c, half):
    q_t = qt_ref[:, _tile_rows(c)]
    row = lax.broadcasted_iota(jnp.int32, q_t.shape, 0)
    mine = (row < HEAD_DIM) if half == 0 else (row >= HEAD_DIM)
    return jnp.where(mine, q_t, jnp.zeros_like(q_t))


KMEAN_ROWS = 16
GATE_ROWS = 8


def _moba_prepare(q_ref, k_ref, v_ref, qt_ref, kaug_ref, kmean_ref, vt_ref):
    seq = k_ref.shape[0]
    n_blocks = seq // MOBA_BLOCK
    kaug_ref[:, :LANES] = k_ref[...]
    row_block = lax.broadcasted_iota(jnp.int32, (seq, LANES), 0) // MOBA_BLOCK
    lane = lax.broadcasted_iota(jnp.int32, (seq, LANES), 1)
    kaug_ref[:, LANES:] = jnp.where(row_block == lane, 1.0, 0.0).astype(BF16)
    means = [jnp.mean(k_ref[_tile_rows(j), :].astype(F32), axis=0, keepdims=True) for j in range(n_blocks)]
    means = jnp.concatenate(means + [jnp.zeros((KMEAN_ROWS - n_blocks, LANES), F32)], axis=0)
    hi = means.astype(BF16)
    lo = (means - hi.astype(F32)).astype(BF16)
    kmean_ref[...] = jnp.concatenate([hi, lo], axis=1)
    qt_ref[...] = _transposed(q_ref).astype(BF16)
    v_t = _transposed(v_ref)
    for head in range(LANES // HEAD_DIM):
        vt_ref[head, :HEAD_DIM, :] = v_t[head * HEAD_DIM:(head + 1) * HEAD_DIM].astype(BF16)
        vt_ref[head, HEAD_DIM:, :] = jnp.ones((ONES_ROWS, seq), BF16)


def _moba_block_bias(c, head, qt_ref, kmean_ref):
    if c <= MOBA_TOPK:
        return None
    qh_t = _half_rows(qt_ref, c, head)
    gate = _dot(kmean_ref[...], jnp.concatenate([qh_t, qh_t], axis=0))[:GATE_ROWS, :]
    blk = lax.broadcasted_iota(jnp.int32, gate.shape, 0)
    cnt = jnp.zeros(gate.shape, jnp.int32)
    for m in range(c):
        gm = gate[m:m + 1, :]
        cnt = cnt + jnp.where(blk > m, jnp.where(gm >= gate, 1, 0), jnp.where(gm > gate, 1, 0))
    bias_t = jnp.where(blk < c, jnp.where(cnt < MOBA_TOPK, 0.0, MASKED), 0.0)
    bias_t = jnp.concatenate([bias_t, jnp.zeros((LANES - GATE_ROWS, gate.shape[1]), F32)], axis=0)
    return bias_t.astype(BF16)


def _moba_scores(c, head, bias_t, s_ref, qt_ref, k_ref, kaug_ref):
    n_kv = (c + 1) * ATT_TILE
    qh_t = _half_rows(qt_ref, c, head)
    if bias_t is None:
        s_t = _dot(k_ref[:n_kv, :], qh_t)
    else:
        s_t = _dot(kaug_ref[:n_kv, :], jnp.concatenate([qh_t, bias_t], axis=0))
    return _store_scores(s_ref, c, s_t)


def _moba_output(c, p_refs, vt_ref, gn_ref, o_ref):
    outs = []
    for head in range(LANES // HEAD_DIM):
        o_h = _weighted_values(p_refs[head], vt_ref.at[head], c, HEAD_DIM)
        ms = jnp.mean(o_h * o_h, axis=0, keepdims=True)
        outs.append(o_h * lax.rsqrt(ms + EPS))
    o_t = jnp.concatenate(outs, axis=0)
    o_ref[_tile_rows(c), :] = (o_t.T * gn_ref[...]).astype(BF16)


def _moba_kernel(q_ref, k_ref, v_ref, gn_ref, o_ref, qt_ref, kaug_ref, kmean_ref, vt_ref, *slot_refs):
    s_refs, p_refs = slot_refs[:2 * PIPE_SLOTS], slot_refs[2 * PIPE_SLOTS:]
    groups = []
    for g in range(GROUPS):
        q_g, k_g, v_g, o_g = [_lane_block(ref, g) for ref in (q_ref, k_ref, v_ref, o_ref)]
        qt_g, kaug_g, kmean_g, vt_g = qt_ref.at[g], kaug_ref.at[g], kmean_ref.at[g], vt_ref.at[g]
        groups.append(dict(
            prepare=functools.partial(_moba_prepare, q_g, k_g, v_g, qt_g, kaug_g, kmean_g, vt_g),
            ahead=functools.partial(_moba_block_bias, qt_ref=qt_g, kmean_ref=kmean_g),
            scores=functools.partial(_moba_scores, qt_ref=qt_g, k_ref=k_g, kaug_ref=kaug_g),
            output=functools.partial(_moba_output, vt_ref=vt_g, gn_ref=gn_ref, o_ref=o_g)))
    _staged(groups, k_ref.shape[0] // ATT_TILE, s_refs, p_refs)


def _moba_call(proj, out_gain, layer, width):
    batch, seq, _ = proj.shape
    n_steps = width // (GROUPS * LANES)
    col_spec = lambda c0: pl.BlockSpec((None, seq, GROUPS * LANES), lambda b, p: (b, 0, c0 + p))
    return pl.pallas_call(
        _moba_kernel,
        grid=(batch, n_steps),
        in_specs=[col_spec(0), col_spec(n_steps), col_spec(2 * n_steps),
                  pl.BlockSpec((None, 1, LANES), lambda b, p: (layer, 0, 0))],
        out_specs=col_spec(0),
        out_shape=jax.ShapeDtypeStruct((batch, seq, width), BF16),
        scratch_shapes=[pltpu.VMEM((GROUPS, LANES, seq), BF16),
                        pltpu.VMEM((GROUPS, seq, 2 * LANES), BF16),
                        pltpu.VMEM((GROUPS, KMEAN_ROWS, 2 * LANES), BF16),
                        pltpu.VMEM((GROUPS, LANES // HEAD_DIM, HEAD_DIM + ONES_ROWS, seq), BF16),
                        *_slot_scratch(seq)],
        compiler_params=pltpu.CompilerParams(
            dimension_semantics=("parallel", "parallel"), vmem_limit_bytes=VMEM_LIMIT),
        name="moba_attention",
    )(proj, proj, proj, out_gain)


def _diff_prepare(q_ref, v_ref, qt_ref, vt_ref):
    qt_ref[...] = _transposed(q_ref).astype(BF16)
    vt_ref[:LANES, :] = _transposed(v_ref).astype(BF16)
    vt_ref[LANES:, :] = jnp.ones((ONES_ROWS, v_ref.shape[0]), BF16)


def _diff_scores(c, comp, _, s_ref, qt_ref, k_ref):
    return _store_scores(s_ref, c, _dot(k_ref[:(c + 1) * ATT_TILE, :], _half_rows(qt_ref, c, comp)))


def _diff_output(c, p_refs, out_scale, lam_ref, vt_ref, gn_ref, o_ref):
    a1, a2 = [_weighted_values(p_ref, vt_ref, c, LANES) for p_ref in p_refs]
    o_t = a1 - lam_ref[0:1, 0:1] * a2
    ms = jnp.mean(o_t * o_t, axis=0, keepdims=True)
    o_t = o_t * lax.rsqrt(ms + EPS)
    o_ref[_tile_rows(c), :] = ((o_t.T * gn_ref[...]) * out_scale).astype(BF16)


def _diff_kernel(out_scale, lam_ref, q_ref, k_ref, v_ref, gn_ref, o_ref, qt_ref, vt_ref, *slot_refs):
    s_refs, p_refs = slot_refs[:2 * PIPE_SLOTS], slot_refs[2 * PIPE_SLOTS:]
    groups = []
    for g in range(GROUPS):
        q_g, k_g, v_g, o_g = [_lane_block(ref, g) for ref in (q_ref, k_ref, v_ref, o_ref)]
        groups.append(dict(
            prepare=functools.partial(_diff_prepare, q_g, v_g, qt_ref.at[g], vt_ref.at[g]),
            scores=functools.partial(_diff_scores, qt_ref=qt_ref.at[g], k_ref=k_g),
            output=functools.partial(_diff_output, out_scale=out_scale, lam_ref=lam_ref, vt_ref=vt_ref.at[g],
                                     gn_ref=gn_ref, o_ref=o_g)))
    _staged(groups, k_ref.shape[0] // ATT_TILE, s_refs, p_refs)


def _diff_call(proj, lam, layer, out_gain, width, col0, out_scale):
    batch, seq, _ = proj.shape
    n_steps = width // (GROUPS * LANES)
    c0 = col0 // (GROUPS * LANES)
    col_spec = lambda c: pl.BlockSpec((None, seq, GROUPS * LANES), lambda b, h: (b, 0, c + h))
    return pl.pallas_call(
        functools.partial(_diff_kernel, out_scale),
        grid=(batch, n_steps),
        in_specs=[pl.BlockSpec((None, 1, LANES), lambda b, h: (layer, 0, 0)),
                  col_spec(c0), col_spec(c0 + n_steps), col_spec(c0 + 2 * n_steps),
                  pl.BlockSpec((None, 1, LANES), lambda b, h: (layer, 0, 0))],
        out_specs=col_spec(0),
        out_shape=jax.ShapeDtypeStruct((batch, seq, width), BF16),
        scratch_shapes=[pltpu.VMEM((GROUPS, LANES, seq), BF16),
                        pltpu.VMEM((GROUPS, LANES + ONES_ROWS, seq), BF16),
                        *_slot_scratch(seq)],
        compiler_params=pltpu.CompilerParams(
            dimension_semantics=("parallel", "parallel"), vmem_limit_bytes=VMEM_LIMIT),
        name="diff_attention",
    )(lam, proj, proj, proj, out_gain)


def _mlp_kernel(om_ref, od_ref, wo_ref, x_ref, ga_ref, nf_ref, sc_ref, sh_ref, gf_ref, wg_ref, wu_ref,
                wd_ref, o_ref, act_ref):
    width = om_ref.shape[1]
    half_rows = o_ref.shape[0] // 2
    halves = [slice(r * half_rows, (r + 1) * half_rows) for r in range(2)]
    hs = []
    for rows in halves:
        y = _dot(om_ref[rows, :], wo_ref[:width, :]) + _dot(od_ref[rows, :], wo_ref[width:, :])
        x = x_ref[rows, :] + ga_ref[...] * y
        o_ref[rows, :] = x
        hs.append(_adaln(x, nf_ref[...], sc_ref[...], sh_ref[...]).astype(BF16))
    for rows, h in zip(halves, hs):
        for c in range(wg_ref.shape[1] // MXU_DIM):
            cols = slice(c * MXU_DIM, (c + 1) * MXU_DIM)
            g = _dot(h, wg_ref[:, cols])
            u = _dot(h, wu_ref[:, cols])
            act_ref[rows, cols] = ((g / (1.0 + jnp.exp(-g))) * u).astype(BF16)
        o_ref[rows, :] = o_ref[rows, :] + gf_ref[...] * _dot(act_ref[rows, :], wd_ref[...])


def _mlp_call(o_m, o_d, x, mod5, layer, norm_gain, w_out, w_gate, w_up, w_down):
    batch, seq, d_model = x.shape
    d_ff = w_gate.shape[2]
    row_spec = lambda cols: pl.BlockSpec((None, MLP_ROW_TILE, cols), lambda b, i: (b, i, 0))
    resident = lambda w: pl.BlockSpec((None,) + w.shape[1:], lambda b, i: (layer, 0, 0),
                                      pipeline_mode=pl.Buffered(1))
    return pl.pallas_call(
        _mlp_kernel,
        grid=(batch, seq // MLP_ROW_TILE),
        in_specs=[
            row_spec(o_m.shape[2]), row_spec(o_d.shape[2]),
            resident(w_out),
            row_spec(d_model),
            _mod_spec(d_model, layer, 2),
            _layer_spec(norm_gain, layer),
            _mod_spec(d_model, layer, 4),
            _mod_spec(d_model, layer, 3),
            _mod_spec(d_model, layer, 5),
            resident(w_gate), resident(w_up), resident(w_down),
        ],
        out_specs=row_spec(d_model),
        out_shape=jax.ShapeDtypeStruct((batch, seq, d_model), F32),
        scratch_shapes=[pltpu.VMEM((MLP_ROW_TILE, d_ff), BF16)],
        compiler_params=pltpu.CompilerParams(
            dimension_semantics=("parallel", "parallel"), vmem_limit_bytes=VMEM_LIMIT),
        name="out_projection_swiglu",
    )(o_m, o_d, w_out, x, mod5, norm_gain, mod5, mod5, mod5, w_gate, w_up, w_down)


def kernel(x, c, positions, w_mod, b_mod, norm_mix, w_in, moba_q_norm, moba_k_norm, moba_out_norm,
           diff_q_norm, diff_k_norm, diff_lambda, diff_subln, w_out, norm_ffn, w_gate, w_up, w_down):
    batch, seq, d_model = x.shape
    depth = w_mod.shape[0]
    moba_width = d_model // 2
    diff_width = d_model // 2
    assert seq % ROW_TILE == 0 and seq % MLP_ROW_TILE == 0 and seq % MOBA_BLOCK == 0
    assert ATT_TILE == MOBA_BLOCK
    assert seq // MOBA_BLOCK <= GATE_ROWS
    assert w_in.shape[2] == 3 * moba_width + 3 * diff_width and moba_width == 2 * MXU_DIM

    lam_inits = tuple(0.8 - 0.6 * math.exp(-0.3 * l) for l in range(depth))
    mod5 = _mod_call(c, w_mod, b_mod).reshape(depth, batch, N_MOD, 1, d_model)
    cs, lam = _trig_call(positions, diff_lambda, lam_inits)
    rope_expand, rope_base = _rope_expansion()

    head_of_lane = np.arange(MXU_DIM) // HEAD_DIM
    group_mean = jnp.asarray((head_of_lane[:, None] == head_of_lane[None, :]) / HEAD_DIM, BF16)
    reps = MXU_DIM // HEAD_DIM
    qk_scale = HEAD_DIM ** -0.5 * math.log2(math.e)
    gains = jnp.stack([jnp.tile(moba_q_norm, (1, reps)) * qk_scale, jnp.tile(moba_k_norm, (1, reps)),
                       jnp.tile(diff_q_norm, (1, reps)) * qk_scale, jnp.tile(diff_k_norm, (1, reps))],
                      axis=1)
    moba_out_gain = jnp.tile(moba_out_norm, (1, LANES // HEAD_DIM))[:, None, :]
    diff_out_gain = diff_subln[:, None, :]
    norm_mix3, norm_ffn3 = norm_mix[:, None, :], norm_ffn[:, None, :]

    w_in_b, w_out_b = w_in.astype(BF16), w_out.astype(BF16)
    w_gate_b, w_up_b, w_down_b = w_gate.astype(BF16), w_up.astype(BF16), w_down.astype(BF16)

    for l in range(depth):
        proj = _inproj_call(x, mod5, l, norm_mix3, w_in_b, group_mean, gains, cs, rope_expand, rope_base)
        o_m = _moba_call(proj, moba_out_gain, l, moba_width)
        o_d = _diff_call(proj, lam, l, diff_out_gain, diff_width, 3 * moba_width, 1.0 - lam_inits[l])
        x = _mlp_call(o_m, o_d, x, mod5, l, norm_ffn3, w_out_b, w_gate_b, w_up_b, w_down_b)
    return x
```

```python
import functools
import math

import numpy as np
import jax
import jax.numpy as jnp
from jax import lax
from jax.experimental import pallas as pl
from jax.experimental.pallas import tpu as pltpu

F32 = jnp.float32
BF16 = jnp.bfloat16

LANES = 128
MXU_DIM = 256
VMEM_LIMIT = 56 * 1024 * 1024

HEAD_DIM = 64
ROPE_DIM = HEAD_DIM // 4
ROPE_HALF = ROPE_DIM // 2
ROPE_THETA = 500000.0
MOBA_BLOCK = 256
MOBA_TOPK = 3
N_MOD = 6
EPS = 1e-6

ROW_TILE = 512
MLP_ROW_TILE = 1024
ATT_TILE = 256
MOD_COL_TILE = 1536


def _dot(a, b):
    return jnp.dot(a, b, preferred_element_type=F32)


N_PIECES = 3


def _bf16_pieces(x, n):
    pieces = []
    for _ in range(n):
        piece = x.astype(BF16).astype(F32)
        pieces.append(piece)
        x = x - piece
    return pieces


def _mod_kernel(c_ref, w_ref, b_ref, o_ref):
    c = c_ref[...]
    batch = c.shape[0]
    cond = jnp.concatenate(_bf16_pieces(c / (1.0 + jnp.exp(-c)), N_PIECES), axis=0).astype(BF16)
    w_hi, w_lo = [piece.astype(BF16) for piece in _bf16_pieces(w_ref[...], 2)]
    y_hi = _dot(cond, w_hi)
    y_lo = _dot(cond[:2 * batch], w_lo)
    o_ref[...] = ((y_hi[2 * batch:] + y_lo[batch:]) + (y_hi[batch:2 * batch] + y_lo[:batch])
                  + y_hi[:batch] + b_ref[...])


def _mod_call(c, w_mod, b_mod):
    depth, d_model, n_out = w_mod.shape
    batch = c.shape[0]
    return pl.pallas_call(
        _mod_kernel,
        grid=(depth, n_out // MOD_COL_TILE),
        in_specs=[
            pl.BlockSpec((batch, d_model), lambda l, j: (0, 0)),
            pl.BlockSpec((None, d_model, MOD_COL_TILE), lambda l, j: (l, 0, j)),
            pl.BlockSpec((None, 1, MOD_COL_TILE), lambda l, j: (l, 0, j)),
        ],
        out_specs=pl.BlockSpec((None, batch, MOD_COL_TILE), lambda l, j: (l, 0, j)),
        out_shape=jax.ShapeDtypeStruct((depth, batch, n_out), F32),
        compiler_params=pltpu.CompilerParams(
            dimension_semantics=("arbitrary", "arbitrary"), vmem_limit_bytes=VMEM_LIMIT),
        name="adaln_mod",
    )(c, w_mod, b_mod.reshape(depth, 1, n_out))


def _trig_kernel(lam_inits, pos_ref, inv_ref, dl_ref, cs_ref, lam_ref):
    ang = pos_ref[...].astype(F32) * inv_ref[...]
    lane = lax.broadcasted_iota(jnp.int32, ang.shape, 1)
    cs = jnp.where((lane & (ROPE_DIM - 1)) < ROPE_HALF, jnp.cos(ang), jnp.sin(ang))
    for i, piece in enumerate(_bf16_pieces(cs, N_PIECES)):
        cs_ref[i] = piece.astype(BF16)
    for l, lam_init in enumerate(lam_inits):
        lp = dl_ref[l]
        a = jnp.sum(lp[0:1] * lp[1:2], axis=-1, keepdims=True)
        b = jnp.sum(lp[2:3] * lp[3:4], axis=-1, keepdims=True)
        lam_ref[l] = jnp.broadcast_to(jnp.exp(a) - jnp.exp(b) + lam_init, (1, LANES))


def _trig_call(positions, diff_lambda, lam_inits):
    batch, seq = positions.shape
    depth = diff_lambda.shape[0]
    inv = ROPE_THETA ** (-jnp.arange(0, ROPE_DIM, 2, dtype=F32) / ROPE_DIM)
    n_rows = batch * seq * ROPE_DIM // LANES
    pos_rep = jnp.broadcast_to(positions[..., None], (batch, seq, ROPE_DIM)).reshape(n_rows, LANES)
    inv_row = jnp.tile(jnp.concatenate([inv, inv]), LANES // ROPE_DIM)[None, :]
    cs, lam = pl.pallas_call(
        functools.partial(_trig_kernel, lam_inits),
        out_shape=(jax.ShapeDtypeStruct((N_PIECES, n_rows, LANES), BF16),
                   jax.ShapeDtypeStruct((depth, 1, LANES), F32)),
        name="rope_trig_lambda",
    )(pos_rep, inv_row, diff_lambda)
    cs = cs.reshape(N_PIECES, batch, seq, ROPE_DIM)
    return jnp.concatenate([cs[i] for i in range(N_PIECES)], axis=-1), lam


def _rope_expansion():
    expand = np.zeros((ROPE_DIM, 3 * LANES), np.float32)
    base = np.zeros((1, 3 * LANES), np.float32)
    base[0, :LANES] = 1.0
    for h0 in range(0, LANES, HEAD_DIM):
        base[0, h0:h0 + ROPE_DIM] = 0.0
        for j in range(ROPE_HALF):
            expand[j, h0 + j] = 1.0
            expand[j, h0 + ROPE_HALF + j] = 1.0
            expand[ROPE_HALF + j, LANES + h0 + j] = -1.0
            expand[ROPE_HALF + j, 2 * LANES + h0 + ROPE_HALF + j] = 1.0
    return jnp.asarray(np.tile(expand, (N_PIECES, 1)), BF16), jnp.asarray(base)


def _adaln(x, norm_gain, scale, shift):
    ms = jnp.mean(x * x, axis=-1, keepdims=True)
    return (x * lax.rsqrt(ms + EPS)) * (norm_gain * (1.0 + scale)) + shift


def _inproj_kernel(x_ref, nm_ref, sc_ref, sh_ref, w_ref, gsum_ref, gains_ref, cs_ref, exp_ref, base_ref,
                   o_ref):
    h = _adaln(x_ref[...], nm_ref[...], sc_ref[...], sh_ref[...]).astype(BF16)
    tab = base_ref[...] + _dot(cs_ref[...], exp_ref[...])
    rope_c, rope_n, rope_p = tab[:, :LANES], tab[:, LANES:2 * LANES], tab[:, 2 * LANES:]
    group_mean = gsum_ref[...]
    n_chunks = w_ref.shape[1] // MXU_DIM
    gain_row = {0: 0, 1: 1, 3: 2, 4: 3}

    def project(c):
        return _dot(h, w_ref[:, c * MXU_DIM:(c + 1) * MXU_DIM])

    def finish(c, y):
        kind = c // 2
        if kind not in gain_row:
            o_ref[:, c * MXU_DIM:(c + 1) * MXU_DIM] = y.astype(BF16)
            return
        inv_rms = lax.rsqrt(_dot((y * y).astype(BF16), group_mean) + EPS)
        r = gain_row[kind]
        y = y * gains_ref[r:r + 1, :]
        for half in range(MXU_DIM // LANES):
            lanes = slice(half * LANES, (half + 1) * LANES)
            yh = y[:, lanes]
            yh = (yh * rope_c + pltpu.roll(yh, LANES - ROPE_HALF, 1) * rope_n
                  + pltpu.roll(yh, ROPE_HALF, 1) * rope_p)
            lo = c * MXU_DIM + half * LANES
            o_ref[:, lo:lo + LANES] = (yh * inv_rms[:, lanes]).astype(BF16)

    upcoming = project(0)
    for c in range(n_chunks):
        y = upcoming
        if c + 1 < n_chunks:
            upcoming = project(c + 1)
        finish(c, y)


def _mod_spec(d_model, layer, k):
    return pl.BlockSpec((None, None, None, 1, d_model), lambda b, i: (layer, b, k, 0, 0))


def _layer_spec(array, layer):
    return pl.BlockSpec((None,) + array.shape[1:], lambda b, i: (layer, 0, 0))


def _const_spec(array):
    return pl.BlockSpec(array.shape, lambda b, i: (0,) * array.ndim)


def _inproj_call(x, mod5, layer, norm_gain, w_in, group_mean, gains, cs, rope_expand, rope_base):
    batch, seq, d_model = x.shape
    n_out = w_in.shape[2]
    return pl.pallas_call(
        _inproj_kernel,
        grid=(batch, seq // ROW_TILE),
        in_specs=[
            pl.BlockSpec((None, ROW_TILE, d_model), lambda b, i: (b, i, 0)),
            _layer_spec(norm_gain, layer),
            _mod_spec(d_model, layer, 1),
            _mod_spec(d_model, layer, 0),
            _layer_spec(w_in, layer),
            _const_spec(group_mean),
            _layer_spec(gains, layer),
            pl.BlockSpec((None, ROW_TILE, N_PIECES * ROPE_DIM), lambda b, i: (b, i, 0)),
            _const_spec(rope_expand),
            _const_spec(rope_base),
        ],
        out_specs=pl.BlockSpec((None, ROW_TILE, n_out), lambda b, i: (b, i, 0)),
        out_shape=jax.ShapeDtypeStruct((batch, seq, n_out), BF16),
        compiler_params=pltpu.CompilerParams(
            dimension_semantics=("parallel", "parallel"), vmem_limit_bytes=VMEM_LIMIT),
        name="in_projection",
    )(x, norm_gain, mod5, mod5, w_in, group_mean, gains, cs, rope_expand, rope_base)


MASKED = -1e30
ONES_ROWS = 16
PIPE_SLOTS = 2
GROUPS = 2


def _slot(i, half):
    return 2 * (i % PIPE_SLOTS) + half


def _slot_scratch(seq):
    return ([pltpu.VMEM((seq, ATT_TILE), F32) for _ in range(2 * PIPE_SLOTS)]
            + [pltpu.VMEM((seq, ATT_TILE), BF16) for _ in range(2 * PIPE_SLOTS)])


def _store_scores(s_ref, c, s_t):
    n_past = c * ATT_TILE
    own = s_t[n_past:, :]
    key = lax.broadcasted_iota(jnp.int32, own.shape, 0)
    query = lax.broadcasted_iota(jnp.int32, own.shape, 1)
    own = jnp.where(key <= query, own, -jnp.inf)
    s_ref[n_past:n_past + ATT_TILE, :] = own
    m = jnp.max(own, axis=0, keepdims=True)
    if c > 0:
        past = s_t[:n_past, :]
        s_ref[:n_past, :] = past
        m = jnp.maximum(m, jnp.max(past, axis=0, keepdims=True))
    return m


def _store_numerators(s_ref, p_ref, m, c):
    n_kv = (c + 1) * ATT_TILE
    p_ref[:n_kv, :] = jnp.exp2(s_ref[:n_kv, :] - m).astype(BF16)


def _weighted_values(p_ref, vt_ref, c, n_rows):
    n_kv = (c + 1) * ATT_TILE
    o_aug = _dot(vt_ref[:, :n_kv], p_ref[:n_kv, :])
    return o_aug[:n_rows] * (1.0 / o_aug[n_rows:n_rows + 1])


def _tile_order(n_tiles):
    return list(range(0, n_tiles, 2)) + list(range(1, n_tiles, 2))[::-1]


def _staged(groups, n_tiles, s_refs, p_refs):
    for group in groups:
        group["prepare"]()
    jobs = [(group, c) for group in groups for c in _tile_order(n_tiles)]
    job = lambda i: jobs[i] if 0 <= i < len(jobs) else None
    maxes, ahead = {}, {}
    for i in range(-2, len(jobs) + 1):
        if job(i + 2) is not None and "ahead" in job(i + 2)[0]:
            group, c = job(i + 2)
            for half in range(2):
                ahead[i + 2, half] = group["ahead"](c, half)
        for half in range(2):
            if job(i + 1) is not None:
                group, c = job(i + 1)
                maxes[i + 1, half] = group["scores"](c, half, ahead.pop((i + 1, half), None),
                                                     s_refs[_slot(i + 1, half)])
            if job(i) is not None:
                _store_numerators(s_refs[_slot(i, half)], p_refs[_slot(i, half)], maxes.pop((i, half)),
                                  job(i)[1])
        if job(i - 1) is not None:
            group, c = job(i - 1)
            group["output"](c, [p_refs[_slot(i - 1, half)] for half in range(2)])


def _lane_block(ref, g):
    return ref.at[:, g * LANES:(g + 1) * LANES]


def _tile_rows(c):
    return slice(c * ATT_TILE, (c + 1) * ATT_TILE)


def _transposed(ref):
    return ref[...].astype(F32).T


def _half_rows(qt_ref, c, half):
    q_t = qt_ref[:, _tile_rows(c)]
    row = lax.broadcasted_iota(jnp.int32, q_t.shape, 0)
    mine = (row < HEAD_DIM) if half == 0 else (row >= HEAD_DIM)
    return jnp.where(mine, q_t, jnp.zeros_like(q_t))


KMEAN_ROWS = 16
GATE_ROWS = 8


def _moba_prepare(q_ref, k_ref, v_ref, qt_ref, kaug_ref, kmean_ref, vt_ref):
    seq = k_ref.shape[0]
    n_blocks = seq // MOBA_BLOCK
    kaug_ref[:, :LANES] = k_ref[...]
    row_block = lax.broadcasted_iota(jnp.int32, (seq, LANES), 0) // MOBA_BLOCK
    lane = lax.broadcasted_iota(jnp.int32, (seq, LANES), 1)
    kaug_ref[:, LANES:] = jnp.where(row_block == lane, 1.0, 0.0).astype(BF16)
    means = [jnp.mean(k_ref[_tile_rows(j), :].astype(F32), axis=0, keepdims=True) for j in range(n_blocks)]
    means = jnp.concatenate(means + [jnp.zeros((KMEAN_ROWS - n_blocks, LANES), F32)], axis=0)
    hi = means.astype(BF16)
    lo = (means - hi.astype(F32)).astype(BF16)
    kmean_ref[...] = jnp.concatenate([hi, lo], axis=1)
    qt_ref[...] = _transposed(q_ref).astype(BF16)
    v_t = _transposed(v_ref)
    for head in range(LANES // HEAD_DIM):
        vt_ref[head, :HEAD_DIM, :] = v_t[head * HEAD_DIM:(head + 1) * HEAD_DIM].astype(BF16)
        vt_ref[head, HEAD_DIM:, :] = jnp.ones((ONES_ROWS, seq), BF16)


def _moba_block_bias(c, head, qt_ref, kmean_ref):
    if c <= MOBA_TOPK:
        return None
    qh_t = _half_rows(qt_ref, c, head)
    gate = _dot(kmean_ref[...], jnp.concatenate([qh_t, qh_t], axis=0))[:GATE_ROWS, :]
    blk = lax.broadcasted_iota(jnp.int32, gate.shape, 0)
    cnt = jnp.zeros(gate.shape, jnp.int32)
    for m in range(c):
        gm = gate[m:m + 1, :]
        cnt = cnt + jnp.where(blk > m, jnp.where(gm >= gate, 1, 0), jnp.where(gm > gate, 1, 0))
    bias_t = jnp.where(blk < c, jnp.where(cnt < MOBA_TOPK, 0.0, MASKED), 0.0)
    bias_t = jnp.concatenate([bias_t, jnp.zeros((LANES - GATE_ROWS, gate.shape[1]), F32)], axis=0)
    return bias_t.astype(BF16)


def _moba_scores(c, head, bias_t, s_ref, qt_ref, k_ref, kaug_ref):
    n_kv = (c + 1) * ATT_TILE
    qh_t = _half_rows(qt_ref, c, head)
    if bias_t is None:
        s_t = _dot(k_ref[:n_kv, :], qh_t)
    else:
        s_t = _dot(kaug_ref[:n_kv, :], jnp.concatenate([qh_t, bias_t], axis=0))
    return _store_scores(s_ref, c, s_t)


def _moba_output(c, p_refs, vt_ref, gn_ref, o_ref):
    outs = []
    for head in range(LANES // HEAD_DIM):
        o_h = _weighted_values(p_refs[head], vt_ref.at[head], c, HEAD_DIM)
        ms = jnp.mean(o_h * o_h, axis=0, keepdims=True)
        outs.append(o_h * lax.rsqrt(ms + EPS))
    o_t = jnp.concatenate(outs, axis=0)
    o_ref[_tile_rows(c), :] = (o_t.T * gn_ref[...]).astype(BF16)


def _moba_kernel(q_ref, k_ref, v_ref, gn_ref, o_ref, qt_ref, kaug_ref, kmean_ref, vt_ref, *slot_refs):
    s_refs, p_refs = slot_refs[:2 * PIPE_SLOTS], slot_refs[2 * PIPE_SLOTS:]
    groups = []
    for g in range(GROUPS):
        q_g, k_g, v_g, o_g = [_lane_block(ref, g) for ref in (q_ref, k_ref, v_ref, o_ref)]
        qt_g, kaug_g, kmean_g, vt_g = qt_ref.at[g], kaug_ref.at[g], kmean_ref.at[g], vt_ref.at[g]
        groups.append(dict(
            prepare=functools.partial(_moba_prepare, q_g, k_g, v_g, qt_g, kaug_g, kmean_g, vt_g),
            ahead=functools.partial(_moba_block_bias, qt_ref=qt_g, kmean_ref=kmean_g),
            scores=functools.partial(_moba_scores, qt_ref=qt_g, k_ref=k_g, kaug_ref=kaug_g),
            output=functools.partial(_moba_output, vt_ref=vt_g, gn_ref=gn_ref, o_ref=o_g)))
    _staged(groups, k_ref.shape[0] // ATT_TILE, s_refs, p_refs)


def _moba_call(proj, out_gain, layer, width):
    batch, seq, _ = proj.shape
    n_steps = width // (GROUPS * LANES)
    col_spec = lambda c0: pl.BlockSpec((None, seq, GROUPS * LANES), lambda b, p: (b, 0, c0 + p))
    return pl.pallas_call(
        _moba_kernel,
        grid=(batch, n_steps),
        in_specs=[col_spec(0), col_spec(n_steps), col_spec(2 * n_steps),
                  pl.BlockSpec((None, 1, LANES), lambda b, p: (layer, 0, 0))],
        out_specs=col_spec(0),
        out_shape=jax.ShapeDtypeStruct((batch, seq, width), BF16),
        scratch_shapes=[pltpu.VMEM((GROUPS, LANES, seq), BF16),
                        pltpu.VMEM((GROUPS, seq, 2 * LANES), BF16),
                        pltpu.VMEM((GROUPS, KMEAN_ROWS, 2 * LANES), BF16),
                        pltpu.VMEM((GROUPS, LANES // HEAD_DIM, HEAD_DIM + ONES_ROWS, seq), BF16),
                        *_slot_scratch(seq)],
        compiler_params=pltpu.CompilerParams(
            dimension_semantics=("parallel", "parallel"), vmem_limit_bytes=VMEM_LIMIT),
        name="moba_attention",
    )(proj, proj, proj, out_gain)


def _diff_prepare(v_ref, vt_ref):
    vt_ref[:LANES, :] = _transposed(v_ref).astype(BF16)
    vt_ref[LANES:, :] = jnp.ones((ONES_ROWS, v_ref.shape[0]), BF16)


def _diff_scores(c, comp, _, s_ref, q_ref, k_ref):
    q = q_ref[_tile_rows(c), :]
    lane = lax.broadcasted_iota(jnp.int32, (1, LANES), 1)
    mine = (lane < HEAD_DIM) if comp == 0 else (lane >= HEAD_DIM)
    qh = jnp.where(mine, q, jnp.zeros_like(q))
    s_t = lax.dot_general(k_ref[:(c + 1) * ATT_TILE, :], qh, (((1,), (1,)), ((), ())),
                          preferred_element_type=F32)
    return _store_scores(s_ref, c, s_t)


def _diff_output(c, p_refs, out_scale, lam_ref, vt_ref, gn_ref, o_ref):
    a1, a2 = [_weighted_values(p_ref, vt_ref, c, LANES) for p_ref in p_refs]
    o_t = a1 - lam_ref[0:1, 0:1] * a2
    ms = jnp.mean(o_t * o_t, axis=0, keepdims=True)
    o_t = o_t * lax.rsqrt(ms + EPS)
    o_ref[_tile_rows(c), :] = ((o_t.T * gn_ref[...]) * out_scale).astype(BF16)


def _diff_kernel(out_scale, lam_ref, q_ref, k_ref, v_ref, gn_ref, o_ref, vt_ref, *slot_refs):
    s_refs, p_refs = slot_refs[:2 * PIPE_SLOTS], slot_refs[2 * PIPE_SLOTS:]
    groups = []
    for g in range(GROUPS):
        q_g, k_g, v_g, o_g = [_lane_block(ref, g) for ref in (q_ref, k_ref, v_ref, o_ref)]
        groups.append(dict(
            prepare=functools.partial(_diff_prepare, v_g, vt_ref.at[g]),
            scores=functools.partial(_diff_scores, q_ref=q_g, k_ref=k_g),
            output=functools.partial(_diff_output, out_scale=out_scale, lam_ref=lam_ref, vt_ref=vt_ref.at[g],
                                     gn_ref=gn_ref, o_ref=o_g)))
    _staged(groups, k_ref.shape[0] // ATT_TILE, s_refs, p_refs)


def _diff_call(proj, lam, layer, out_gain, width, col0, out_scale):
    batch, seq, _ = proj.shape
    n_steps = width // (GROUPS * LANES)
    c0 = col0 // (GROUPS * LANES)
    col_spec = lambda c: pl.BlockSpec((None, seq, GROUPS * LANES), lambda b, h: (b, 0, c + h))
    return pl.pallas_call(
        functools.partial(_diff_kernel, out_scale),
        grid=(batch, n_steps),
        in_specs=[pl.BlockSpec((None, 1, LANES), lambda b, h: (layer, 0, 0)),
                  col_spec(c0), col_spec(c0 + n_steps), col_spec(c0 + 2 * n_steps),
                  pl.BlockSpec((None, 1, LANES), lambda b, h: (layer, 0, 0))],
        out_specs=col_spec(0),
        out_shape=jax.ShapeDtypeStruct((batch, seq, width), BF16),
        scratch_shapes=[pltpu.VMEM((GROUPS, LANES + ONES_ROWS, seq), BF16),
                        *_slot_scratch(seq)],
        compiler_params=pltpu.CompilerParams(
            dimension_semantics=("parallel", "parallel"), vmem_limit_bytes=VMEM_LIMIT),
        name="diff_attention",
    )(lam, proj, proj, proj, out_gain)


def _mlp_kernel(om_ref, od_ref, wo_ref, x_ref, ga_ref, nf_ref, sc_ref, sh_ref, gf_ref, wg_ref, wu_ref,
                wd_ref, o_ref, act_ref):
    width = om_ref.shape[1]
    half_rows = o_ref.shape[0] // 2
    halves = [slice(r * half_rows, (r + 1) * half_rows) for r in range(2)]
    hs = []
    for rows in halves:
        y = _dot(om_ref[rows, :], wo_ref[:width, :]) + _dot(od_ref[rows, :], wo_ref[width:, :])
        x = x_ref[rows, :] + ga_ref[...] * y
        o_ref[rows, :] = x
        hs.append(_adaln(x, nf_ref[...], sc_ref[...], sh_ref[...]).astype(BF16))
    for rows, h in zip(halves, hs):
        for c in range(wg_ref.shape[1] // MXU_DIM):
            cols = slice(c * MXU_DIM, (c + 1) * MXU_DIM)
            g = _dot(h, wg_ref[:, cols])
            u = _dot(h, wu_ref[:, cols])
            act_ref[rows, cols] = ((g / (1.0 + jnp.exp(-g))) * u).astype(BF16)
        o_ref[rows, :] = o_ref[rows, :] + gf_ref[...] * _dot(act_ref[rows, :], wd_ref[...])


def _mlp_call(o_m, o_d, x, mod5, layer, norm_gain, w_out, w_gate, w_up, w_down):
    batch, seq, d_model = x.shape
    d_ff = w_gate.shape[2]
    row_spec = lambda cols: pl.BlockSpec((None, MLP_ROW_TILE, cols), lambda b, i: (b, i, 0))
    resident = lambda w: pl.BlockSpec((None,) + w.shape[1:], lambda b, i: (layer, 0, 0),
                                      pipeline_mode=pl.Buffered(1))
    return pl.pallas_call(
        _mlp_kernel,
        grid=(batch, seq // MLP_ROW_TILE),
        in_specs=[
            row_spec(o_m.shape[2]), row_spec(o_d.shape[2]),
            resident(w_out),
            row_spec(d_model),
            _mod_spec(d_model, layer, 2),
            _layer_spec(norm_gain, layer),
            _mod_spec(d_model, layer, 4),
            _mod_spec(d_model, layer, 3),
            _mod_spec(d_model, layer, 5),
            resident(w_gate), resident(w_up), resident(w_down),
        ],
        out_specs=row_spec(d_model),
        out_shape=jax.ShapeDtypeStruct((batch, seq, d_model), F32),
        scratch_shapes=[pltpu.VMEM((MLP_ROW_TILE, d_ff), BF16)],
        compiler_params=pltpu.CompilerParams(
            dimension_semantics=("parallel", "parallel"), vmem_limit_bytes=VMEM_LIMIT),
        name="out_projection_swiglu",
    )(o_m, o_d, w_out, x, mod5, norm_gain, mod5, mod5, mod5, w_gate, w_up, w_down)


def kernel(x, c, positions, w_mod, b_mod, norm_mix, w_in, moba_q_norm, moba_k_norm, moba_out_norm,
           diff_q_norm, diff_k_norm, diff_lambda, diff_subln, w_out, norm_ffn, w_gate, w_up, w_down):
    batch, seq, d_model = x.shape
    depth = w_mod.shape[0]
    moba_width = d_model // 2
    diff_width = d_model // 2
    assert seq % ROW_TILE == 0 and seq % MLP_ROW_TILE == 0 and seq % MOBA_BLOCK == 0
    assert ATT_TILE == MOBA_BLOCK
    assert seq // MOBA_BLOCK <= GATE_ROWS
    assert w_in.shape[2] == 3 * moba_width + 3 * diff_width and moba_width == 2 * MXU_DIM

    lam_inits = tuple(0.8 - 0.6 * math.exp(-0.3 * l) for l in range(depth))
    mod5 = _mod_call(c, w_mod, b_mod).reshape(depth, batch, N_MOD, 1, d_model)
    cs, lam = _trig_call(positions, diff_lambda, lam_inits)
    rope_expand, rope_base = _rope_expansion()

    head_of_lane = np.arange(MXU_DIM) // HEAD_DIM
    group_mean = jnp.asarray((head_of_lane[:, None] == head_of_lane[None, :]) / HEAD_DIM, BF16)
    reps = MXU_DIM // HEAD_DIM
    qk_scale = HEAD_DIM ** -0.5 * math.log2(math.e)
    gains = jnp.stack([jnp.tile(moba_q_norm, (1, reps)) * qk_scale, jnp.tile(moba_k_norm, (1, reps)),
                       jnp.tile(diff_q_norm, (1, reps)) * qk_scale, jnp.tile(diff_k_norm, (1, reps))],
                      axis=1)
    moba_out_gain = jnp.tile(moba_out_norm, (1, LANES // HEAD_DIM))[:, None, :]
    diff_out_gain = diff_subln[:, None, :]
    norm_mix3, norm_ffn3 = norm_mix[:, None, :], norm_ffn[:, None, :]

    w_in_b, w_out_b = w_in.astype(BF16), w_out.astype(BF16)
    w_gate_b, w_up_b, w_down_b = w_gate.astype(BF16), w_up.astype(BF16), w_down.astype(BF16)

    for l in range(depth):
        proj = _inproj_call(x, mod5, l, norm_mix3, w_in_b, group_mean, gains, cs, rope_expand, rope_base)
        o_m = _moba_call(proj, moba_out_gain, l, moba_width)
        o_d = _diff_call(proj, lam, l, diff_out_gain, diff_width, 3 * moba_width, 1.0 - lam_inits[l])
        x = _mlp_call(o_m, o_d, x, mod5, l, norm_ffn3, w_out_b, w_gate_b, w_up_b, w_down_b)
    return x
```

```python
import functools
import math

import numpy as np
import jax
import jax.numpy as jnp
from jax import lax
from jax.experimental import pallas as pl
from jax.experimental.pallas import tpu as pltpu

F32 = jnp.float32
BF16 = jnp.bfloat16

LANES = 128
MXU_DIM = 256
VMEM_LIMIT = 56 * 1024 * 1024

HEAD_DIM = 64
ROPE_DIM = HEAD_DIM // 4
ROPE_HALF = ROPE_DIM // 2
ROPE_THETA = 500000.0
MOBA_BLOCK = 256
MOBA_TOPK = 3
N_MOD = 6
EPS = 1e-6

ROW_TILE = 512
MLP_ROW_TILE = 1024
ATT_TILE = 256
MOD_COL_TILE = 1536


def _dot(a, b):
    return jnp.dot(a, b, preferred_element_type=F32)


N_PIECES = 3


def _bf16_pieces(x, n):
    pieces = []
    for _ in range(n):
        piece = x.astype(BF16).astype(F32)
        pieces.append(piece)
        x = x - piece
    return pieces


def _mod_kernel(c_ref, w_ref, b_ref, o_ref):
    c = c_ref[...]
    batch = c.shape[0]
    cond = jnp.concatenate(_bf16_pieces(c / (1.0 + jnp.exp(-c)), N_PIECES), axis=0).astype(BF16)
    w_hi, w_lo = [piece.astype(BF16) for piece in _bf16_pieces(w_ref[...], 2)]
    y_hi = _dot(cond, w_hi)
    y_lo = _dot(cond[:2 * batch], w_lo)
    o_ref[...] = ((y_hi[2 * batch:] + y_lo[batch:]) + (y_hi[batch:2 * batch] + y_lo[:batch])
                  + y_hi[:batch] + b_ref[...])


def _mod_call(c, w_mod, b_mod):
    depth, d_model, n_out = w_mod.shape
    batch = c.shape[0]
    return pl.pallas_call(
        _mod_kernel,
        grid=(depth, n_out // MOD_COL_TILE),
        in_specs=[
            pl.BlockSpec((batch, d_model), lambda l, j: (0, 0)),
            pl.BlockSpec((None, d_model, MOD_COL_TILE), lambda l, j: (l, 0, j)),
            pl.BlockSpec((None, 1, MOD_COL_TILE), lambda l, j: (l, 0, j)),
        ],
        out_specs=pl.BlockSpec((None, batch, MOD_COL_TILE), lambda l, j: (l, 0, j)),
        out_shape=jax.ShapeDtypeStruct((depth, batch, n_out), F32),
        compiler_params=pltpu.CompilerParams(
            dimension_semantics=("arbitrary", "arbitrary"), vmem_limit_bytes=VMEM_LIMIT),
        name="adaln_mod",
    )(c, w_mod, b_mod.reshape(depth, 1, n_out))


def _trig_kernel(lam_inits, pos_ref, inv_ref, dl_ref, cs_ref, lam_ref):
    ang = pos_ref[...].astype(F32) * inv_ref[...]
    lane = lax.broadcasted_iota(jnp.int32, ang.shape, 1)
    cs_ref[...] = jnp.where((lane & (ROPE_DIM - 1)) < ROPE_HALF, jnp.cos(ang), jnp.sin(ang))
    for l, lam_init in enumerate(lam_inits):
        lp = dl_ref[l]
        a = jnp.sum(lp[0:1] * lp[1:2], axis=-1, keepdims=True)
        b = jnp.sum(lp[2:3] * lp[3:4], axis=-1, keepdims=True)
        lam_ref[l] = jnp.broadcast_to(jnp.exp(a) - jnp.exp(b) + lam_init, (1, LANES))


def _trig_call(positions, diff_lambda, lam_inits):
    batch, seq = positions.shape
    depth = diff_lambda.shape[0]
    inv = ROPE_THETA ** (-jnp.arange(0, ROPE_DIM, 2, dtype=F32) / ROPE_DIM)
    n_rows = batch * seq * ROPE_DIM // LANES
    pos_rep = jnp.broadcast_to(positions[..., None], (batch, seq, ROPE_DIM)).reshape(n_rows, LANES)
    inv_row = jnp.tile(jnp.concatenate([inv, inv]), LANES // ROPE_DIM)[None, :]
    cs, lam = pl.pallas_call(
        functools.partial(_trig_kernel, lam_inits),
        out_shape=(jax.ShapeDtypeStruct((n_rows, LANES), F32),
                   jax.ShapeDtypeStruct((depth, 1, LANES), F32)),
        name="rope_trig_lambda",
    )(pos_rep, inv_row, diff_lambda)
    return cs.reshape(batch, seq, ROPE_DIM), lam


def _rope_expansion():
    expand = np.zeros((ROPE_DIM, 3 * LANES), np.float32)
    base = np.zeros((1, 3 * LANES), np.float32)
    base[0, :LANES] = 1.0
    for h0 in range(0, LANES, HEAD_DIM):
        base[0, h0:h0 + ROPE_DIM] = 0.0
        for j in range(ROPE_HALF):
            expand[j, h0 + j] = 1.0
            expand[j, h0 + ROPE_HALF + j] = 1.0
            expand[ROPE_HALF + j, LANES + h0 + j] = -1.0
            expand[ROPE_HALF + j, 2 * LANES + h0 + ROPE_HALF + j] = 1.0
    return jnp.asarray(expand, BF16), jnp.asarray(base)


def _adaln(x, norm_gain, scale, shift):
    ms = jnp.mean(x * x, axis=-1, keepdims=True)
    return (x * lax.rsqrt(ms + EPS)) * (norm_gain * (1.0 + scale)) + shift


def _inproj_kernel(x_ref, nm_ref, sc_ref, sh_ref, w_ref, gsum_ref, gains_ref, cs_ref, exp_ref, base_ref,
                   o_ref):
    h = _adaln(x_ref[...], nm_ref[...], sc_ref[...], sh_ref[...]).astype(BF16)
    tab = base_ref[...]
    for piece in _bf16_pieces(cs_ref[...], N_PIECES):
        tab = tab + _dot(piece.astype(BF16), exp_ref[...])
    rope_c, rope_n, rope_p = tab[:, :LANES], tab[:, LANES:2 * LANES], tab[:, 2 * LANES:]
    group_mean = gsum_ref[...]
    n_chunks = w_ref.shape[1] // MXU_DIM
    gain_row = {0: 0, 1: 1, 3: 2, 4: 3}

    def project(c):
        return _dot(h, w_ref[:, c * MXU_DIM:(c + 1) * MXU_DIM])

    def finish(c, y):
        kind = c // 2
        if kind not in gain_row:
            o_ref[:, c * MXU_DIM:(c + 1) * MXU_DIM] = y.astype(BF16)
            return
        inv_rms = lax.rsqrt(_dot((y * y).astype(BF16), group_mean) + EPS)
        r = gain_row[kind]
        y = y * gains_ref[r:r + 1, :]
        for half in range(MXU_DIM // LANES):
            lanes = slice(half * LANES, (half + 1) * LANES)
            yh = y[:, lanes]
            yh = (yh * rope_c + pltpu.roll(yh, LANES - ROPE_HALF, 1) * rope_n
                  + pltpu.roll(yh, ROPE_HALF, 1) * rope_p)
            lo = c * MXU_DIM + half * LANES
            o_ref[:, lo:lo + LANES] = (yh * inv_rms[:, lanes]).astype(BF16)

    upcoming = project(0)
    for c in range(n_chunks):
        y = upcoming
        if c + 1 < n_chunks:
            upcoming = project(c + 1)
        finish(c, y)


def _mod_spec(d_model, layer, k):
    return pl.BlockSpec((None, None, None, 1, d_model), lambda b, i: (layer, b, k, 0, 0))


def _layer_spec(array, layer):
    return pl.BlockSpec((None,) + array.shape[1:], lambda b, i: (layer, 0, 0))


def _const_spec(array):
    return pl.BlockSpec(array.shape, lambda b, i: (0,) * array.ndim)


def _inproj_call(x, mod5, layer, norm_gain, w_in, group_mean, gains, cs, rope_expand, rope_base):
    batch, seq, d_model = x.shape
    n_out = w_in.shape[2]
    return pl.pallas_call(
        _inproj_kernel,
        grid=(batch, seq // ROW_TILE),
        in_specs=[
            pl.BlockSpec((None, ROW_TILE, d_model), lambda b, i: (b, i, 0)),
            _layer_spec(norm_gain, layer),
            _mod_spec(d_model, layer, 1),
            _mod_spec(d_model, layer, 0),
            _layer_spec(w_in, layer),
            _const_spec(group_mean),
            _layer_spec(gains, layer),
            pl.BlockSpec((None, ROW_TILE, ROPE_DIM), lambda b, i: (b, i, 0)),
            _const_spec(rope_expand),
            _const_spec(rope_base),
        ],
        out_specs=pl.BlockSpec((None, ROW_TILE, n_out), lambda b, i: (b, i, 0)),
        out_shape=jax.ShapeDtypeStruct((batch, seq, n_out), BF16),
        compiler_params=pltpu.CompilerParams(
            dimension_semantics=("parallel", "parallel"), vmem_limit_bytes=VMEM_LIMIT),
        name="in_projection",
    )(x, norm_gain, mod5, mod5, w_in, group_mean, gains, cs, rope_expand, rope_base)


MASKED = -1e30
ONES_ROWS = 16
PIPE_SLOTS = 2
GROUPS = 2


def _slot(i, half):
    return 2 * (i % PIPE_SLOTS) + half


def _slot_scratch(seq):
    return ([pltpu.VMEM((seq, ATT_TILE), F32) for _ in range(2 * PIPE_SLOTS)]
            + [pltpu.VMEM((seq, ATT_TILE), BF16) for _ in range(2 * PIPE_SLOTS)])


def _store_scores(s_ref, c, s_t):
    n_past = c * ATT_TILE
    own = s_t[n_past:, :]
    key = lax.broadcasted_iota(jnp.int32, own.shape, 0)
    query = lax.broadcasted_iota(jnp.int32, own.shape, 1)
    own = jnp.where(key <= query, own, -jnp.inf)
    s_ref[n_past:n_past + ATT_TILE, :] = own
    m = jnp.max(own, axis=0, keepdims=True)
    if c > 0:
        past = s_t[:n_past, :]
        s_ref[:n_past, :] = past
        m = jnp.maximum(m, jnp.max(past, axis=0, keepdims=True))
    return m


def _store_numerators(s_ref, p_ref, m, c):
    n_kv = (c + 1) * ATT_TILE
    p_ref[:n_kv, :] = jnp.exp2(s_ref[:n_kv, :] - m).astype(BF16)


def _weighted_values(p_ref, vt_ref, c, n_rows):
    n_kv = (c + 1) * ATT_TILE
    o_aug = _dot(vt_ref[:, :n_kv], p_ref[:n_kv, :])
    return o_aug[:n_rows] * (1.0 / o_aug[n_rows:n_rows + 1])


def _tile_order(n_tiles):
    return list(range(0, n_tiles, 2)) + list(range(1, n_tiles, 2))[::-1]


def _staged(groups, n_tiles, s_refs, p_refs):
    for group in groups:
        group["prepare"]()
    jobs = [(group, c) for group in groups for c in _tile_order(n_tiles)]
    job = lambda i: jobs[i] if 0 <= i < len(jobs) else None
    maxes, ahead = {}, {}
    for i in range(-2, len(jobs) + 1):
        if job(i + 2) is not None and "ahead" in job(i + 2)[0]:
            group, c = job(i + 2)
            for half in range(2):
                ahead[i + 2, half] = group["ahead"](c, half)
        for half in range(2):
            if job(i + 1) is not None:
                group, c = job(i + 1)
                maxes[i + 1, half] = group["scores"](c, half, ahead.pop((i + 1, half), None),
                                                     s_refs[_slot(i + 1, half)])
            if job(i) is not None:
                _store_numerators(s_refs[_slot(i, half)], p_refs[_slot(i, half)], maxes.pop((i, half)),
                                  job(i)[1])
        if job(i - 1) is not None:
            group, c = job(i - 1)
            group["output"](c, [p_refs[_slot(i - 1, half)] for half in range(2)])


def _lane_block(ref, g):
    return ref.at[:, g * LANES:(g + 1) * LANES]


def _tile_rows(c):
    return slice(c * ATT_TILE, (c + 1) * ATT_TILE)


def _transposed(ref):
    return ref[...].astype(F32).T


def _half_rows(qt_ref, c, half):
    q_t = qt_ref[:, _tile_rows(c)]
    row = lax.broadcasted_iota(jnp.int32, q_t.shape, 0)
    mine = (row < HEAD_DIM) if half == 0 else (row >= HEAD_DIM)
    return jnp.where(mine, q_t, jnp.zeros_like(q_t))


KMEAN_ROWS = 16
GATE_ROWS = 8


def _moba_prepare(q_ref, k_ref, v_ref, qt_ref, kaug_ref, kmean_ref, vt_ref):
    seq = k_ref.shape[0]
    n_blocks = seq // MOBA_BLOCK
    kaug_ref[:, :LANES] = k_ref[...]
    row_block = lax.broadcasted_iota(jnp.int32, (seq, LANES), 0) // MOBA_BLOCK
    lane = lax.broadcasted_iota(jnp.int32, (seq, LANES), 1)
    kaug_ref[:, LANES:] = jnp.where(row_block == lane, 1.0, 0.0).astype(BF16)
    means = [jnp.mean(k_ref[_tile_rows(j), :].astype(F32), axis=0, keepdims=True) for j in range(n_blocks)]
    means = jnp.concatenate(means + [jnp.zeros((KMEAN_ROWS - n_blocks, LANES), F32)], axis=0)
    hi = means.astype(BF16)
    lo = (means - hi.astype(F32)).astype(BF16)
    kmean_ref[...] = jnp.concatenate([hi, lo], axis=1)
    qt_ref[...] = _transposed(q_ref).astype(BF16)
    v_t = _transposed(v_ref)
    for head in range(LANES // HEAD_DIM):
        vt_ref[head, :HEAD_DIM, :] = v_t[head * HEAD_DIM:(head + 1) * HEAD_DIM].astype(BF16)
        vt_ref[head, HEAD_DIM:, :] = jnp.ones((ONES_ROWS, seq), BF16)


def _moba_block_bias(c, head, qt_ref, kmean_ref):
    if c <= MOBA_TOPK:
        return None
    qh_t = _half_rows(qt_ref, c, head)
    gate = _dot(kmean_ref[...], jnp.concatenate([qh_t, qh_t], axis=0))[:GATE_ROWS, :]
    blk = lax.broadcasted_iota(jnp.int32, gate.shape, 0)
    cnt = jnp.zeros(gate.shape, jnp.int32)
    for m in range(c):
        gm = gate[m:m + 1, :]
        cnt = cnt + jnp.where(blk > m, jnp.where(gm >= gate, 1, 0), jnp.where(gm > gate, 1, 0))
    bias_t = jnp.where(blk < c, jnp.where(cnt < MOBA_TOPK, 0.0, MASKED), 0.0)
    bias_t = jnp.concatenate([bias_t, jnp.zeros((LANES - GATE_ROWS, gate.shape[1]), F32)], axis=0)
    return bias_t.astype(BF16)


def _moba_scores(c, head, bias_t, s_ref, qt_ref, k_ref, kaug_ref):
    n_kv = (c + 1) * ATT_TILE
    qh_t = _half_rows(qt_ref, c, head)
    if bias_t is None:
        s_t = _dot(k_ref[:n_kv, :], qh_t)
    else:
        s_t = _dot(kaug_ref[:n_kv, :], jnp.concatenate([qh_t, bias_t], axis=0))
    return _store_scores(s_ref, c, s_t)


def _moba_output(c, p_refs, vt_ref, gn_ref, o_ref):
    outs = []
    for head in range(LANES // HEAD_DIM):
        o_h = _weighted_values(p_refs[head], vt_ref.at[head], c, HEAD_DIM)
        ms = jnp.mean(o_h * o_h, axis=0, keepdims=True)
        outs.append(o_h * lax.rsqrt(ms + EPS))
    o_t = jnp.concatenate(outs, axis=0)
    o_ref[_tile_rows(c), :] = (o_t.T * gn_ref[...]).astype(BF16)


def _moba_kernel(q_ref, k_ref, v_ref, gn_ref, o_ref, qt_ref, kaug_ref, kmean_ref, vt_ref, *slot_refs):
    s_refs, p_refs = slot_refs[:2 * PIPE_SLOTS], slot_refs[2 * PIPE_SLOTS:]
    groups = []
    for g in range(GROUPS):
        q_g, k_g, v_g, o_g = [_lane_block(ref, g) for ref in (q_ref, k_ref, v_ref, o_ref)]
        qt_g, kaug_g, kmean_g, vt_g = qt_ref.at[g], kaug_ref.at[g], kmean_ref.at[g], vt_ref.at[g]
        groups.append(dict(
            prepare=functools.partial(_moba_prepare, q_g, k_g, v_g, qt_g, kaug_g, kmean_g, vt_g),
            ahead=functools.partial(_moba_block_bias, qt_ref=qt_g, kmean_ref=kmean_g),
            scores=functools.partial(_moba_scores, qt_ref=qt_g, k_ref=k_g, kaug_ref=kaug_g),
            output=functools.partial(_moba_output, vt_ref=vt_g, gn_ref=gn_ref, o_ref=o_g)))
    _staged(groups, k_ref.shape[0] // ATT_TILE, s_refs, p_refs)


def _moba_call(proj, out_gain, layer, width):
    batch, seq, _ = proj.shape
    n_steps = width // (GROUPS * LANES)
    col_spec = lambda c0: pl.BlockSpec((None, seq, GROUPS * LANES), lambda b, p: (b, 0, c0 + p))
    return pl.pallas_call(
        _moba_kernel,
        grid=(batch, n_steps),
        in_specs=[col_spec(0), col_spec(n_steps), col_spec(2 * n_steps),
                  pl.BlockSpec((None, 1, LANES), lambda b, p: (layer, 0, 0))],
        out_specs=col_spec(0),
        out_shape=jax.ShapeDtypeStruct((batch, seq, width), BF16),
        scratch_shapes=[pltpu.VMEM((GROUPS, LANES, seq), BF16),
                        pltpu.VMEM((GROUPS, seq, 2 * LANES), BF16),
                        pltpu.VMEM((GROUPS, KMEAN_ROWS, 2 * LANES), BF16),
                        pltpu.VMEM((GROUPS, LANES // HEAD_DIM, HEAD_DIM + ONES_ROWS, seq), BF16),
                        *_slot_scratch(seq)],
        compiler_params=pltpu.CompilerParams(
            dimension_semantics=("parallel", "parallel"), vmem_limit_bytes=VMEM_LIMIT),
        name="moba_attention",
    )(proj, proj, proj, out_gain)


def _diff_prepare(v_ref, vt_ref):
    vt_ref[:LANES, :] = _transposed(v_ref).astype(BF16)
    vt_ref[LANES:, :] = jnp.ones((ONES_ROWS, v_ref.shape[0]), BF16)


def _diff_scores(c, comp, _, s_ref, q_ref, k_ref):
    q = q_ref[_tile_rows(c), :]
    lane = lax.broadcasted_iota(jnp.int32, (1, LANES), 1)
    mine = (lane < HEAD_DIM) if comp == 0 else (lane >= HEAD_DIM)
    qh = jnp.where(mine, q, jnp.zeros_like(q))
    s_t = lax.dot_general(k_ref[:(c + 1) * ATT_TILE, :], qh, (((1,), (1,)), ((), ())),
                          preferred_element_type=F32)
    return _store_scores(s_ref, c, s_t)


def _diff_output(c, p_refs, out_scale, lam_ref, vt_ref, gn_ref, o_ref):
    a1, a2 = [_weighted_values(p_ref, vt_ref, c, LANES) for p_ref in p_refs]
    o_t = a1 - lam_ref[0:1, 0:1] * a2
    ms = jnp.mean(o_t * o_t, axis=0, keepdims=True)
    o_t = o_t * lax.rsqrt(ms + EPS)
    o_ref[_tile_rows(c), :] = ((o_t.T * gn_ref[...]) * out_scale).astype(BF16)


def _diff_kernel(out_scale, lam_ref, q_ref, k_ref, v_ref, gn_ref, o_ref, vt_ref, *slot_refs):
    s_refs, p_refs = slot_refs[:2 * PIPE_SLOTS], slot_refs[2 * PIPE_SLOTS:]
    groups = []
    for g in range(GROUPS):
        q_g, k_g, v_g, o_g = [_lane_block(ref, g) for ref in (q_ref, k_ref, v_ref, o_ref)]
        groups.append(dict(
            prepare=functools.partial(_diff_prepare, v_g, vt_ref.at[g]),
            scores=functools.partial(_diff_scores, q_ref=q_g, k_ref=k_g),
            output=functools.partial(_diff_output, out_scale=out_scale, lam_ref=lam_ref, vt_ref=vt_ref.at[g],
                                     gn_ref=gn_ref, o_ref=o_g)))
    _staged(groups, k_ref.shape[0] // ATT_TILE, s_refs, p_refs)


def _diff_call(proj, lam, layer, out_gain, width, col0, out_scale):
    batch, seq, _ = proj.shape
    n_steps = width // (GROUPS * LANES)
    c0 = col0 // (GROUPS * LANES)
    col_spec = lambda c: pl.BlockSpec((None, seq, GROUPS * LANES), lambda b, h: (b, 0, c + h))
    return pl.pallas_call(
        functools.partial(_diff_kernel, out_scale),
        grid=(batch, n_steps),
        in_specs=[pl.BlockSpec((None, 1, LANES), lambda b, h: (layer, 0, 0)),
                  col_spec(c0), col_spec(c0 + n_steps), col_spec(c0 + 2 * n_steps),
                  pl.BlockSpec((None, 1, LANES), lambda b, h: (layer, 0, 0))],
        out_specs=col_spec(0),
        out_shape=jax.ShapeDtypeStruct((batch, seq, width), BF16),
        scratch_shapes=[pltpu.VMEM((GROUPS, LANES + ONES_ROWS, seq), BF16),
                        *_slot_scratch(seq)],
        compiler_params=pltpu.CompilerParams(
            dimension_semantics=("parallel", "parallel"), vmem_limit_bytes=VMEM_LIMIT),
        name="diff_attention",
    )(lam, proj, proj, proj, out_gain)


def _mlp_kernel(om_ref, od_ref, wo_ref, x_ref, ga_ref, nf_ref, sc_ref, sh_ref, gf_ref, wg_ref, wu_ref,
                wd_ref, o_ref, act_ref):
    width = om_ref.shape[1]
    half_rows = o_ref.shape[0] // 2
    halves = [slice(r * half_rows, (r + 1) * half_rows) for r in range(2)]
    hs = []
    for rows in halves:
        y = _dot(om_ref[rows, :], wo_ref[:width, :]) + _dot(od_ref[rows, :], wo_ref[width:, :])
        x = x_ref[rows, :] + ga_ref[...] * y
        o_ref[rows, :] = x
        hs.append(_adaln(x, nf_ref[...], sc_ref[...], sh_ref[...]).astype(BF16))
    for rows, h in zip(halves, hs):
        for c in range(wg_ref.shape[1] // MXU_DIM):
            cols = slice(c * MXU_DIM, (c + 1) * MXU_DIM)
            g = _dot(h, wg_ref[:, cols])
            u = _dot(h, wu_ref[:, cols])
            act_ref[rows, cols] = ((g / (1.0 + jnp.exp(-g))) * u).astype(BF16)
        o_ref[rows, :] = o_ref[rows, :] + gf_ref[...] * _dot(act_ref[rows, :], wd_ref[...])


def _mlp_call(o_m, o_d, x, mod5, layer, norm_gain, w_out, w_gate, w_up, w_down):
    batch, seq, d_model = x.shape
    d_ff = w_gate.shape[2]
    row_spec = lambda cols: pl.BlockSpec((None, MLP_ROW_TILE, cols), lambda b, i: (b, i, 0))
    resident = lambda w: pl.BlockSpec((None,) + w.shape[1:], lambda b, i: (layer, 0, 0),
                                      pipeline_mode=pl.Buffered(1))
    return pl.pallas_call(
        _mlp_kernel,
        grid=(batch, seq // MLP_ROW_TILE),
        in_specs=[
            row_spec(o_m.shape[2]), row_spec(o_d.shape[2]),
            resident(w_out),
            row_spec(d_model),
            _mod_spec(d_model, layer, 2),
            _layer_spec(norm_gain, layer),
            _mod_spec(d_model, layer, 4),
            _mod_spec(d_model, layer, 3),
            _mod_spec(d_model, layer, 5),
            resident(w_gate), resident(w_up), resident(w_down),
        ],
        out_specs=row_spec(d_model),
        out_shape=jax.ShapeDtypeStruct((batch, seq, d_model), F32),
        scratch_shapes=[pltpu.VMEM((MLP_ROW_TILE, d_ff), BF16)],
        compiler_params=pltpu.CompilerParams(
            dimension_semantics=("parallel", "parallel"), vmem_limit_bytes=VMEM_LIMIT),
        name="out_projection_swiglu",
    )(o_m, o_d, w_out, x, mod5, norm_gain, mod5, mod5, mod5, w_gate, w_up, w_down)


def kernel(x, c, positions, w_mod, b_mod, norm_mix, w_in, moba_q_norm, moba_k_norm, moba_out_norm,
           diff_q_norm, diff_k_norm, diff_lambda, diff_subln, w_out, norm_ffn, w_gate, w_up, w_down):
    batch, seq, d_model = x.shape
    depth = w_mod.shape[0]
    moba_width = d_model // 2
    diff_width = d_model // 2
    assert seq % ROW_TILE == 0 and seq % MLP_ROW_TILE == 0 and seq % MOBA_BLOCK == 0
    assert ATT_TILE == MOBA_BLOCK
    assert seq // MOBA_BLOCK <= GATE_ROWS
    assert w_in.shape[2] == 3 * moba_width + 3 * diff_width and moba_width == 2 * MXU_DIM

    lam_inits = tuple(0.8 - 0.6 * math.exp(-0.3 * l) for l in range(depth))
    mod5 = _mod_call(c, w_mod, b_mod).reshape(depth, batch, N_MOD, 1, d_model)
    cs, lam = _trig_call(positions, diff_lambda, lam_inits)
    rope_expand, rope_base = _rope_expansion()

    head_of_lane = np.arange(MXU_DIM) // HEAD_DIM
    group_mean = jnp.asarray((head_of_lane[:, None] == head_of_lane[None, :]) / HEAD_DIM, BF16)
    reps = MXU_DIM // HEAD_DIM
    qk_scale = HEAD_DIM ** -0.5 * math.log2(math.e)
    gains = jnp.stack([jnp.tile(moba_q_norm, (1, reps)) * qk_scale, jnp.tile(moba_k_norm, (1, reps)),
                       jnp.tile(diff_q_norm, (1, reps)) * qk_scale, jnp.tile(diff_k_norm, (1, reps))],
                      axis=1)
    moba_out_gain = jnp.tile(moba_out_norm, (1, LANES // HEAD_DIM))[:, None, :]
    diff_out_gain = diff_subln[:, None, :]
    norm_mix3, norm_ffn3 = norm_mix[:, None, :], norm_ffn[:, None, :]

    w_in_b, w_out_b = w_in.astype(BF16), w_out.astype(BF16)
    w_gate_b, w_up_b, w_down_b = w_gate.astype(BF16), w_up.astype(BF16), w_down.astype(BF16)

    for l in range(depth):
        proj = _inproj_call(x, mod5, l, norm_mix3, w_in_b, group_mean, gains, cs, rope_expand, rope_base)
        o_m = _moba_call(proj, moba_out_gain, l, moba_width)
        o_d = _diff_call(proj, lam, l, diff_out_gain, diff_width, 3 * moba_width, 1.0 - lam_inits[l])
        x = _mlp_call(o_m, o_d, x, mod5, l, norm_ffn3, w_out_b, w_gate_b, w_up_b, w_down_b)
    return x
```

```python
import functools
import math

import numpy as np
import jax
import jax.numpy as jnp
from jax import lax
from jax.experimental import pallas as pl
from jax.experimental.pallas import tpu as pltpu

F32 = jnp.float32
BF16 = jnp.bfloat16

LANES = 128
MXU_DIM = 256
VMEM_LIMIT = 56 * 1024 * 1024

HEAD_DIM = 64
ROPE_DIM = HEAD_DIM // 4
ROPE_HALF = ROPE_DIM // 2
ROPE_THETA = 500000.0
MOBA_BLOCK = 256
MOBA_TOPK = 3
N_MOD = 6
EPS = 1e-6

ROW_TILE = 512
MLP_ROW_TILE = 1024
ATT_TILE = 256
MOD_COL_TILE = 1536


def _dot(a, b):
    return jnp.dot(a, b, preferred_element_type=F32)


N_PIECES = 3


def _bf16_pieces(x, n):
    pieces = []
    for _ in range(n):
        piece = x.astype(BF16).astype(F32)
        pieces.append(piece)
        x = x - piece
    return pieces


def _mod_kernel(c_ref, w_ref, b_ref, o_ref):
    c = c_ref[...]
    batch = c.shape[0]
    cond = jnp.concatenate(_bf16_pieces(c / (1.0 + jnp.exp(-c)), N_PIECES), axis=0).astype(BF16)
    w_hi, w_lo = [piece.astype(BF16) for piece in _bf16_pieces(w_ref[...], 2)]
    y_hi = _dot(cond, w_hi)
    y_lo = _dot(cond[:2 * batch], w_lo)
    o_ref[...] = ((y_hi[2 * batch:] + y_lo[batch:]) + (y_hi[batch:2 * batch] + y_lo[:batch])
                  + y_hi[:batch] + b_ref[...])


def _mod_call(c, w_mod, b_mod):
    depth, d_model, n_out = w_mod.shape
    batch = c.shape[0]
    return pl.pallas_call(
        _mod_kernel,
        grid=(depth, n_out // MOD_COL_TILE),
        in_specs=[
            pl.BlockSpec((batch, d_model), lambda l, j: (0, 0)),
            pl.BlockSpec((None, d_model, MOD_COL_TILE), lambda l, j: (l, 0, j)),
            pl.BlockSpec((None, 1, MOD_COL_TILE), lambda l, j: (l, 0, j)),
        ],
        out_specs=pl.BlockSpec((None, batch, MOD_COL_TILE), lambda l, j: (l, 0, j)),
        out_shape=jax.ShapeDtypeStruct((depth, batch, n_out), F32),
        compiler_params=pltpu.CompilerParams(
            dimension_semantics=("arbitrary", "arbitrary"), vmem_limit_bytes=VMEM_LIMIT),
        name="adaln_mod",
    )(c, w_mod, b_mod.reshape(depth, 1, n_out))


def _trig_kernel(lam_inits, pos_ref, inv_ref, dl_ref, cs_ref, lam_ref):
    ang = pos_ref[...].astype(F32) * inv_ref[...]
    lane = lax.broadcasted_iota(jnp.int32, ang.shape, 1)
    cs_ref[...] = jnp.where((lane & (ROPE_DIM - 1)) < ROPE_HALF, jnp.cos(ang), jnp.sin(ang))
    for l, lam_init in enumerate(lam_inits):
        lp = dl_ref[l]
        a = jnp.sum(lp[0:1] * lp[1:2], axis=-1, keepdims=True)
        b = jnp.sum(lp[2:3] * lp[3:4], axis=-1, keepdims=True)
        lam_ref[l] = jnp.broadcast_to(jnp.exp(a) - jnp.exp(b) + lam_init, (1, LANES))


def _trig_call(positions, diff_lambda, lam_inits):
    batch, seq = positions.shape
    depth = diff_lambda.shape[0]
    inv = ROPE_THETA ** (-jnp.arange(0, ROPE_DIM, 2, dtype=F32) / ROPE_DIM)
    n_rows = batch * seq * ROPE_DIM // LANES
    pos_rep = jnp.broadcast_to(positions[..., None], (batch, seq, ROPE_DIM)).reshape(n_rows, LANES)
    inv_row = jnp.tile(jnp.concatenate([inv, inv]), LANES // ROPE_DIM)[None, :]
    cs, lam = pl.pallas_call(
        functools.partial(_trig_kernel, lam_inits),
        out_shape=(jax.ShapeDtypeStruct((n_rows, LANES), F32),
                   jax.ShapeDtypeStruct((depth, 1, LANES), F32)),
        name="rope_trig_lambda",
    )(pos_rep, inv_row, diff_lambda)
    return cs.reshape(batch, seq, ROPE_DIM), lam


def _rope_expansion():
    expand = np.zeros((ROPE_DIM, 3 * LANES), np.float32)
    base = np.zeros((1, 3 * LANES), np.float32)
    base[0, :LANES] = 1.0
    for h0 in range(0, LANES, HEAD_DIM):
        base[0, h0:h0 + ROPE_DIM] = 0.0
        for j in range(ROPE_HALF):
            expand[j, h0 + j] = 1.0
            expand[j, h0 + ROPE_HALF + j] = 1.0
            expand[ROPE_HALF + j, LANES + h0 + j] = -1.0
            expand[ROPE_HALF + j, 2 * LANES + h0 + ROPE_HALF + j] = 1.0
    return jnp.asarray(expand, BF16), jnp.asarray(base)


def _adaln(x, norm_gain, scale, shift):
    ms = jnp.mean(x * x, axis=-1, keepdims=True)
    return (x * lax.rsqrt(ms + EPS)) * (norm_gain * (1.0 + scale)) + shift


def _inproj_kernel(x_ref, nm_ref, sc_ref, sh_ref, w_ref, gsum_ref, gains_ref, cs_ref, exp_ref, base_ref,
                   o_ref):
    h = _adaln(x_ref[...], nm_ref[...], sc_ref[...], sh_ref[...]).astype(BF16)
    tab = base_ref[...]
    for piece in _bf16_pieces(cs_ref[...], N_PIECES):
        tab = tab + _dot(piece.astype(BF16), exp_ref[...])
    rope_c, rope_n, rope_p = tab[:, :LANES], tab[:, LANES:2 * LANES], tab[:, 2 * LANES:]
    group_mean = gsum_ref[...]
    n_chunks = w_ref.shape[1] // MXU_DIM
    gain_row = {0: 0, 1: 1, 3: 2, 4: 3}

    def project(c):
        return _dot(h, w_ref[:, c * MXU_DIM:(c + 1) * MXU_DIM].astype(BF16))

    def finish(c, y):
        kind = c // 2
        if kind not in gain_row:
            o_ref[:, c * MXU_DIM:(c + 1) * MXU_DIM] = y.astype(BF16)
            return
        inv_rms = lax.rsqrt(_dot((y * y).astype(BF16), group_mean) + EPS)
        r = gain_row[kind]
        y = y * gains_ref[r:r + 1, :]
        for half in range(MXU_DIM // LANES):
            lanes = slice(half * LANES, (half + 1) * LANES)
            yh = y[:, lanes]
            yh = (yh * rope_c + pltpu.roll(yh, LANES - ROPE_HALF, 1) * rope_n
                  + pltpu.roll(yh, ROPE_HALF, 1) * rope_p)
            lo = c * MXU_DIM + half * LANES
            o_ref[:, lo:lo + LANES] = (yh * inv_rms[:, lanes]).astype(BF16)

    upcoming = project(0)
    for c in range(n_chunks):
        y = upcoming
        if c + 1 < n_chunks:
            upcoming = project(c + 1)
        finish(c, y)


def _mod_spec(d_model, layer, k):
    return pl.BlockSpec((None, None, None, 1, d_model), lambda b, i: (layer, b, k, 0, 0))


def _layer_spec(array, layer):
    return pl.BlockSpec((None,) + array.shape[1:], lambda b, i: (layer, 0, 0))


def _const_spec(array):
    return pl.BlockSpec(array.shape, lambda b, i: (0,) * array.ndim)


def _inproj_call(x, mod5, layer, norm_gain, w_in, group_mean, gains, cs, rope_expand, rope_base):
    batch, seq, d_model = x.shape
    n_out = w_in.shape[2]
    return pl.pallas_call(
        _inproj_kernel,
        grid=(batch, seq // ROW_TILE),
        in_specs=[
            pl.BlockSpec((None, ROW_TILE, d_model), lambda b, i: (b, i, 0)),
            _layer_spec(norm_gain, layer),
            _mod_spec(d_model, layer, 1),
            _mod_spec(d_model, layer, 0),
            pl.BlockSpec((None,) + w_in.shape[1:], lambda b, i: (layer, 0, 0), pipeline_mode=pl.Buffered(1)),
            _const_spec(group_mean),
            _layer_spec(gains, layer),
            pl.BlockSpec((None, ROW_TILE, ROPE_DIM), lambda b, i: (b, i, 0)),
            _const_spec(rope_expand),
            _const_spec(rope_base),
        ],
        out_specs=pl.BlockSpec((None, ROW_TILE, n_out), lambda b, i: (b, i, 0)),
        out_shape=jax.ShapeDtypeStruct((batch, seq, n_out), BF16),
        compiler_params=pltpu.CompilerParams(
            dimension_semantics=("parallel", "parallel"), vmem_limit_bytes=VMEM_LIMIT),
        name="in_projection",
    )(x, norm_gain, mod5, mod5, w_in, group_mean, gains, cs, rope_expand, rope_base)


MASKED = -1e30
ONES_ROWS = 16
PIPE_SLOTS = 2
GROUPS = 2


def _slot(i, half):
    return 2 * (i % PIPE_SLOTS) + half


def _slot_scratch(seq):
    return ([pltpu.VMEM((seq, ATT_TILE), F32) for _ in range(2 * PIPE_SLOTS)]
            + [pltpu.VMEM((seq, ATT_TILE), BF16) for _ in range(2 * PIPE_SLOTS)])


def _store_scores(s_ref, c, s_t):
    n_past = c * ATT_TILE
    own = s_t[n_past:, :]
    key = lax.broadcasted_iota(jnp.int32, own.shape, 0)
    query = lax.broadcasted_iota(jnp.int32, own.shape, 1)
    own = jnp.where(key <= query, own, -jnp.inf)
    s_ref[n_past:n_past + ATT_TILE, :] = own
    m = jnp.max(own, axis=0, keepdims=True)
    if c > 0:
        past = s_t[:n_past, :]
        s_ref[:n_past, :] = past
        m = jnp.maximum(m, jnp.max(past, axis=0, keepdims=True))
    return m


def _store_numerators(s_ref, p_ref, m, c):
    n_kv = (c + 1) * ATT_TILE
    p_ref[:n_kv, :] = jnp.exp2(s_ref[:n_kv, :] - m).astype(BF16)


def _weighted_values(p_ref, vt_ref, c, n_rows):
    n_kv = (c + 1) * ATT_TILE
    o_aug = _dot(vt_ref[:, :n_kv], p_ref[:n_kv, :])
    return o_aug[:n_rows] * (1.0 / o_aug[n_rows:n_rows + 1])


def _tile_order(n_tiles):
    return list(range(0, n_tiles, 2)) + list(range(1, n_tiles, 2))[::-1]


def _staged(groups, n_tiles, s_refs, p_refs):
    for group in groups:
        group["prepare"]()
    jobs = [(group, c) for group in groups for c in _tile_order(n_tiles)]
    job = lambda i: jobs[i] if 0 <= i < len(jobs) else None
    maxes, ahead = {}, {}
    for i in range(-2, len(jobs) + 1):
        if job(i + 2) is not None and "ahead" in job(i + 2)[0]:
            group, c = job(i + 2)
            for half in range(2):
                ahead[i + 2, half] = group["ahead"](c, half)
        for half in range(2):
            if job(i + 1) is not None:
                group, c = job(i + 1)
                maxes[i + 1, half] = group["scores"](c, half, ahead.pop((i + 1, half), None),
                                                     s_refs[_slot(i + 1, half)])
            if job(i) is not None:
                _store_numerators(s_refs[_slot(i, half)], p_refs[_slot(i, half)], maxes.pop((i, half)),
                                  job(i)[1])
        if job(i - 1) is not None:
            group, c = job(i - 1)
            group["output"](c, [p_refs[_slot(i - 1, half)] for half in range(2)])


def _lane_block(ref, g):
    return ref.at[:, g * LANES:(g + 1) * LANES]


def _tile_rows(c):
    return slice(c * ATT_TILE, (c + 1) * ATT_TILE)


def _transposed(ref):
    return ref[...].astype(F32).T


def _half_rows(qt_ref, c, half):
    q_t = qt_ref[:, _tile_rows(c)]
    row = lax.broadcasted_iota(jnp.int32, q_t.shape, 0)
    mine = (row < HEAD_DIM) if half == 0 else (row >= HEAD_DIM)
    return jnp.where(mine, q_t, jnp.zeros_like(q_t))


KMEAN_ROWS = 16
GATE_ROWS = 8


def _moba_prepare(q_ref, k_ref, v_ref, qt_ref, kaug_ref, kmean_ref, vt_ref):
    seq = k_ref.shape[0]
    n_blocks = seq // MOBA_BLOCK
    kaug_ref[:, :LANES] = k_ref[...]
    row_block = lax.broadcasted_iota(jnp.int32, (seq, LANES), 0) // MOBA_BLOCK
    lane = lax.broadcasted_iota(jnp.int32, (seq, LANES), 1)
    kaug_ref[:, LANES:] = jnp.where(row_block == lane, 1.0, 0.0).astype(BF16)
    means = [jnp.mean(k_ref[_tile_rows(j), :].astype(F32), axis=0, keepdims=True) for j in range(n_blocks)]
    means = jnp.concatenate(means + [jnp.zeros((KMEAN_ROWS - n_blocks, LANES), F32)], axis=0)
    hi = means.astype(BF16)
    lo = (means - hi.astype(F32)).astype(BF16)
    kmean_ref[...] = jnp.concatenate([hi, lo], axis=1)
    qt_ref[...] = _transposed(q_ref).astype(BF16)
    v_t = _transposed(v_ref)
    for head in range(LANES // HEAD_DIM):
        vt_ref[head, :HEAD_DIM, :] = v_t[head * HEAD_DIM:(head + 1) * HEAD_DIM].astype(BF16)
        vt_ref[head, HEAD_DIM:, :] = jnp.ones((ONES_ROWS, seq), BF16)


def _moba_block_bias(c, head, qt_ref, kmean_ref):
    if c <= MOBA_TOPK:
        return None
    qh_t = _half_rows(qt_ref, c, head)
    gate = _dot(kmean_ref[...], jnp.concatenate([qh_t, qh_t], axis=0))[:GATE_ROWS, :]
    blk = lax.broadcasted_iota(jnp.int32, gate.shape, 0)
    cnt = jnp.zeros(gate.shape, jnp.int32)
    for m in range(c):
        gm = gate[m:m + 1, :]
        cnt = cnt + jnp.where(blk > m, jnp.where(gm >= gate, 1, 0), jnp.where(gm > gate, 1, 0))
    bias_t = jnp.where(blk < c, jnp.where(cnt < MOBA_TOPK, 0.0, MASKED), 0.0)
    bias_t = jnp.concatenate([bias_t, jnp.zeros((LANES - GATE_ROWS, gate.shape[1]), F32)], axis=0)
    return bias_t.astype(BF16)


def _moba_scores(c, head, bias_t, s_ref, qt_ref, k_ref, kaug_ref):
    n_kv = (c + 1) * ATT_TILE
    qh_t = _half_rows(qt_ref, c, head)
    if bias_t is None:
        s_t = _dot(k_ref[:n_kv, :], qh_t)
    else:
        s_t = _dot(kaug_ref[:n_kv, :], jnp.concatenate([qh_t, bias_t], axis=0))
    return _store_scores(s_ref, c, s_t)


def _moba_output(c, p_refs, vt_ref, gn_ref, o_ref):
    outs = []
    for head in range(LANES // HEAD_DIM):
        o_h = _weighted_values(p_refs[head], vt_ref.at[head], c, HEAD_DIM)
        ms = jnp.mean(o_h * o_h, axis=0, keepdims=True)
        outs.append(o_h * lax.rsqrt(ms + EPS))
    o_t = jnp.concatenate(outs, axis=0)
    o_ref[_tile_rows(c), :] = (o_t.T * gn_ref[...]).astype(BF16)


def _moba_kernel(q_ref, k_ref, v_ref, gn_ref, o_ref, qt_ref, kaug_ref, kmean_ref, vt_ref, *slot_refs):
    s_refs, p_refs = slot_refs[:2 * PIPE_SLOTS], slot_refs[2 * PIPE_SLOTS:]
    groups = []
    for g in range(GROUPS):
        q_g, k_g, v_g, o_g = [_lane_block(ref, g) for ref in (q_ref, k_ref, v_ref, o_ref)]
        qt_g, kaug_g, kmean_g, vt_g = qt_ref.at[g], kaug_ref.at[g], kmean_ref.at[g], vt_ref.at[g]
        groups.append(dict(
            prepare=functools.partial(_moba_prepare, q_g, k_g, v_g, qt_g, kaug_g, kmean_g, vt_g),
            ahead=functools.partial(_moba_block_bias, qt_ref=qt_g, kmean_ref=kmean_g),
            scores=functools.partial(_moba_scores, qt_ref=qt_g, k_ref=k_g, kaug_ref=kaug_g),
            output=functools.partial(_moba_output, vt_ref=vt_g, gn_ref=gn_ref, o_ref=o_g)))
    _staged(groups, k_ref.shape[0] // ATT_TILE, s_refs, p_refs)


def _moba_call(proj, out_gain, layer, width):
    batch, seq, _ = proj.shape
    n_steps = width // (GROUPS * LANES)
    col_spec = lambda c0: pl.BlockSpec((None, seq, GROUPS * LANES), lambda b, p: (b, 0, c0 + p))
    return pl.pallas_call(
        _moba_kernel,
        grid=(batch, n_steps),
        in_specs=[col_spec(0), col_spec(n_steps), col_spec(2 * n_steps),
                  pl.BlockSpec((None, 1, LANES), lambda b, p: (layer, 0, 0))],
        out_specs=col_spec(0),
        out_shape=jax.ShapeDtypeStruct((batch, seq, width), BF16),
        scratch_shapes=[pltpu.VMEM((GROUPS, LANES, seq), BF16),
                        pltpu.VMEM((GROUPS, seq, 2 * LANES), BF16),
                        pltpu.VMEM((GROUPS, KMEAN_ROWS, 2 * LANES), BF16),
                        pltpu.VMEM((GROUPS, LANES // HEAD_DIM, HEAD_DIM + ONES_ROWS, seq), BF16),
                        *_slot_scratch(seq)],
        compiler_params=pltpu.CompilerParams(
            dimension_semantics=("parallel", "parallel"), vmem_limit_bytes=VMEM_LIMIT),
        name="moba_attention",
    )(proj, proj, proj, out_gain)


def _diff_prepare(v_ref, vt_ref):
    vt_ref[:LANES, :] = _transposed(v_ref).astype(BF16)
    vt_ref[LANES:, :] = jnp.ones((ONES_ROWS, v_ref.shape[0]), BF16)


def _diff_scores(c, comp, _, s_ref, q_ref, k_ref):
    q = q_ref[_tile_rows(c), :]
    lane = lax.broadcasted_iota(jnp.int32, (1, LANES), 1)
    mine = (lane < HEAD_DIM) if comp == 0 else (lane >= HEAD_DIM)
    qh = jnp.where(mine, q, jnp.zeros_like(q))
    s_t = lax.dot_general(k_ref[:(c + 1) * ATT_TILE, :], qh, (((1,), (1,)), ((), ())),
                          preferred_element_type=F32)
    return _store_scores(s_ref, c, s_t)


def _diff_output(c, p_refs, out_scale, lam_ref, vt_ref, gn_ref, o_ref):
    a1, a2 = [_weighted_values(p_ref, vt_ref, c, LANES) for p_ref in p_refs]
    o_t = a1 - lam_ref[0:1, 0:1] * a2
    ms = jnp.mean(o_t * o_t, axis=0, keepdims=True)
    o_t = o_t * lax.rsqrt(ms + EPS)
    o_ref[_tile_rows(c), :] = ((o_t.T * gn_ref[...]) * out_scale).astype(BF16)


def _diff_kernel(out_scale, lam_ref, q_ref, k_ref, v_ref, gn_ref, o_ref, vt_ref, *slot_refs):
    s_refs, p_refs = slot_refs[:2 * PIPE_SLOTS], slot_refs[2 * PIPE_SLOTS:]
    groups = []
    for g in range(GROUPS):
        q_g, k_g, v_g, o_g = [_lane_block(ref, g) for ref in (q_ref, k_ref, v_ref, o_ref)]
        groups.append(dict(
            prepare=functools.partial(_diff_prepare, v_g, vt_ref.at[g]),
            scores=functools.partial(_diff_scores, q_ref=q_g, k_ref=k_g),
            output=functools.partial(_diff_output, out_scale=out_scale, lam_ref=lam_ref, vt_ref=vt_ref.at[g],
                                     gn_ref=gn_ref, o_ref=o_g)))
    _staged(groups, k_ref.shape[0] // ATT_TILE, s_refs, p_refs)


def _diff_call(proj, lam, layer, out_gain, width, col0, out_scale):
    batch, seq, _ = proj.shape
    n_steps = width // (GROUPS * LANES)
    c0 = col0 // (GROUPS * LANES)
    col_spec = lambda c: pl.BlockSpec((None, seq, GROUPS * LANES), lambda b, h: (b, 0, c + h))
    return pl.pallas_call(
        functools.partial(_diff_kernel, out_scale),
        grid=(batch, n_steps),
        in_specs=[pl.BlockSpec((None, 1, LANES), lambda b, h: (layer, 0, 0)),
                  col_spec(c0), col_spec(c0 + n_steps), col_spec(c0 + 2 * n_steps),
                  pl.BlockSpec((None, 1, LANES), lambda b, h: (layer, 0, 0))],
        out_specs=col_spec(0),
        out_shape=jax.ShapeDtypeStruct((batch, seq, width), BF16),
        scratch_shapes=[pltpu.VMEM((GROUPS, LANES + ONES_ROWS, seq), BF16),
                        *_slot_scratch(seq)],
        compiler_params=pltpu.CompilerParams(
            dimension_semantics=("parallel", "parallel"), vmem_limit_bytes=VMEM_LIMIT),
        name="diff_attention",
    )(lam, proj, proj, proj, out_gain)


def _mlp_kernel(om_ref, od_ref, wo_ref, x_ref, ga_ref, nf_ref, sc_ref, sh_ref, gf_ref, wg_ref, wu_ref,
                wd_ref, o_ref, act_ref):
    width = om_ref.shape[1]
    half_rows = o_ref.shape[0] // 2
    halves = [slice(r * half_rows, (r + 1) * half_rows) for r in range(2)]
    hs = []
    for rows in halves:
        y = _dot(om_ref[rows, :], wo_ref[:width, :]) + _dot(od_ref[rows, :], wo_ref[width:, :])
        x = x_ref[rows, :] + ga_ref[...] * y
        o_ref[rows, :] = x
        hs.append(_adaln(x, nf_ref[...], sc_ref[...], sh_ref[...]).astype(BF16))
    for rows, h in zip(halves, hs):
        for c in range(wg_ref.shape[1] // MXU_DIM):
            cols = slice(c * MXU_DIM, (c + 1) * MXU_DIM)
            g = _dot(h, wg_ref[:, cols])
            u = _dot(h, wu_ref[:, cols])
            act_ref[rows, cols] = ((g / (1.0 + jnp.exp(-g))) * u).astype(BF16)
        o_ref[rows, :] = o_ref[rows, :] + gf_ref[...] * _dot(act_ref[rows, :], wd_ref[...])


def _mlp_call(o_m, o_d, x, mod5, layer, norm_gain, w_out, w_gate, w_up, w_down):
    batch, seq, d_model = x.shape
    d_ff = w_gate.shape[2]
    row_spec = lambda cols: pl.BlockSpec((None, MLP_ROW_TILE, cols), lambda b, i: (b, i, 0))
    resident = lambda w: pl.BlockSpec((None,) + w.shape[1:], lambda b, i: (layer, 0, 0),
                                      pipeline_mode=pl.Buffered(1))
    return pl.pallas_call(
        _mlp_kernel,
        grid=(batch, seq // MLP_ROW_TILE),
        in_specs=[
            row_spec(o_m.shape[2]), row_spec(o_d.shape[2]),
            resident(w_out),
            row_spec(d_model),
            _mod_spec(d_model, layer, 2),
            _layer_spec(norm_gain, layer),
            _mod_spec(d_model, layer, 4),
            _mod_spec(d_model, layer, 3),
            _mod_spec(d_model, layer, 5),
            resident(w_gate), resident(w_up), resident(w_down),
        ],
        out_specs=row_spec(d_model),
        out_shape=jax.ShapeDtypeStruct((batch, seq, d_model), F32),
        scratch_shapes=[pltpu.VMEM((MLP_ROW_TILE, d_ff), BF16)],
        compiler_params=pltpu.CompilerParams(
            dimension_semantics=("parallel", "parallel"), vmem_limit_bytes=VMEM_LIMIT),
        name="out_projection_swiglu",
    )(o_m, o_d, w_out, x, mod5, norm_gain, mod5, mod5, mod5, w_gate, w_up, w_down)


def kernel(x, c, positions, w_mod, b_mod, norm_mix, w_in, moba_q_norm, moba_k_norm, moba_out_norm,
           diff_q_norm, diff_k_norm, diff_lambda, diff_subln, w_out, norm_ffn, w_gate, w_up, w_down):
    batch, seq, d_model = x.shape
    depth = w_mod.shape[0]
    moba_width = d_model // 2
    diff_width = d_model // 2
    assert seq % ROW_TILE == 0 and seq % MLP_ROW_TILE == 0 and seq % MOBA_BLOCK == 0
    assert ATT_TILE == MOBA_BLOCK
    assert seq // MOBA_BLOCK <= GATE_ROWS
    assert w_in.shape[2] == 3 * moba_width + 3 * diff_width and moba_width == 2 * MXU_DIM

    lam_inits = tuple(0.8 - 0.6 * math.exp(-0.3 * l) for l in range(depth))
    mod5 = _mod_call(c, w_mod, b_mod).reshape(depth, batch, N_MOD, 1, d_model)
    cs, lam = _trig_call(positions, diff_lambda, lam_inits)
    rope_expand, rope_base = _rope_expansion()

    head_of_lane = np.arange(MXU_DIM) // HEAD_DIM
    group_mean = jnp.asarray((head_of_lane[:, None] == head_of_lane[None, :]) / HEAD_DIM, BF16)
    reps = MXU_DIM // HEAD_DIM
    qk_scale = HEAD_DIM ** -0.5 * math.log2(math.e)
    gains = jnp.stack([jnp.tile(moba_q_norm, (1, reps)) * qk_scale, jnp.tile(moba_k_norm, (1, reps)),
                       jnp.tile(diff_q_norm, (1, reps)) * qk_scale, jnp.tile(diff_k_norm, (1, reps))],
                      axis=1)
    moba_out_gain = jnp.tile(moba_out_norm, (1, LANES // HEAD_DIM))[:, None, :]
    diff_out_gain = diff_subln[:, None, :]
    norm_mix3, norm_ffn3 = norm_mix[:, None, :], norm_ffn[:, None, :]

    w_out_b = w_out.astype(BF16)
    w_gate_b, w_up_b, w_down_b = w_gate.astype(BF16), w_up.astype(BF16), w_down.astype(BF16)

    for l in range(depth):
        proj = _inproj_call(x, mod5, l, norm_mix3, w_in, group_mean, gains, cs, rope_expand, rope_base)
        o_m = _moba_call(proj, moba_out_gain, l, moba_width)
        o_d = _diff_call(proj, lam, l, diff_out_gain, diff_width, 3 * moba_width, 1.0 - lam_inits[l])
        x = _mlp_call(o_m, o_d, x, mod5, l, norm_ffn3, w_out_b, w_gate_b, w_up_b, w_down_b)
    return x
```
